```python
import jax
import jax.numpy as jnp
from jax import lax
import numpy as np

D_MODEL = 1024
BATCH = 8
SEQ = 4096
DEPTH = 4
DEC_BATCH = 4
DEC_SEQ = 4096
PAST_LEN = 128

EPS = 1e-6
NEG_INF = -1e30
ROPE_BASE = 10000.0
N_BRANCHES = 4
BRANCH_WIDTH = 512

LRU_WIDTH = 512
LRU_BLOCKS = 8
LRU_BLOCK = LRU_WIDTH // LRU_BLOCKS
CONV_WIDTH = 4
CONV_LEFT = 2
LRU_C = 8.0

MLA_HEADS = 8
MLA_NOPE = 64
MLA_ROPE = 32
MLA_V = 64
MLA_Q_RANK = 256
MLA_KV_RANK = 128
Q_BLOCK = 128

DIL_PAIRS = ((128, 1), (512, 4), (2048, 16))
DIL_HEADS_PER_GROUP = 8
DIL_HEADS = DIL_HEADS_PER_GROUP * len(DIL_PAIRS)
DIL_HEAD_DIM = 64
BAND_BLOCK = 64

RET_HEADS = 4
RET_QK = 128
RET_V = 128
RET_CHUNK = 128

N_EXPERTS = 16
EC_FACTOR = 2
D_EXPERT = 1024

SPLIT_SIZES = (
    LRU_WIDTH, LRU_WIDTH,
    MLA_Q_RANK, MLA_KV_RANK, MLA_ROPE,
    DIL_HEADS * DIL_HEAD_DIM, DIL_HEADS * DIL_HEAD_DIM, DIL_HEADS * DIL_HEAD_DIM,
    RET_HEADS * RET_QK, RET_HEADS * RET_QK, RET_HEADS * RET_V, RET_HEADS * RET_V,
    N_BRANCHES * D_MODEL,
)
IN_COLS = sum(SPLIT_SIZES)

kernel_name = "hybrid_bidir_lru_mla_dilated_retention_ecmoe"


def rms_norm(x, g):
    xf = x.astype(jnp.float32)
    y = xf * lax.rsqrt(jnp.mean(xf * xf, axis=-1, keepdims=True) + EPS)
    return (y * g.astype(jnp.float32)).astype(x.dtype)


def rope(x, pos):
    half = x.shape[-1] // 2
    inv = ROPE_BASE ** (-jnp.arange(half, dtype=jnp.float32) / half)
    ang = pos.astype(jnp.float32)[:, None] * inv[None, :]
    cos = jnp.cos(ang)[:, None, :]
    sin = jnp.sin(ang)[:, None, :]
    x1 = x[..., :half].astype(jnp.float32)
    x2 = x[..., half:].astype(jnp.float32)
    return jnp.concatenate([x1 * cos - x2 * sin, x1 * sin + x2 * cos], axis=-1).astype(x.dtype)


def alibi_slopes(n):
    return jnp.asarray([2.0 ** (-8.0 * (h + 1) / n) for h in range(n)], dtype=jnp.float32)


def split_cols(z):
    parts, off = [], 0
    for s in SPLIT_SIZES:
        parts.append(z[..., off:off + s])
        off += s
    return parts


def lru_combine(c1, c2):
    a1, b1 = c1
    a2, b2 = c2
    return a1 * a2, a2 * b1 + b2


def rglru_branch(xa, ga, conv_w, conv_b, gate_w, gate_b, lam):
    B, T, W = xa.shape
    xp = jnp.pad(xa, ((0, 0), (CONV_LEFT, CONV_WIDTH - 1 - CONV_LEFT), (0, 0)))
    xc = conv_b + sum(xp[:, k:k + T] * conv_w[k] for k in range(CONV_WIDTH))
    xb = xc.reshape(B, T, LRU_BLOCKS, LRU_BLOCK)
    gl = jnp.einsum('btni,dgnij->dgbtnj', xb, gate_w).reshape(2, 2, B, T, W)
    gates = jax.nn.sigmoid((gl + gate_b[:, :, None, None, :]).astype(jnp.float32))
    r, i = gates[:, 0], gates[:, 1]
    log_a = -LRU_C * r * jax.nn.softplus(-lam.astype(jnp.float32))[:, None, None, :]
    a = jnp.exp(log_a)
    b = jnp.sqrt(-jnp.expm1(2.0 * log_a)) * i * xc.astype(jnp.float32)[None]
    h_fwd = lax.associative_scan(lru_combine, (a[0], b[0]), axis=1)[1]
    h_bwd = lax.associative_scan(lru_combine, (a[1], b[1]), axis=1, reverse=True)[1]
    return (jax.nn.gelu(ga.astype(jnp.float32)) * (h_fwd + h_bwd)).astype(xa.dtype)


def mla_branch(c_q, c_kv, k_r, q_norm, kv_norm, w_uq, w_ukv, pos):
    B, T, _ = c_q.shape
    d_qk = MLA_NOPE + MLA_ROPE
    q = jnp.einsum('btr,rhd->bthd', rms_norm(c_q, q_norm), w_uq)
    kv = jnp.einsum('btr,rhd->bthd', rms_norm(c_kv, kv_norm), w_ukv)
    q = jnp.concatenate([q[..., :MLA_NOPE], rope(q[..., MLA_NOPE:], pos)], axis=-1)
    k_rope = jnp.broadcast_to(rope(k_r[:, :, None, :], pos), (B, T, MLA_HEADS, MLA_ROPE))
    k = jnp.concatenate([kv[..., :MLA_NOPE], k_rope], axis=-1)
    v = kv[..., MLA_NOPE:]
    scale = d_qk ** -0.5
    qb = q.reshape(B, T // Q_BLOCK, Q_BLOCK, MLA_HEADS, d_qk).transpose(1, 0, 2, 3, 4)

    def block(qi):
        s = jnp.einsum('bqhd,bkhd->bhqk', qi, k).astype(jnp.float32) * scale
        p = jax.nn.softmax(s, axis=-1)
        return jnp.einsum('bhqk,bkhd->bqhd', p.astype(v.dtype), v)

    o = lax.map(block, qb)
    return o.transpose(1, 0, 2, 3, 4).reshape(B, T, MLA_HEADS * MLA_V)


def band_attention(q, k, v, slopes, dil, radius):
    N, L, H, hd = q.shape
    nb = -(-L // BAND_BLOCK)
    Lp = nb * BAND_BLOCK
    qp = jnp.pad(q, ((0, 0), (0, Lp - L), (0, 0), (0, 0))).reshape(N, nb, BAND_BLOCK, H, hd)
    kpad = ((0, 0), (BAND_BLOCK, Lp - L + BAND_BLOCK), (0, 0), (0, 0))
    kp = jnp.pad(k, kpad).reshape(N, nb + 2, BAND_BLOCK, H, hd)
    vp = jnp.pad(v, kpad).reshape(N, nb + 2, BAND_BLOCK, H, hd)
    kw = jnp.concatenate([kp[:, :-2], kp[:, 1:-1], kp[:, 2:]], axis=2)
    vw = jnp.concatenate([vp[:, :-2], vp[:, 1:-1], vp[:, 2:]], axis=2)
    blk = jnp.arange(nb)[:, None] * BAND_BLOCK
    qpos = blk + jnp.arange(BAND_BLOCK)[None, :]
    kpos = blk - BAND_BLOCK + jnp.arange(3 * BAND_BLOCK)[None, :]
    dist = jnp.abs(qpos[:, :, None] - kpos[:, None, :])
    valid = (dist <= radius) & (kpos[:, None, :] >= 0) & (kpos[:, None, :] < L)
    s = jnp.einsum('nbqhd,nbkhd->nbhqk', qp, kw).astype(jnp.float32) * (hd ** -0.5)
    s = s - slopes[None, None, :, None, None] * (dil * dist).astype(jnp.float32)[None, :, None]
    s = jnp.where(valid[None, :, None], s, NEG_INF)
    m = jnp.max(s, axis=-1, keepdims=True)
    e = jnp.exp(s - m)
    den = jnp.sum(e, axis=-1, keepdims=True)
    o = jnp.einsum('nbhqk,nbkhd->nbqhd', (e / den).astype(v.dtype), vw)
    lse = (m + jnp.log(den))[..., 0]
    o = o.reshape(N, Lp, H, hd)[:, :L]
    lse = lse.transpose(0, 1, 3, 2).reshape(N, Lp, H)[:, :L]
    return o, lse


def dilated_branch(q, k, v):
    B, T, _, hd = q.shape
    hg = DIL_HEADS_PER_GROUP
    slopes = alibi_slopes(DIL_HEADS)
    outs, lses = [], []
    for g, (window, dil) in enumerate(DIL_PAIRS):
        lo, hi = g * hg, (g + 1) * hg

        def strided(a):
            a = a[:, :, lo:hi].reshape(B, T // dil, dil, hg, hd).transpose(0, 2, 1, 3, 4)
            return a.reshape(B * dil, T // dil, hg, hd)

        o, lse = band_attention(strided(q), strided(k), strided(v), slopes[lo:hi], dil, window // (2 * dil))
        outs.append(o.reshape(B, dil, T // dil, hg, hd).transpose(0, 2, 1, 3, 4).reshape(B, T, hg, hd))
        lses.append(lse.reshape(B, dil, T // dil, hg).transpose(0, 2, 1, 3).reshape(B, T, hg))
    o = jnp.stack(outs)
    w = jax.nn.softmax(jnp.stack(lses), axis=0)
    return jnp.einsum('gbth,gbthd->bthd', w.astype(o.dtype), o).reshape(B, T, hg * hd)


def retention_dir(q, k, v, log_gamma, strict):
    B, T, H, dk = q.shape
    dv = v.shape[-1]
    C = RET_CHUNK
    n = T // C
    qc = q.reshape(B, n, C, H, dk)
    kc = k.reshape(B, n, C, H, dk)
    vc = v.reshape(B, n, C, H, dv)
    idx = jnp.arange(C, dtype=jnp.float32)
    diff = idx[:, None] - idx[None, :]
    mask = (diff > 0) if strict else (diff >= 0)
    decay = jnp.where(mask[None], jnp.exp(log_gamma[:, None, None] * jnp.maximum(diff, 0.0)[None]), 0.0)
    inner = jnp.einsum('bnihd,bnjhd->bnhij', qc, kc) * decay[None, None]
    o_inner = jnp.einsum('bnhij,bnjhe->bnihe', inner, vc)
    zeta = jnp.exp(log_gamma[:, None] * (C - 1 - idx)[None, :])
    xi = jnp.exp(log_gamma[:, None] * (idx + 1.0)[None, :])
    chunk_decay = jnp.exp(log_gamma * C)
    kv_chunk = jnp.einsum('bnjhd,hj,bnjhe->nbhde', kc, zeta, vc)

    def step(R, kv):
        return chunk_decay[None, :, None, None] * R + kv, R

    _, r_prev = lax.scan(step, jnp.zeros((B, H, dk, dv), jnp.float32), kv_chunk.astype(jnp.float32))
    o_cross = jnp.einsum('bnihd,nbhde,hi->bnihe', qc, r_prev, xi)
    return (o_inner + o_cross).reshape(B, T, H, dv)


def retention_branch(q, k, v, g, ret_decay, pos):
    B, T, _ = q.shape
    q = rope(q.reshape(B, T, RET_HEADS, RET_QK), pos)
    k = rope(k.reshape(B, T, RET_HEADS, RET_QK), pos) * (RET_QK ** -0.5)
    v = v.reshape(B, T, RET_HEADS, RET_V)
    log_gamma = jax.nn.log_sigmoid(ret_decay.astype(jnp.float32))
    o_fwd = retention_dir(q, k, v, log_gamma[0], False)
    o_bwd = jnp.flip(retention_dir(jnp.flip(q, 1), jnp.flip(k, 1), jnp.flip(v, 1), log_gamma[1], True), 1)
    of = (o_fwd + o_bwd).astype(jnp.float32)
    mu = jnp.mean(of, axis=-1, keepdims=True)
    var = jnp.mean(jnp.square(of - mu), axis=-1, keepdims=True)
    of = ((of - mu) * lax.rsqrt(var + EPS)).reshape(B, T, RET_HEADS * RET_V)
    return (jax.nn.silu(g.astype(jnp.float32)) * of).astype(g.dtype)


def mixer(h, pos, w_in, conv_w, conv_b, lru_gate_w, lru_gate_b, lru_lambda, mla_q_norm, mla_kv_norm,
          w_uq, w_ukv, ret_decay, w_branch, w_out):
    B, T, _ = h.shape
    z = jnp.einsum('btd,dc->btc', h, w_in)
    (xa, ga, c_q, c_kv, k_r, dq, dk, dv, rq, rk, rv, rg, zg) = split_cols(z)
    ya = rglru_branch(xa, ga, conv_w, conv_b, lru_gate_w, lru_gate_b, lru_lambda)
    yb = mla_branch(c_q, c_kv, k_r, mla_q_norm, mla_kv_norm, w_uq, w_ukv, pos)
    shp = (B, T, DIL_HEADS, DIL_HEAD_DIM)
    yc = dilated_branch(dq.reshape(shp), dk.reshape(shp), dv.reshape(shp))
    yd = retention_branch(rq, rk, rv, rg, ret_decay, pos)
    gates = jax.nn.sigmoid(zg.reshape(B, T, N_BRANCHES, D_MODEL))
    merged = (gates[:, :, 0] * jnp.einsum('btw,wd->btd', ya, w_branch[0])
              + gates[:, :, 1] * jnp.einsum('btw,wd->btd', yb, w_branch[1])
              + gates[:, :, 2] * jnp.einsum('btw,wd->btd', yc, w_branch[2])
              + gates[:, :, 3] * jnp.einsum('btw,wd->btd', yd, w_branch[3]))
    return jnp.einsum('btd,de->bte', merged, w_out)


def ec_moe(h, w_router, w_gate, w_up, w_down):
    B, T, D = h.shape
    n_tok = B * T
    xt = h.reshape(n_tok, D)
    aff = jax.nn.softmax(jnp.einsum('td,de->te', xt, w_router).astype(jnp.float32), axis=-1)
    cap = EC_FACTOR * n_tok // N_EXPERTS
    g, idx = lax.top_k(aff.T, cap)
    xe = xt[idx]
    he = jax.nn.silu(jnp.einsum('ecd,edf->ecf', xe, w_gate)) * jnp.einsum('ecd,edf->ecf', xe, w_up)
    ye = jnp.einsum('ecf,efd->ecd', he, w_down) * g[..., None].astype(he.dtype)
    out = jnp.zeros_like(xt).at[idx.reshape(-1)].add(ye.reshape(-1, D))
    return out.reshape(B, T, D)


def trunk(x, norm_mix, w_in, conv_w, conv_b, lru_gate_w, lru_gate_b, lru_lambda, mla_q_norm, mla_kv_norm,
          w_uq, w_ukv, ret_decay, w_branch, w_out, norm_ffn, w_router, w_gate, w_up, w_down, norm_final):
    pos = jnp.arange(x.shape[1])
    for l in range(DEPTH):
        h = rms_norm(x, norm_mix[l])
        x = x + mixer(h, pos, w_in[l], conv_w[l], conv_b[l], lru_gate_w[l], lru_gate_b[l], lru_lambda[l],
                      mla_q_norm[l], mla_kv_norm[l], w_uq[l], w_ukv[l], ret_decay[l], w_branch[l], w_out[l])
        h = rms_norm(x, norm_ffn[l])
        x = x + ec_moe(h, w_router[l], w_gate[l], w_up[l], w_down[l])
    return rms_norm(x, norm_final)


def setup_inputs(seed: int = 0) -> dict:
    key = jax.random.key(seed)
    ks = jax.random.split(key, 24)
    f32 = jnp.float32
    L = DEPTH

    def nrm(k, shape, fan_in):
        return jax.random.normal(k, shape, f32) * (fan_in ** -0.5)

    def gain(k, shape):
        return 1.0 + 0.05 * jax.random.normal(k, shape, f32)

    u = jax.random.uniform(ks[8], (L, 2, LRU_WIDTH), f32, 0.9, 0.999)
    a0 = u ** (1.0 / LRU_C)
    gamma0 = 1.0 - 2.0 ** (-5.0 - jnp.arange(RET_HEADS, dtype=f32))
    return {
        'x_prompt': jax.random.normal(ks[0], (BATCH, SEQ, D_MODEL), f32),
        'x_sample': jax.random.normal(ks[1], (DEC_BATCH, DEC_SEQ, D_MODEL), f32),
        'norm_mix': gain(ks[2], (L, D_MODEL)),
        'w_in': nrm(ks[3], (L, D_MODEL, IN_COLS), D_MODEL),
        'conv_w': nrm(ks[4], (L, CONV_WIDTH, LRU_WIDTH), CONV_WIDTH),
        'conv_b': 0.01 * jax.random.normal(ks[5], (L, LRU_WIDTH), f32),
        'lru_gate_w': nrm(ks[6], (L, 2, 2, LRU_BLOCKS, LRU_BLOCK, LRU_BLOCK), LRU_BLOCK),
        'lru_gate_b': 0.01 * jax.random.normal(ks[7], (L, 2, 2, LRU_WIDTH), f32),
        'lru_lambda': jnp.log(a0) - jnp.log1p(-a0),
        'mla_q_norm': gain(ks[9], (L, MLA_Q_RANK)),
        'mla_kv_norm': gain(ks[10], (L, MLA_KV_RANK)),
        'w_uq': nrm(ks[11], (L, MLA_Q_RANK, MLA_HEADS, MLA_NOPE + MLA_ROPE), MLA_Q_RANK),
        'w_ukv': nrm(ks[12], (L, MLA_KV_RANK, MLA_HEADS, MLA_NOPE + MLA_V), MLA_KV_RANK),
        'ret_decay': jnp.log(gamma0) - jnp.log1p(-gamma0) + 0.05 * jax.random.normal(ks[13], (L, 2, RET_HEADS), f32),
        'w_branch': nrm(ks[14], (L, N_BRANCHES, BRANCH_WIDTH, D_MODEL), BRANCH_WIDTH),
        'w_out': nrm(ks[15], (L, D_MODEL, D_MODEL), D_MODEL),
        'norm_ffn': gain(ks[16], (L, D_MODEL)),
        'w_router': nrm(ks[17], (L, D_MODEL, N_EXPERTS), D_MODEL),
        'w_gate': nrm(ks[18], (L, N_EXPERTS, D_MODEL, D_EXPERT), D_MODEL),
        'w_up': nrm(ks[19], (L, N_EXPERTS, D_MODEL, D_EXPERT), D_MODEL),
        'w_down': nrm(ks[20], (L, N_EXPERTS, D_EXPERT, D_MODEL), D_EXPERT),
        'norm_final': gain(ks[21], (D_MODEL,)),
    }


def reference(x_prompt, x_sample, norm_mix, w_in, conv_w, conv_b, lru_gate_w, lru_gate_b, lru_lambda,
              mla_q_norm, mla_kv_norm, w_uq, w_ukv, ret_decay, w_branch, w_out, norm_ffn, w_router,
              w_gate, w_up, w_down, norm_final):
    y_prompt = trunk(x_prompt, norm_mix, w_in, conv_w, conv_b, lru_gate_w, lru_gate_b, lru_lambda,
                     mla_q_norm, mla_kv_norm, w_uq, w_ukv, ret_decay, w_branch, w_out, norm_ffn, w_router,
                     w_gate, w_up, w_down, norm_final)
    y_sample = trunk(x_sample, norm_mix, w_in, conv_w, conv_b, lru_gate_w, lru_gate_b, lru_lambda,
                     mla_q_norm, mla_kv_norm, w_uq, w_ukv, ret_decay, w_branch, w_out, norm_ffn, w_router,
                     w_gate, w_up, w_down, norm_final)
    return (y_prompt, y_sample)
```

```python
import functools

import jax
import jax.numpy as jnp
from jax import lax
from jax.experimental import pallas as pl
from jax.experimental.pallas import tpu as pltpu

D_MODEL = 1024
DEPTH = 4
EPS = 1e-6
NEG_INF = -1e30
ROPE_BASE = 10000.0
N_BRANCHES = 4
LRU_WIDTH = 512
LRU_BLOCKS = 8
LRU_BLOCK = LRU_WIDTH // LRU_BLOCKS
CONV_WIDTH = 4
CONV_LEFT = 2
LRU_C = 8.0
MLA_HEADS = 8
MLA_NOPE = 64
MLA_ROPE = 32
MLA_V = 64
MLA_Q_RANK = 256
MLA_KV_RANK = 128
Q_BLOCK = 128
DIL_PAIRS = ((128, 1), (512, 4), (2048, 16))
DIL_HEADS_PER_GROUP = 8
DIL_HEADS = DIL_HEADS_PER_GROUP * len(DIL_PAIRS)
DIL_HEAD_DIM = 64
BAND_BLOCK = 64
RET_HEADS = 4
RET_QK = 128
RET_V = 128
RET_CHUNK = 128
N_EXPERTS = 16
EC_FACTOR = 2
D_EXPERT = 1024
SPLIT_SIZES = (
    LRU_WIDTH, LRU_WIDTH,
    MLA_Q_RANK, MLA_KV_RANK, MLA_ROPE,
    DIL_HEADS * DIL_HEAD_DIM, DIL_HEADS * DIL_HEAD_DIM, DIL_HEADS * DIL_HEAD_DIM,
    RET_HEADS * RET_QK, RET_HEADS * RET_QK, RET_HEADS * RET_V, RET_HEADS * RET_V,
    N_BRANCHES * D_MODEL,
)
IN_COLS = sum(SPLIT_SIZES)

VMEM_LIMIT_BYTES = 48 * 1024 * 1024


def _norm_matmul_kernel(x_ref, g_ref, w_ref, o_ref, h_ref):
    @pl.when(pl.program_id(1) == 0)
    def _():
        x = x_ref[...]
        y = x * lax.rsqrt(jnp.mean(x * x, axis=-1, keepdims=True) + EPS)
        h_ref[...] = (y * g_ref[...]).astype(h_ref.dtype)

    o_ref[...] = jnp.dot(h_ref[...], w_ref[...], preferred_element_type=jnp.float32).astype(o_ref.dtype)


def norm_matmul(x, g, w, out_dtype, tm=512, tn=512):
    n, d = x.shape
    c = w.shape[1]
    assert n % tm == 0 and c % tn == 0
    return pl.pallas_call(
        _norm_matmul_kernel,
        grid=(n // tm, c // tn),
        in_specs=[
            pl.BlockSpec((tm, d), lambda i, j: (i, 0)),
            pl.BlockSpec((1, d), lambda i, j: (0, 0)),
            pl.BlockSpec((d, tn), lambda i, j: (0, j)),
        ],
        out_specs=pl.BlockSpec((tm, tn), lambda i, j: (i, j)),
        out_shape=jax.ShapeDtypeStruct((n, c), out_dtype),
        scratch_shapes=[pltpu.VMEM((tm, d), jnp.bfloat16)],
        compiler_params=pltpu.CompilerParams(
            dimension_semantics=("arbitrary", "arbitrary"), vmem_limit_bytes=VMEM_LIMIT_BYTES),
        name="norm_matmul",
    )(x, g.reshape(1, d), w)


def _rmsnorm_kernel(x_ref, g_ref, o_ref):
    x = x_ref[...]
    y = x * lax.rsqrt(jnp.mean(x * x, axis=-1, keepdims=True) + EPS)
    o_ref[...] = y * g_ref[...]


def rmsnorm_pallas(x, g, tm=1024):
    n, d = x.shape
    return pl.pallas_call(
        _rmsnorm_kernel,
        grid=(n // tm,),
        in_specs=[pl.BlockSpec((tm, d), lambda i: (i, 0)), pl.BlockSpec((1, d), lambda i: (0, 0))],
        out_specs=pl.BlockSpec((tm, d), lambda i: (i, 0)),
        out_shape=jax.ShapeDtypeStruct((n, d), jnp.float32),
        compiler_params=pltpu.CompilerParams(
            dimension_semantics=("arbitrary",), vmem_limit_bytes=VMEM_LIMIT_BYTES),
        name="final_rmsnorm",
    )(x, g.reshape(1, d))


def rms_norm(x, g):
    xf = x.astype(jnp.float32)
    y = xf * lax.rsqrt(jnp.mean(xf * xf, axis=-1, keepdims=True) + EPS)
    return (y * g.astype(jnp.float32)).astype(x.dtype)


def rope(x, pos):
    half = x.shape[-1] // 2
    inv = ROPE_BASE ** (-jnp.arange(half, dtype=jnp.float32) / half)
    ang = pos.astype(jnp.float32)[:, None] * inv[None, :]
    cos = jnp.cos(ang)[:, None, :]
    sin = jnp.sin(ang)[:, None, :]
    x1 = x[..., :half].astype(jnp.float32)
    x2 = x[..., half:].astype(jnp.float32)
    return jnp.concatenate([x1 * cos - x2 * sin, x1 * sin + x2 * cos], axis=-1).astype(x.dtype)


def alibi_slopes(n):
    return jnp.asarray([2.0 ** (-8.0 * (h + 1) / n) for h in range(n)], dtype=jnp.float32)


def split_cols(z):
    parts, off = [], 0
    for s in SPLIT_SIZES:
        parts.append(z[..., off:off + s])
        off += s
    return parts


def lru_combine(c1, c2):
    a1, b1 = c1
    a2, b2 = c2
    return a1 * a2, a2 * b1 + b2


def rglru_branch(xa, ga, conv_w, conv_b, gate_w, gate_b, lam):
    B, T, W = xa.shape
    xp = jnp.pad(xa, ((0, 0), (CONV_LEFT, CONV_WIDTH - 1 - CONV_LEFT), (0, 0)))
    xc = conv_b + sum(xp[:, k:k + T] * conv_w[k] for k in range(CONV_WIDTH))
    xb = xc.reshape(B, T, LRU_BLOCKS, LRU_BLOCK)
    gl = jnp.einsum('btni,dgnij->dgbtnj', xb, gate_w).reshape(2, 2, B, T, W)
    gates = jax.nn.sigmoid((gl + gate_b[:, :, None, None, :]).astype(jnp.float32))
    r, i = gates[:, 0], gates[:, 1]
    log_a = -LRU_C * r * jax.nn.softplus(-lam.astype(jnp.float32))[:, None, None, :]
    a = jnp.exp(log_a)
    b = jnp.sqrt(-jnp.expm1(2.0 * log_a)) * i * xc.astype(jnp.float32)[None]
    h_fwd = lax.associative_scan(lru_combine, (a[0], b[0]), axis=1)[1]
    h_bwd = lax.associative_scan(lru_combine, (a[1], b[1]), axis=1, reverse=True)[1]
    return (jax.nn.gelu(ga.astype(jnp.float32)) * (h_fwd + h_bwd)).astype(xa.dtype)


def mla_branch(c_q, c_kv, k_r, q_norm, kv_norm, w_uq, w_ukv, pos):
    B, T, _ = c_q.shape
    d_qk = MLA_NOPE + MLA_ROPE
    q = jnp.einsum('btr,rhd->bthd', rms_norm(c_q, q_norm), w_uq)
    kv = jnp.einsum('btr,rhd->bthd', rms_norm(c_kv, kv_norm), w_ukv)
    q = jnp.concatenate([q[..., :MLA_NOPE], rope(q[..., MLA_NOPE:], pos)], axis=-1)
    k_rope = jnp.broadcast_to(rope(k_r[:, :, None, :], pos), (B, T, MLA_HEADS, MLA_ROPE))
    k = jnp.concatenate([kv[..., :MLA_NOPE], k_rope], axis=-1)
    v = kv[..., MLA_NOPE:]
    scale = d_qk ** -0.5
    qb = q.reshape(B, T // Q_BLOCK, Q_BLOCK, MLA_HEADS, d_qk).transpose(1, 0, 2, 3, 4)

    def block(qi):
        s = jnp.einsum('bqhd,bkhd->bhqk', qi, k).astype(jnp.float32) * scale
        p = jax.nn.softmax(s, axis=-1)
        return jnp.einsum('bhqk,bkhd->bqhd', p.astype(v.dtype), v)

    o = lax.map(block, qb)
    return o.transpose(1, 0, 2, 3, 4).reshape(B, T, MLA_HEADS * MLA_V)


def band_attention(q, k, v, slopes, dil, radius):
    N, L, H, hd = q.shape
    nb = -(-L // BAND_BLOCK)
    Lp = nb * BAND_BLOCK
    qp = jnp.pad(q, ((0, 0), (0, Lp - L), (0, 0), (0, 0))).reshape(N, nb, BAND_BLOCK, H, hd)
    kpad = ((0, 0), (BAND_BLOCK, Lp - L + BAND_BLOCK), (0, 0), (0, 0))
    kp = jnp.pad(k, kpad).reshape(N, nb + 2, BAND_BLOCK, H, hd)
    vp = jnp.pad(v, kpad).reshape(N, nb + 2, BAND_BLOCK, H, hd)
    kw = jnp.concatenate([kp[:, :-2], kp[:, 1:-1], kp[:, 2:]], axis=2)
    vw = jnp.concatenate([vp[:, :-2], vp[:, 1:-1], vp[:, 2:]], axis=2)
    blk = jnp.arange(nb)[:, None] * BAND_BLOCK
    qpos = blk + jnp.arange(BAND_BLOCK)[None, :]
    kpos = blk - BAND_BLOCK + jnp.arange(3 * BAND_BLOCK)[None, :]
    dist = jnp.abs(qpos[:, :, None] - kpos[:, None, :])
    valid = (dist <= radius) & (kpos[:, None, :] >= 0) & (kpos[:, None, :] < L)
    s = jnp.einsum('nbqhd,nbkhd->nbhqk', qp, kw).astype(jnp.float32) * (hd ** -0.5)
    s = s - slopes[None, None, :, None, None] * (dil * dist).astype(jnp.float32)[None, :, None]
    s = jnp.where(valid[None, :, None], s, NEG_INF)
    m = jnp.max(s, axis=-1, keepdims=True)
    e = jnp.exp(s - m)
    den = jnp.sum(e, axis=-1, keepdims=True)
    o = jnp.einsum('nbhqk,nbkhd->nbqhd', (e / den).astype(v.dtype), vw)
    lse = (m + jnp.log(den))[..., 0]
    o = o.reshape(N, Lp, H, hd)[:, :L]
    lse = lse.transpose(0, 1, 3, 2).reshape(N, Lp, H)[:, :L]
    return o, lse


def dilated_branch(q, k, v):
    B, T, _, hd = q.shape
    hg = DIL_HEADS_PER_GROUP
    slopes = alibi_slopes(DIL_HEADS)
    outs, lses = [], []
    for g, (window, dil) in enumerate(DIL_PAIRS):
        lo, hi = g * hg, (g + 1) * hg

        def strided(a):
            a = a[:, :, lo:hi].reshape(B, T // dil, dil, hg, hd).transpose(0, 2, 1, 3, 4)
            return a.reshape(B * dil, T // dil, hg, hd)

        o, lse = band_attention(strided(q), strided(k), strided(v), slopes[lo:hi], dil, window // (2 * dil))
        outs.append(o.reshape(B, dil, T // dil, hg, hd).transpose(0, 2, 1, 3, 4).reshape(B, T, hg, hd))
        lses.append(lse.reshape(B, dil, T // dil, hg).transpose(0, 2, 1, 3).reshape(B, T, hg))
    o = jnp.stack(outs)
    w = jax.nn.softmax(jnp.stack(lses), axis=0)
    return jnp.einsum('gbth,gbthd->bthd', w.astype(o.dtype), o).reshape(B, T, hg * hd)


def retention_dir(q, k, v, log_gamma, strict):
    B, T, H, dk = q.shape
    dv = v.shape[-1]
    C = RET_CHUNK
    n = T // C
    qc = q.reshape(B, n, C, H, dk)
    kc = k.reshape(B, n, C, H, dk)
    vc = v.reshape(B, n, C, H, dv)
    idx = jnp.arange(C, dtype=jnp.float32)
    diff = idx[:, None] - idx[None, :]
    mask = (diff > 0) if strict else (diff >= 0)
    decay = jnp.where(mask[None], jnp.exp(log_gamma[:, None, None] * jnp.maximum(diff, 0.0)[None]), 0.0)
    inner = jnp.einsum('bnihd,bnjhd->bnhij', qc, kc) * decay[None, None]
    o_inner = jnp.einsum('bnhij,bnjhe->bnihe', inner, vc)
    zeta = jnp.exp(log_gamma[:, None] * (C - 1 - idx)[None, :])
    xi = jnp.exp(log_gamma[:, None] * (idx + 1.0)[None, :])
    chunk_decay = jnp.exp(log_gamma * C)
    kv_chunk = jnp.einsum('bnjhd,hj,bnjhe->nbhde', kc, zeta, vc)

    def step(R, kv):
        return chunk_decay[None, :, None, None] * R + kv, R

    _, r_prev = lax.scan(step, jnp.zeros((B, H, dk, dv), jnp.float32), kv_chunk.astype(jnp.float32))
    o_cross = jnp.einsum('bnihd,nbhde,hi->bnihe', qc, r_prev, xi)
    return (o_inner + o_cross).reshape(B, T, H, dv)


def retention_branch(q, k, v, g, ret_decay, pos):
    B, T, _ = q.shape
    q = rope(q.reshape(B, T, RET_HEADS, RET_QK), pos)
    k = rope(k.reshape(B, T, RET_HEADS, RET_QK), pos) * (RET_QK ** -0.5)
    v = v.reshape(B, T, RET_HEADS, RET_V)
    log_gamma = jax.nn.log_sigmoid(ret_decay.astype(jnp.float32))
    o_fwd = retention_dir(q, k, v, log_gamma[0], False)
    o_bwd = jnp.flip(retention_dir(jnp.flip(q, 1), jnp.flip(k, 1), jnp.flip(v, 1), log_gamma[1], True), 1)
    of = (o_fwd + o_bwd).astype(jnp.float32)
    mu = jnp.mean(of, axis=-1, keepdims=True)
    var = jnp.mean(jnp.square(of - mu), axis=-1, keepdims=True)
    of = ((of - mu) * lax.rsqrt(var + EPS)).reshape(B, T, RET_HEADS * RET_V)
    return (jax.nn.silu(g.astype(jnp.float32)) * of).astype(g.dtype)


def mixer(x, pos, norm_g, w_in, conv_w, conv_b, lru_gate_w, lru_gate_b, lru_lambda, mla_q_norm, mla_kv_norm,
          w_uq, w_ukv, ret_decay, w_branch, w_out):
    B, T, D = x.shape
    w_pad = jnp.pad(w_in.astype(jnp.bfloat16), ((0, 0), (0, 12288 - IN_COLS)))
    z = norm_matmul(x.reshape(B * T, D), norm_g, w_pad, jnp.float32)[:, :IN_COLS].reshape(B, T, IN_COLS)
    (xa, ga, c_q, c_kv, k_r, dq, dk, dv, rq, rk, rv, rg, zg) = split_cols(z)
    ya = rglru_branch(xa, ga, conv_w, conv_b, lru_gate_w, lru_gate_b, lru_lambda)
    yb = mla_branch(c_q, c_kv, k_r, mla_q_norm, mla_kv_norm, w_uq, w_ukv, pos)
    shp = (B, T, DIL_HEADS, DIL_HEAD_DIM)
    yc = dilated_branch(dq.reshape(shp), dk.reshape(shp), dv.reshape(shp))
    yd = retention_branch(rq, rk, rv, rg, ret_decay, pos)
    gates = jax.nn.sigmoid(zg.reshape(B, T, N_BRANCHES, D_MODEL))
    merged = (gates[:, :, 0] * jnp.einsum('btw,wd->btd', ya, w_branch[0])
              + gates[:, :, 1] * jnp.einsum('btw,wd->btd', yb, w_branch[1])
              + gates[:, :, 2] * jnp.einsum('btw,wd->btd', yc, w_branch[2])
              + gates[:, :, 3] * jnp.einsum('btw,wd->btd', yd, w_branch[3]))
    return jnp.einsum('btd,de->bte', merged, w_out)


def ec_moe(h, w_router, w_gate, w_up, w_down):
    B, T, D = h.shape
    n_tok = B * T
    xt = h.reshape(n_tok, D)
    aff = jax.nn.softmax(jnp.einsum('td,de->te', xt, w_router).astype(jnp.float32), axis=-1)
    cap = EC_FACTOR * n_tok // N_EXPERTS
    g, idx = lax.top_k(aff.T, cap)
    xe = xt[idx]
    he = jax.nn.silu(jnp.einsum('ecd,edf->ecf', xe, w_gate)) * jnp.einsum('ecd,edf->ecf', xe, w_up)
    ye = jnp.einsum('ecf,efd->ecd', he, w_down) * g[..., None].astype(he.dtype)
    out = jnp.zeros_like(xt).at[idx.reshape(-1)].add(ye.reshape(-1, D))
    return out.reshape(B, T, D)


def trunk(x, norm_mix, w_in, conv_w, conv_b, lru_gate_w, lru_gate_b, lru_lambda, mla_q_norm, mla_kv_norm,
          w_uq, w_ukv, ret_decay, w_branch, w_out, norm_ffn, w_router, w_gate, w_up, w_down, norm_final):
    B, T, D = x.shape
    pos = jnp.arange(T)
    for l in range(DEPTH):
        x = x + mixer(x, pos, norm_mix[l], w_in[l], conv_w[l], conv_b[l], lru_gate_w[l], lru_gate_b[l],
                      lru_lambda[l], mla_q_norm[l], mla_kv_norm[l], w_uq[l], w_ukv[l], ret_decay[l],
                      w_branch[l], w_out[l])
        h = rms_norm(x, norm_ffn[l])
        x = x + ec_moe(h, w_router[l], w_gate[l], w_up[l], w_down[l])
    return rmsnorm_pallas(x.reshape(B * T, D), norm_final).reshape(B, T, D)


def kernel(x_prompt, x_sample, norm_mix, w_in, conv_w, conv_b, lru_gate_w, lru_gate_b, lru_lambda,
           mla_q_norm, mla_kv_norm, w_uq, w_ukv, ret_decay, w_branch, w_out, norm_ffn, w_router,
           w_gate, w_up, w_down, norm_final):
    args = (norm_mix, w_in, conv_w, conv_b, lru_gate_w, lru_gate_b, lru_lambda, mla_q_norm, mla_kv_norm,
            w_uq, w_ukv, ret_decay, w_branch, w_out, norm_ffn, w_router, w_gate, w_up, w_down, norm_final)
    return trunk(x_prompt, *args), trunk(x_sample, *args)
```

```python
import functools
import math

import numpy as np
import jax
import jax.numpy as jnp
from jax import lax
from jax.experimental import pallas as pl
from jax.experimental.pallas import tpu as pltpu

D_MODEL = 1024
DEPTH = 4
EPS = 1e-6
NEG_INF = -1e30
ROPE_BASE = 10000.0
N_BRANCHES = 4
LRU_WIDTH = 512
LRU_BLOCKS = 8
LRU_BLOCK = LRU_WIDTH // LRU_BLOCKS
CONV_WIDTH = 4
CONV_LEFT = 2
LRU_C = 8.0
MLA_HEADS = 8
MLA_NOPE = 64
MLA_ROPE = 32
MLA_V = 64
MLA_Q_RANK = 256
MLA_KV_RANK = 128
DIL_PAIRS = ((128, 1), (512, 4), (2048, 16))
DIL_HEADS_PER_GROUP = 8
DIL_HEADS = DIL_HEADS_PER_GROUP * len(DIL_PAIRS)
DIL_HEAD_DIM = 64
RET_HEADS = 4
RET_QK = 128
RET_V = 128
N_EXPERTS = 16
EC_FACTOR = 2
D_EXPERT = 1024

LANES = 128
VMEM_LIMIT_BYTES = 56 * 1024 * 1024

Z_COLS = 12288
Z_PAD_AT = 2 * LRU_WIDTH + MLA_Q_RANK + MLA_KV_RANK + MLA_ROPE
Z_PAD = 96
CB_XA, CB_GA = 0, 4
CB_MLA = 8
CB_DQ, CB_DK, CB_DV = 12, 24, 36
CB_RQ, CB_RK, CB_RV, CB_RG = 48, 52, 56, 60
CB_ZG = 64


def _cparams(*sem):
    return pltpu.CompilerParams(dimension_semantics=sem, vmem_limit_bytes=VMEM_LIMIT_BYTES)


def _norm_matmul_kernel(x_ref, g_ref, w_ref, o_ref, h_ref):
    @pl.when(pl.program_id(1) == 0)
    def _():
        x = x_ref[...]
        y = x * lax.rsqrt(jnp.mean(x * x, axis=-1, keepdims=True) + EPS)
        h_ref[...] = (y * g_ref[...]).astype(h_ref.dtype)

    o_ref[...] = jnp.dot(h_ref[...], w_ref[...], preferred_element_type=jnp.float32).astype(o_ref.dtype)


def norm_matmul(x, g, w, out_dtype, tm, tn):
    n, d = x.shape
    c = w.shape[1]
    return pl.pallas_call(
        _norm_matmul_kernel,
        grid=(n // tm, c // tn),
        in_specs=[
            pl.BlockSpec((tm, d), lambda i, j: (i, 0)),
            pl.BlockSpec((1, d), lambda i, j: (0, 0)),
            pl.BlockSpec((d, tn), lambda i, j: (0, j)),
        ],
        out_specs=pl.BlockSpec((tm, tn), lambda i, j: (i, j)),
        out_shape=jax.ShapeDtypeStruct((n, c), out_dtype),
        scratch_shapes=[pltpu.VMEM((tm, d), jnp.bfloat16)],
        compiler_params=_cparams("arbitrary", "arbitrary"),
        name="norm_matmul",
    )(x, g.reshape(1, d), w)


def _rmsnorm_kernel(x_ref, g_ref, o_ref):
    x = x_ref[...]
    y = x * lax.rsqrt(jnp.mean(x * x, axis=-1, keepdims=True) + EPS)
    o_ref[...] = (y * g_ref[...]).astype(o_ref.dtype)


def rmsnorm_pallas(x, g, out_dtype, tm):
    n, d = x.shape
    return pl.pallas_call(
        _rmsnorm_kernel,
        grid=(n // tm,),
        in_specs=[pl.BlockSpec((tm, d), lambda i: (i, 0)), pl.BlockSpec((1, d), lambda i: (0, 0))],
        out_specs=pl.BlockSpec((tm, d), lambda i: (i, 0)),
        out_shape=jax.ShapeDtypeStruct((n, d), out_dtype),
        compiler_params=_cparams("arbitrary"),
        name="rmsnorm",
    )(x, g.reshape(1, d))


LRU_TC = 256
LRU_HALO = 16


def _softplus(x):
    return jnp.maximum(x, 0.0) + jnp.log(1.0 + jnp.exp(-jnp.abs(x)))


def _gelu_tanh(x):
    return 0.5 * x * (1.0 + jnp.tanh(math.sqrt(2.0 / math.pi) * (x + 0.044715 * (x * x * x))))


def _lru_scan_chunk(a, b, reverse):
    n = a.shape[0]
    row = lax.broadcasted_iota(jnp.int32, a.shape, 0)
    s = 1
    while s < n:
        if reverse:
            keep = row < (n - s)
            a_s = pltpu.roll(a, n - s, 0)
            b_s = pltpu.roll(b, n - s, 0)
        else:
            keep = row >= s
            a_s = pltpu.roll(a, s, 0)
            b_s = pltpu.roll(b, s, 0)
        b = jnp.where(keep, a * b_s + b, b)
        a = jnp.where(keep, a * a_s, a)
        s *= 2
    return a, b


def _lru_kernel(*refs, reverse, n_t):
    if reverse:
        (xp_ref, xc_ref, xn_ref, cw_ref, cb_ref, wg_ref, gb_ref, lam_ref, hf_ref, ga_ref,
         o_ref, carry_ref) = refs
    else:
        (xp_ref, xc_ref, xn_ref, cw_ref, cb_ref, wg_ref, gb_ref, lam_ref, o_ref, carry_ref) = refs
    step = pl.program_id(1)
    t = (n_t - 1 - step) if reverse else step

    @pl.when(step == 0)
    def _():
        carry_ref[...] = jnp.zeros_like(carry_ref)

    prev = jnp.where(t > 0, xp_ref[...].astype(jnp.float32), 0.0)
    nxt = jnp.where(t < n_t - 1, xn_ref[...].astype(jnp.float32), 0.0)
    win = jnp.concatenate([prev, xc_ref[...].astype(jnp.float32), nxt], axis=0)
    xc = cb_ref[...]
    for k in range(CONV_WIDTH):
        lo = LRU_HALO - CONV_LEFT + k
        xc = xc + cw_ref[k:k + 1, :] * win[lo:lo + LRU_TC, :]
    gl = jnp.dot(xc.astype(jnp.bfloat16), wg_ref[0], preferred_element_type=jnp.float32) + gb_ref[0]
    r = jax.nn.sigmoid(gl[:, :LRU_WIDTH])
    i = jax.nn.sigmoid(gl[:, LRU_WIDTH:])
    log_a = (-LRU_C) * r * _softplus(-lam_ref[0])
    a = jnp.exp(log_a)
    b = jnp.sqrt(1.0 - jnp.exp(2.0 * log_a)) * i * xc
    a_cum, b_cum = _lru_scan_chunk(a, b, reverse)
    h = b_cum + a_cum * carry_ref[0:1, :]
    last = 0 if reverse else LRU_TC - 1
    carry_ref[0:1, :] = h[last:last + 1, :]
    if reverse:
        o_ref[...] = (_gelu_tanh(ga_ref[...].astype(jnp.float32)) * (hf_ref[...] + h)).astype(o_ref.dtype)
    else:
        o_ref[...] = h


def lru_direction(z, conv_w, conv_b, wg, gb, lam, batch, seq, reverse, h_fwd=None):
    n_t = seq // LRU_TC
    per_halo = LRU_TC // LRU_HALO
    n_halo = batch * seq // LRU_HALO
    d = 1 if reverse else 0

    def tt(s):
        return (n_t - 1 - s) if reverse else s

    def cur(b, s):
        return (b * n_t + tt(s), 0)

    def prev(b, s):
        return (jnp.maximum((b * n_t + tt(s)) * per_halo - 1, 0), 0)

    def nxt(b, s):
        return (jnp.minimum((b * n_t + tt(s) + 1) * per_halo, n_halo - 1), 0)

    const2 = lambda b, s: (0, 0)
    in_specs = [
        pl.BlockSpec((LRU_HALO, LRU_WIDTH), prev),
        pl.BlockSpec((LRU_TC, LRU_WIDTH), cur),
        pl.BlockSpec((LRU_HALO, LRU_WIDTH), nxt),
        pl.BlockSpec((CONV_WIDTH, LRU_WIDTH), const2),
        pl.BlockSpec((1, LRU_WIDTH), const2),
        pl.BlockSpec((1, LRU_WIDTH, 2 * LRU_WIDTH), lambda b, s: (d, 0, 0)),
        pl.BlockSpec((1, 1, 2 * LRU_WIDTH), lambda b, s: (d, 0, 0)),
        pl.BlockSpec((1, 1, LRU_WIDTH), lambda b, s: (d, 0, 0)),
    ]
    args = [z, z, z, conv_w, conv_b.reshape(1, LRU_WIDTH), wg, gb, lam.reshape(2, 1, LRU_WIDTH)]
    if reverse:
        in_specs += [pl.BlockSpec((LRU_TC, LRU_WIDTH), cur),
                     pl.BlockSpec((LRU_TC, LRU_WIDTH), lambda b, s: (b * n_t + tt(s), 1))]
        args += [h_fwd, z]
        out_dtype = jnp.bfloat16
    else:
        out_dtype = jnp.float32
    return pl.pallas_call(
        functools.partial(_lru_kernel, reverse=reverse, n_t=n_t),
        grid=(batch, n_t),
        in_specs=in_specs,
        out_specs=pl.BlockSpec((LRU_TC, LRU_WIDTH), cur),
        out_shape=jax.ShapeDtypeStruct((batch * seq, LRU_WIDTH), out_dtype),
        scratch_shapes=[pltpu.VMEM((8, LRU_WIDTH), jnp.float32)],
        compiler_params=_cparams("arbitrary", "arbitrary"),
        name="lru_bwd" if reverse else "lru_fwd",
    )(*args)


def lru_gate_dense(gate_w, gate_b):
    eye = jnp.eye(LRU_BLOCKS, dtype=gate_w.dtype)
    dense = jnp.einsum('dgnij,nm->dgnimj', gate_w, eye).reshape(2, 2, LRU_WIDTH, LRU_WIDTH)
    wg = jnp.concatenate([dense[:, 0], dense[:, 1]], axis=-1).astype(jnp.bfloat16)
    gb = jnp.concatenate([gate_b[:, 0], gate_b[:, 1]], axis=-1).reshape(2, 1, 2 * LRU_WIDTH)
    return wg, gb


MLA_TM = 512
MLA_TQ = 256
MLA_TK = 512
MLA_DP = 128


def _mla_proj_kernel(z_ref, qn_ref, kn_ref, wqa_ref, wqb_ref, wk_ref, wv_ref, ea_ref, eb_ref,
                     cos_ref, sin_ref, q_ref, k_ref, v_ref):
    z = z_ref[...]
    cq = z[:, :MLA_Q_RANK].astype(jnp.float32)
    ckv = z[:, MLA_Q_RANK:MLA_Q_RANK + MLA_KV_RANK].astype(jnp.float32)
    kr = z[:, MLA_Q_RANK + MLA_KV_RANK:]
    cqn = (cq * lax.rsqrt(jnp.mean(cq * cq, axis=-1, keepdims=True) + EPS) * qn_ref[...]).astype(jnp.bfloat16)
    ckn = (ckv * lax.rsqrt(jnp.mean(ckv * ckv, axis=-1, keepdims=True) + EPS) * kn_ref[...]).astype(jnp.bfloat16)
    cos = cos_ref[...]
    sin = sin_ref[...]
    f32 = jnp.float32
    k_rope = (jnp.dot(kr, ea_ref[...], preferred_element_type=f32) * cos
              + jnp.dot(kr, eb_ref[...], preferred_element_type=f32) * sin)
    scale = (MLA_NOPE + MLA_ROPE) ** -0.5
    for h in range(MLA_HEADS):
        qa = jnp.dot(cqn, wqa_ref[h], preferred_element_type=f32)
        qb = jnp.dot(cqn, wqb_ref[h], preferred_element_type=f32)
        q_ref[0, h] = ((qa * cos + qb * sin) * scale).astype(q_ref.dtype)
        k_ref[0, h] = (jnp.dot(ckn, wk_ref[h], preferred_element_type=f32) + k_rope).astype(k_ref.dtype)
        v_ref[0, h] = jnp.dot(ckn, wv_ref[h], preferred_element_type=f32).astype(v_ref.dtype)


def _rot_half_matrix(n):
    half = n // 2
    r = np.zeros((n, n), np.float32)
    for j in range(half):
        r[half + j, j] = -1.0
        r[j, half + j] = 1.0
    return r


def mla_weights(w_uq, w_ukv):
    rot = jnp.asarray(_rot_half_matrix(MLA_ROPE))
    wq = jnp.transpose(w_uq, (1, 0, 2))
    pad = lambda a, lo, hi: jnp.pad(a, ((0, 0), (0, 0), (lo, hi)))
    wqa = pad(wq, 0, MLA_DP - MLA_NOPE - MLA_ROPE)
    wqb = pad(jnp.einsum('hrd,de->hre', wq[..., MLA_NOPE:], rot), MLA_NOPE, MLA_DP - MLA_NOPE - MLA_ROPE)
    wkv = jnp.transpose(w_ukv, (1, 0, 2))
    wk = pad(wkv[..., :MLA_NOPE], 0, MLA_DP - MLA_NOPE)
    wv_even = pad(wkv[..., MLA_NOPE:], 0, MLA_V)
    wv_odd = pad(wkv[..., MLA_NOPE:], MLA_V, 0)
    wv = jnp.where((jnp.arange(MLA_HEADS) % 2 == 0)[:, None, None], wv_even, wv_odd)
    ea = np.zeros((LANES, MLA_DP), np.float32)
    for j in range(MLA_ROPE):
        ea[j, MLA_NOPE + j] = 1.0
    eb = np.zeros((LANES, MLA_DP), np.float32)
    eb[:MLA_ROPE, MLA_NOPE:MLA_NOPE + MLA_ROPE] = _rot_half_matrix(MLA_ROPE)
    bf = jnp.bfloat16
    return (wqa.astype(bf), wqb.astype(bf), wk.astype(bf), wv.astype(bf),
            jnp.asarray(ea, bf), jnp.asarray(eb, bf))


def mla_rope_tables(seq):
    half = MLA_ROPE // 2
    inv = ROPE_BASE ** (-jnp.arange(half, dtype=jnp.float32) / half)
    ang = jnp.arange(seq, dtype=jnp.float32)[:, None] * inv[None, :]
    ones = jnp.ones((seq, MLA_NOPE), jnp.float32)
    zeros = jnp.zeros((seq, MLA_DP - MLA_NOPE - MLA_ROPE), jnp.float32)
    cos = jnp.concatenate([ones, jnp.cos(ang), jnp.cos(ang), zeros], axis=1)
    sin = jnp.concatenate([0.0 * ones, jnp.sin(ang), jnp.sin(ang), zeros], axis=1)
    return cos, sin


def mla_project(z, q_norm, kv_norm, weights, tables, batch, seq):
    wqa, wqb, wk, wv, ea, eb = weights
    cos, sin = tables
    n_t = seq // MLA_TM
    hshape = (batch, MLA_HEADS, seq, MLA_DP)
    c3 = lambda b, t: (0, 0, 0)
    c2 = lambda b, t: (0, 0)
    hspec = pl.BlockSpec((1, MLA_HEADS, MLA_TM, MLA_DP), lambda b, t: (b, 0, t, 0))
    return pl.pallas_call(
        _mla_proj_kernel,
        grid=(batch, n_t),
        in_specs=[
            pl.BlockSpec((MLA_TM, 4 * LANES), lambda b, t: (b * n_t + t, CB_MLA // 4)),
            pl.BlockSpec((1, MLA_Q_RANK), c2),
            pl.BlockSpec((1, MLA_KV_RANK), c2),
            pl.BlockSpec((MLA_HEADS, MLA_Q_RANK, MLA_DP), c3),
            pl.BlockSpec((MLA_HEADS, MLA_Q_RANK, MLA_DP), c3),
            pl.BlockSpec((MLA_HEADS, MLA_KV_RANK, MLA_DP), c3),
            pl.BlockSpec((MLA_HEADS, MLA_KV_RANK, MLA_DP), c3),
            pl.BlockSpec((LANES, MLA_DP), c2),
            pl.BlockSpec((LANES, MLA_DP), c2),
            pl.BlockSpec((MLA_TM, MLA_DP), lambda b, t: (t, 0)),
            pl.BlockSpec((MLA_TM, MLA_DP), lambda b, t: (t, 0)),
        ],
        out_specs=[hspec, hspec, hspec],
        out_shape=[jax.ShapeDtypeStruct(hshape, jnp.bfloat16)] * 3,
        compiler_params=_cparams("arbitrary", "arbitrary"),
        name="mla_project",
    )(z, q_norm.reshape(1, -1), kv_norm.reshape(1, -1), wqa, wqb, wk, wv, ea, eb, cos, sin)


def _mla_attn_kernel(q_ref, k_ref, v_ref, o_ref, m_ref, l_ref, acc_ref, *, n_kv):
    out = None
    for hh in range(2):
        q = q_ref[0, hh]
        m_ref[...] = jnp.full_like(m_ref, NEG_INF)
        l_ref[...] = jnp.zeros_like(l_ref)
        acc_ref[...] = jnp.zeros_like(acc_ref)

        def body(j, carry):
            k = k_ref[0, hh, pl.ds(pl.multiple_of(j * MLA_TK, MLA_TK), MLA_TK), :]
            v = v_ref[0, hh, pl.ds(pl.multiple_of(j * MLA_TK, MLA_TK), MLA_TK), :]
            s = lax.dot_general(q, k, (((1,), (1,)), ((), ())), preferred_element_type=jnp.float32)
            m_prev = m_ref[...]
            m_next = jnp.maximum(m_prev, jnp.max(s, axis=1, keepdims=True))
            p = jnp.exp(s - m_next[:, :1])
            alpha = jnp.exp(m_prev - m_next)
            l_ref[...] = alpha * l_ref[...] + jnp.sum(p, axis=1, keepdims=True)
            acc_ref[...] = alpha * acc_ref[...] + jnp.dot(p.astype(jnp.bfloat16), v,
                                                           preferred_element_type=jnp.float32)
            m_ref[...] = m_next
            return carry

        lax.fori_loop(0, n_kv, body, 0)
        part = acc_ref[...] / l_ref[...]
        out = part if out is None else out + part
    o_ref[...] = out.astype(o_ref.dtype)


def mla_attention(q, k, v, batch, seq):
    n_q = seq // MLA_TQ
    qspec = pl.BlockSpec((1, 2, MLA_TQ, MLA_DP), lambda b, hp, i: (b, hp, i, 0))
    kvspec = pl.BlockSpec((1, 2, seq, MLA_DP), lambda b, hp, i: (b, hp, 0, 0))
    return pl.pallas_call(
        functools.partial(_mla_attn_kernel, n_kv=seq // MLA_TK),
        grid=(batch, MLA_HEADS // 2, n_q),
        in_specs=[qspec, kvspec, kvspec],
        out_specs=pl.BlockSpec((MLA_TQ, LANES), lambda b, hp, i: (b * n_q + i, hp)),
        out_shape=jax.ShapeDtypeStruct((batch * seq, MLA_HEADS * MLA_V), jnp.bfloat16),
        scratch_shapes=[pltpu.VMEM((MLA_TQ, LANES), jnp.float32)] * 3,
        compiler_params=_cparams("arbitrary", "arbitrary", "arbitrary"),
        name="mla_attention",
    )(q, k, v)


BAND_Q = 128
BAND_W = 256
BAND_RADIUS = 64


def band_bias_table(group, dil):
    n = DIL_HEADS
    slopes = np.asarray([2.0 ** (-8.0 * (h + 1) / n) for h in range(n)], np.float32)
    slopes = slopes[group * DIL_HEADS_PER_GROUP:(group + 1) * DIL_HEADS_PER_GROUP]
    iq = np.arange(BAND_Q)[:, None]
    ik = np.arange(BAND_W)[None, :]
    tabs = []
    for d in range(3):
        dist = np.abs(d * BAND_RADIUS + iq - ik)
        bias = -slopes[:, None, None] * (dil * dist).astype(np.float32)[None]
        tabs.append(np.where((dist <= BAND_RADIUS)[None], bias, np.float32(NEG_INF)))
    return jnp.asarray(np.stack(tabs).astype(np.float32))


def _band_kernel(q_ref, k_ref, v_ref, bias_ref, o_ref, lse_ref, *, length):
    lane = lax.broadcasted_iota(jnp.int32, (1, LANES), 1)
    first = lane < DIL_HEAD_DIM
    scale = DIL_HEAD_DIM ** -0.5

    def body(qb, carry):
        q0 = pl.multiple_of(qb * BAND_Q, BAND_Q)
        start = pl.multiple_of(jnp.clip(q0 - BAND_RADIUS, 0, length - BAND_W), BAND_RADIUS)
        didx = (q0 - start) // BAND_RADIUS
        q = q_ref[0, pl.ds(q0, BAND_Q), :]
        kw = k_ref[0, pl.ds(start, BAND_W), :]
        vw = v_ref[0, pl.ds(start, BAND_W), :]
        outs, lses = [], []
        for hh in range(2):
            sel = first if hh == 0 else jnp.logical_not(first)
            qh = jnp.where(sel, q, jnp.zeros_like(q))
            s = lax.dot_general(qh, kw, (((1,), (1,)), ((), ())), preferred_element_type=jnp.float32)
            s = s * scale + bias_ref[didx, hh]
            m = jnp.max(s, axis=1, keepdims=True)
            e = jnp.exp(s - m)
            den = jnp.sum(e, axis=1, keepdims=True)
            p = (e / den).astype(jnp.bfloat16)
            outs.append(jnp.dot(p, vw, preferred_element_type=jnp.float32))
            lses.append(m + jnp.log(den))
        o_ref[0, pl.ds(q0, BAND_Q), :] = jnp.where(first, outs[0], outs[1]).astype(o_ref.dtype)
        lse_ref[0, pl.ds(q0, BAND_Q), :] = jnp.where(first, lses[0], lses[1])
        return carry

    lax.fori_loop(0, length // BAND_Q, body, 0)


def band_group(z, group, dil, batch, seq):
    length = seq // dil
    zv = z.reshape(batch, length, dil * Z_COLS)
    zblocks = Z_COLS // LANES
    pairs = DIL_HEADS_PER_GROUP // 2

    def zspec(cb):
        return pl.BlockSpec((1, length, LANES), lambda b, r, hp: (b, 0, r * zblocks + cb + group * pairs + hp))

    ospec = pl.BlockSpec((1, length, LANES), lambda b, r, hp: (b, 0, r * pairs + hp))
    width = DIL_HEADS_PER_GROUP * DIL_HEAD_DIM
    o, lse = pl.pallas_call(
        functools.partial(_band_kernel, length=length),
        grid=(batch, dil, pairs),
        in_specs=[zspec(CB_DQ), zspec(CB_DK), zspec(CB_DV),
                  pl.BlockSpec((3, 2, BAND_Q, BAND_W), lambda b, r, hp: (0, hp, 0, 0))],
        out_specs=[ospec, ospec],
        out_shape=[jax.ShapeDtypeStruct((batch, length, dil * width), jnp.bfloat16),
                   jax.ShapeDtypeStruct((batch, length, dil * width), jnp.float32)],
        compiler_params=_cparams("arbitrary", "arbitrary", "arbitrary"),
        name=f"band_attention_g{group}",
    )(zv, zv, zv, band_bias_table(group, dil))
    return o.reshape(batch * seq, width), lse.reshape(batch * seq, width)


RET_C = 256


def _ret_kernel(lg_ref, q_ref, k_ref, v_ref, g_ref, cos_ref, sin_ref, o_ref, qs_ref, ks_ref, o1_ref, *, n_c):
    h = pl.program_id(1)
    lgf = lg_ref[0, h]
    lgb = lg_ref[1, h]
    c = RET_C
    f32 = jnp.float32
    bf = jnp.bfloat16
    ii = lax.broadcasted_iota(jnp.int32, (c, c), 0)
    jj = lax.broadcasted_iota(jnp.int32, (c, c), 1)
    diff = (ii - jj).astype(f32)
    decay = jnp.where(diff >= 0.0, jnp.exp(lgf * jnp.maximum(diff, 0.0)), jnp.exp(lgb * jnp.maximum(-diff, 0.0)))
    idx = lax.broadcasted_iota(jnp.int32, (c, 1), 0).astype(f32)
    xi_f = jnp.exp(lgf * (idx + 1.0))
    zeta_f = jnp.exp(lgf * (c - 1.0 - idx))
    xi_b = jnp.exp(lgb * (c - idx))
    zeta_b = jnp.exp(lgb * idx)
    cd_f = jnp.exp(lgf * c)
    cd_b = jnp.exp(lgb * c)
    kscale = RET_QK ** -0.5

    def rope(x, rows):
        return x * cos_ref[rows, :] + pltpu.roll(x, RET_QK // 2, 1) * sin_ref[rows, :]

    def fwd(n, state):
        rows = pl.ds(pl.multiple_of(n * c, c), c)
        q = rope(q_ref[0, rows, :].astype(f32), rows)
        k = rope(k_ref[0, rows, :].astype(f32), rows) * kscale
        v = v_ref[0, rows, :]
        qs_ref[rows, :] = q
        ks_ref[rows, :] = k
        qb = q.astype(bf)
        s = lax.dot_general(qb, k.astype(bf), (((1,), (1,)), ((), ())), preferred_element_type=f32) * decay
        o = jnp.dot(s.astype(bf), v, preferred_element_type=f32)
        o = o + xi_f * jnp.dot(qb, state.astype(bf), preferred_element_type=f32)
        o1_ref[rows, :] = o
        kz = (k * zeta_f).T.astype(bf)
        return cd_f * state + jnp.dot(kz, v, preferred_element_type=f32)

    lax.fori_loop(0, n_c, fwd, jnp.zeros((RET_QK, RET_V), f32))

    def bwd(step, state):
        n = n_c - 1 - step
        rows = pl.ds(pl.multiple_of(n * c, c), c)
        q = qs_ref[rows, :]
        k = ks_ref[rows, :]
        v = v_ref[0, rows, :]
        of = o1_ref[rows, :] + xi_b * jnp.dot(q.astype(bf), state.astype(bf), preferred_element_type=f32)
        mu = jnp.mean(of, axis=-1, keepdims=True)
        var = jnp.mean(jnp.square(of - mu), axis=-1, keepdims=True)
        of = (of - mu) * lax.rsqrt(var + EPS)
        g = g_ref[0, rows, :].astype(f32)
        o_ref[0, rows, :] = (g * jax.nn.sigmoid(g) * of).astype(o_ref.dtype)
        kz = (k * zeta_b).T.astype(bf)
        return cd_b * state + jnp.dot(kz, v, preferred_element_type=f32)

    lax.fori_loop(0, n_c, bwd, jnp.zeros((RET_QK, RET_V), f32))


def ret_rope_tables(seq):
    half = RET_QK // 2
    inv = ROPE_BASE ** (-jnp.arange(half, dtype=jnp.float32) / half)
    ang = jnp.arange(seq, dtype=jnp.float32)[:, None] * inv[None, :]
    cos = jnp.concatenate([jnp.cos(ang), jnp.cos(ang)], axis=1)
    sin = jnp.concatenate([-jnp.sin(ang), jnp.sin(ang)], axis=1)
    return cos, sin


def retention(z, ret_decay, tables, batch, seq):
    zv = z.reshape(batch, seq, Z_COLS)
    log_gamma = jax.nn.log_sigmoid(ret_decay.astype(jnp.float32))
    cos, sin = tables

    def zspec(cb):
        return pl.BlockSpec((1, seq, LANES), lambda b, h: (b, 0, cb + h))

    tspec = pl.BlockSpec((seq, RET_QK), lambda b, h: (0, 0))
    out = pl.pallas_call(
        functools.partial(_ret_kernel, n_c=seq // RET_C),
        grid=(batch, RET_HEADS),
        in_specs=[pl.BlockSpec(memory_space=pltpu.SMEM),
                  zspec(CB_RQ), zspec(CB_RK), zspec(CB_RV), zspec(CB_RG), tspec, tspec],
        out_specs=pl.BlockSpec((1, seq, LANES), lambda b, h: (b, 0, h)),
        out_shape=jax.ShapeDtypeStruct((batch, seq, RET_HEADS * RET_V), jnp.bfloat16),
        scratch_shapes=[pltpu.VMEM((seq, RET_QK), jnp.float32)] * 3,
        compiler_params=_cparams("arbitrary", "arbitrary"),
        name="retention",
    )(log_gamma, zv, zv, zv, zv, cos, sin)
    return out.reshape(batch * seq, RET_HEADS * RET_V)


MERGE_TM = 512


def _merge_kernel(x_ref, ya_ref, yb_ref, o0_ref, o1_ref, o2_ref, l0_ref, l1_ref, l2_ref, yd_ref,
                  zg_ref, wb_ref, wo_ref, out_ref):
    f32 = jnp.float32
    l0, l1, l2 = l0_ref[...], l1_ref[...], l2_ref[...]
    m = jnp.maximum(jnp.maximum(l0, l1), l2)
    e0, e1, e2 = jnp.exp(l0 - m), jnp.exp(l1 - m), jnp.exp(l2 - m)
    inv = 1.0 / (e0 + e1 + e2)
    yc = ((e0 * inv) * o0_ref[...].astype(f32) + (e1 * inv) * o1_ref[...].astype(f32)
          + (e2 * inv) * o2_ref[...].astype(f32)).astype(jnp.bfloat16)
    merged = None
    for i, y in enumerate((ya_ref[...], yb_ref[...], yc, yd_ref[...])):
        gate = jax.nn.sigmoid(zg_ref[:, i * D_MODEL:(i + 1) * D_MODEL].astype(f32))
        term = gate * jnp.dot(y, wb_ref[i], preferred_element_type=f32)
        merged = term if merged is None else merged + term
    out_ref[...] = x_ref[...] + jnp.dot(merged.astype(jnp.bfloat16), wo_ref[...], preferred_element_type=f32)


def merge_project(x, ya, yb, dil_o, dil_lse, yd, z, w_branch, w_out):
    n = x.shape[0]
    tm = MERGE_TM
    width = 4 * LANES
    row = lambda i: (i, 0)
    bspec = pl.BlockSpec((tm, width), row)
    return pl.pallas_call(
        _merge_kernel,
        grid=(n // tm,),
        in_specs=[pl.BlockSpec((tm, D_MODEL), row), bspec, bspec, bspec, bspec, bspec, bspec, bspec, bspec,
                  bspec,
                  pl.BlockSpec((tm, N_BRANCHES * D_MODEL), lambda i: (i, CB_ZG * LANES // (N_BRANCHES * D_MODEL))),
                  pl.BlockSpec((N_BRANCHES, width, D_MODEL), lambda i: (0, 0, 0)),
                  pl.BlockSpec((D_MODEL, D_MODEL), lambda i: (0, 0))],
        out_specs=pl.BlockSpec((tm, D_MODEL), row),
        out_shape=jax.ShapeDtypeStruct((n, D_MODEL), jnp.float32),
        compiler_params=_cparams("arbitrary"),
        name="merge_project",
    )(x, ya, yb, *dil_o, *dil_lse, yd, z, w_branch, w_out)


MOE_TM = 512


def _expert_kernel(x_ref, g_ref, wg_ref, wu_ref, wd_ref, o_ref):
    f32 = jnp.float32
    x = x_ref[0]
    a = jnp.dot(x, wg_ref[0], preferred_element_type=f32)
    u = jnp.dot(x, wu_ref[0], preferred_element_type=f32)
    he = (a * jax.nn.sigmoid(a) * u).astype(jnp.bfloat16)
    o_ref[0] = jnp.dot(he, wd_ref[0], preferred_element_type=f32) * g_ref[0]


def expert_ffn(xe, gate, w_gate, w_up, w_down):
    e, cap, d = xe.shape
    tm = min(MOE_TM, cap)
    wspec = pl.BlockSpec((1, d, D_EXPERT), lambda i, j: (i, 0, 0))
    return pl.pallas_call(
        _expert_kernel,
        grid=(e, cap // tm),
        in_specs=[pl.BlockSpec((1, tm, d), lambda i, j: (i, j, 0)),
                  pl.BlockSpec((1, tm, 1), lambda i, j: (i, j, 0)),
                  wspec, wspec,
                  pl.BlockSpec((1, D_EXPERT, d), lambda i, j: (i, 0, 0))],
        out_specs=pl.BlockSpec((1, tm, d), lambda i, j: (i, j, 0)),
        out_shape=jax.ShapeDtypeStruct((e, cap, d), jnp.float32),
        compiler_params=_cparams("arbitrary", "arbitrary"),
        name="expert_ffn",
    )(xe, gate[..., None], w_gate, w_up, w_down)


def ec_moe(x, norm_g, w_router, w_gate, w_up, w_down):
    n_tok, d = x.shape
    h = rmsnorm_pallas(x, norm_g, jnp.bfloat16, min(1024, n_tok))
    aff = jax.nn.softmax(jnp.dot(h, w_router.astype(jnp.bfloat16), preferred_element_type=jnp.float32), axis=-1)
    cap = EC_FACTOR * n_tok // N_EXPERTS
    g, idx = lax.top_k(aff.T, cap)
    ye = expert_ffn(h[idx], g, w_gate, w_up, w_down)
    return x.at[idx.reshape(-1)].add(ye.reshape(-1, d))


def pad_in_proj(w_in):
    w = w_in.astype(jnp.bfloat16)
    zeros = jnp.zeros((w.shape[0], Z_PAD), w.dtype)
    return jnp.concatenate([w[:, :Z_PAD_AT], zeros, w[:, Z_PAD_AT:]], axis=1)


def mixer(x, batch, seq, norm_g, w_in, conv_w, conv_b, lru_gate_w, lru_gate_b, lru_lambda, mla_q_norm,
          mla_kv_norm, w_uq, w_ukv, ret_decay, w_branch, w_out, mla_tables, ret_tables):
    n = batch * seq
    z = norm_matmul(x, norm_g, pad_in_proj(w_in), jnp.bfloat16, min(1024, n), 1024)
    wg, gb = lru_gate_dense(lru_gate_w, lru_gate_b)
    h_fwd = lru_direction(z, conv_w, conv_b, wg, gb, lru_lambda, batch, seq, False)
    ya = lru_direction(z, conv_w, conv_b, wg, gb, lru_lambda, batch, seq, True, h_fwd)
    q, k, v = mla_project(z, mla_q_norm, mla_kv_norm, mla_weights(w_uq, w_ukv), mla_tables, batch, seq)
    yb = mla_attention(q, k, v, batch, seq)
    dil = [band_group(z, g, d, batch, seq) for g, (_, d) in enumerate(DIL_PAIRS)]
    yd = retention(z, ret_decay, ret_tables, batch, seq)
    return merge_project(x, ya, yb, [o for o, _ in dil], [l for _, l in dil], yd, z,
                         w_branch.astype(jnp.bfloat16), w_out.astype(jnp.bfloat16))


def trunk(x, norm_mix, w_in, conv_w, conv_b, lru_gate_w, lru_gate_b, lru_lambda, mla_q_norm, mla_kv_norm,
          w_uq, w_ukv, ret_decay, w_branch, w_out, norm_ffn, w_router, w_gate, w_up, w_down, norm_final):
    batch, seq, d = x.shape
    x = x.reshape(batch * seq, d)
    mla_tables = mla_rope_tables(seq)
    ret_tables = ret_rope_tables(seq)
    bf = jnp.bfloat16
    for l in range(norm_mix.shape[0]):
        x = mixer(x, batch, seq, norm_mix[l], w_in[l], conv_w[l], conv_b[l], lru_gate_w[l], lru_gate_b[l],
                  lru_lambda[l], mla_q_norm[l], mla_kv_norm[l], w_uq[l], w_ukv[l], ret_decay[l],
                  w_branch[l], w_out[l], mla_tables, ret_tables)
        x = ec_moe(x, norm_ffn[l], w_router[l], w_gate[l].astype(bf), w_up[l].astype(bf), w_down[l].astype(bf))
    return rmsnorm_pallas(x, norm_final, jnp.float32, min(1024, batch * seq)).reshape(batch, seq, d)


def kernel(x_prompt, x_sample, norm_mix, w_in, conv_w, conv_b, lru_gate_w, lru_gate_b, lru_lambda,
           mla_q_norm, mla_kv_norm, w_uq, w_ukv, ret_decay, w_branch, w_out, norm_ffn, w_router,
           w_gate, w_up, w_down, norm_final):
    args = (norm_mix, w_in, conv_w, conv_b, lru_gate_w, lru_gate_b, lru_lambda, mla_q_norm, mla_kv_norm,
            w_uq, w_ukv, ret_decay, w_branch, w_out, norm_ffn, w_router, w_gate, w_up, w_down, norm_final)
    return trunk(x_prompt, *args), trunk(x_sample, *args)
```

```python
import functools
import math

import numpy as np
import jax
import jax.numpy as jnp
from jax import lax
from jax.experimental import pallas as pl
from jax.experimental.pallas import tpu as pltpu

D_MODEL = 1024
DEPTH = 4
EPS = 1e-6
NEG_INF = -1e30
ROPE_BASE = 10000.0
N_BRANCHES = 4
LRU_WIDTH = 512
LRU_BLOCKS = 8
LRU_BLOCK = LRU_WIDTH // LRU_BLOCKS
CONV_WIDTH = 4
CONV_LEFT = 2
LRU_C = 8.0
MLA_HEADS = 8
MLA_NOPE = 64
MLA_ROPE = 32
MLA_V = 64
MLA_Q_RANK = 256
MLA_KV_RANK = 128
DIL_PAIRS = ((128, 1), (512, 4), (2048, 16))
DIL_HEADS_PER_GROUP = 8
DIL_HEADS = DIL_HEADS_PER_GROUP * len(DIL_PAIRS)
DIL_HEAD_DIM = 64
RET_HEADS = 4
RET_QK = 128
RET_V = 128
N_EXPERTS = 16
EC_FACTOR = 2
D_EXPERT = 1024

LANES = 128
VMEM_LIMIT_BYTES = 56 * 1024 * 1024

Z_COLS = 9216
Z_PAD = 96
CB_ZG = 0
CB_XA, CB_GA = 32, 36
CB_MLA = 40
CB_DQ, CB_DK, CB_DV = 44, 48, 52
CB_RQ, CB_RK, CB_RV, CB_RG = 56, 60, 64, 68
DIL_GROUP_WIDTH = DIL_HEADS_PER_GROUP * DIL_HEAD_DIM


def _cparams(*sem):
    return pltpu.CompilerParams(dimension_semantics=sem, vmem_limit_bytes=VMEM_LIMIT_BYTES)


def _norm_matmul_kernel(x_ref, g_ref, w_ref, o_ref, h_ref):
    @pl.when(pl.program_id(1) == 0)
    def _():
        x = x_ref[...]
        y = x * lax.rsqrt(jnp.mean(x * x, axis=-1, keepdims=True) + EPS)
        h_ref[...] = (y * g_ref[...]).astype(h_ref.dtype)

    o_ref[...] = jnp.dot(h_ref[...], w_ref[...], preferred_element_type=jnp.float32).astype(o_ref.dtype)


def norm_matmul(x, g, w, out_dtype, tm, tn):
    n, d = x.shape
    c = w.shape[1]
    return pl.pallas_call(
        _norm_matmul_kernel,
        grid=(n // tm, c // tn),
        in_specs=[
            pl.BlockSpec((tm, d), lambda i, j: (i, 0)),
            pl.BlockSpec((1, d), lambda i, j: (0, 0)),
            pl.BlockSpec((d, tn), lambda i, j: (0, j)),
        ],
        out_specs=pl.BlockSpec((tm, tn), lambda i, j: (i, j)),
        out_shape=jax.ShapeDtypeStruct((n, c), out_dtype),
        scratch_shapes=[pltpu.VMEM((tm, d), jnp.bfloat16)],
        compiler_params=_cparams("arbitrary", "arbitrary"),
        name="norm_matmul",
    )(x, g.reshape(1, d), w)


def _norm_matmul_dil_kernel(x_ref, g_ref, w_ref, o_ref, hf_ref, hp_ref, *, dil):
    x = x_ref[...]
    y = x * lax.rsqrt(jnp.mean(x * x, axis=-1, keepdims=True) + EPS)
    y = y * g_ref[...]
    rows = x.shape[0] // dil
    for c in range(x.shape[1] // LANES):
        cols = slice(c * LANES, (c + 1) * LANES)
        hf_ref[c] = y[:, cols]
        for r in range(dil):
            hp_ref[r * rows:(r + 1) * rows, cols] = hf_ref[c, pl.ds(r, rows, stride=dil), :].astype(hp_ref.dtype)
    out = jnp.dot(hp_ref[...], w_ref[...], preferred_element_type=jnp.float32)
    for r in range(dil):
        o_ref[0, r] = out[r * rows:(r + 1) * rows, :].astype(o_ref.dtype)


def norm_matmul_dil(x, g, w, dil, batch, seq, tm):
    n, d = x.shape
    c = w.shape[1]
    n_t = seq // tm
    return pl.pallas_call(
        functools.partial(_norm_matmul_dil_kernel, dil=dil),
        grid=(n // tm,),
        in_specs=[
            pl.BlockSpec((tm, d), lambda i: (i, 0)),
            pl.BlockSpec((1, d), lambda i: (0, 0)),
            pl.BlockSpec((d, c), lambda i: (0, 0)),
        ],
        out_specs=pl.BlockSpec((1, dil, tm // dil, c), lambda i: (i // n_t, 0, i % n_t, 0)),
        out_shape=jax.ShapeDtypeStruct((batch, dil, seq // dil, c), jnp.bfloat16),
        scratch_shapes=[pltpu.VMEM((d // LANES, tm, LANES), jnp.float32), pltpu.VMEM((tm, d), jnp.bfloat16)],
        compiler_params=_cparams("arbitrary"),
        name=f"norm_matmul_dil{dil}",
    )(x, g.reshape(1, d), w)


def _rmsnorm_kernel(x_ref, g_ref, o_ref):
    x = x_ref[...]
    y = x * lax.rsqrt(jnp.mean(x * x, axis=-1, keepdims=True) + EPS)
    o_ref[...] = (y * g_ref[...]).astype(o_ref.dtype)


def rmsnorm_pallas(x, g, out_dtype, tm):
    n, d = x.shape
    return pl.pallas_call(
        _rmsnorm_kernel,
        grid=(n // tm,),
        in_specs=[pl.BlockSpec((tm, d), lambda i: (i, 0)), pl.BlockSpec((1, d), lambda i: (0, 0))],
        out_specs=pl.BlockSpec((tm, d), lambda i: (i, 0)),
        out_shape=jax.ShapeDtypeStruct((n, d), out_dtype),
        compiler_params=_cparams("arbitrary"),
        name="rmsnorm",
    )(x, g.reshape(1, d))


LRU_TC = 256
LRU_HALO = 16


def _softplus(x):
    return jnp.maximum(x, 0.0) + jnp.log(1.0 + jnp.exp(-jnp.abs(x)))


def _gelu_tanh(x):
    return 0.5 * x * (1.0 + jnp.tanh(math.sqrt(2.0 / math.pi) * (x + 0.044715 * (x * x * x))))


def _lru_scan_chunk(a, b, reverse):
    n = a.shape[0]
    row = lax.broadcasted_iota(jnp.int32, a.shape, 0)
    s = 1
    while s < n:
        if reverse:
            keep = row < (n - s)
            a_s = pltpu.roll(a, n - s, 0)
            b_s = pltpu.roll(b, n - s, 0)
        else:
            keep = row >= s
            a_s = pltpu.roll(a, s, 0)
            b_s = pltpu.roll(b, s, 0)
        b = jnp.where(keep, a * b_s + b, b)
        a = jnp.where(keep, a * a_s, a)
        s *= 2
    return a, b


def _lru_kernel(*refs, reverse, n_t):
    if reverse:
        (xp_ref, xc_ref, xn_ref, cw_ref, cb_ref, wg_ref, gb_ref, lam_ref, hf_ref, ga_ref,
         o_ref, carry_ref) = refs
    else:
        (xp_ref, xc_ref, xn_ref, cw_ref, cb_ref, wg_ref, gb_ref, lam_ref, o_ref, carry_ref) = refs
    step = pl.program_id(1)
    t = (n_t - 1 - step) if reverse else step

    @pl.when(step == 0)
    def _():
        carry_ref[...] = jnp.zeros_like(carry_ref)

    prev = jnp.where(t > 0, xp_ref[...].astype(jnp.float32), 0.0)
    nxt = jnp.where(t < n_t - 1, xn_ref[...].astype(jnp.float32), 0.0)
    win = jnp.concatenate([prev, xc_ref[...].astype(jnp.float32), nxt], axis=0)
    xc = cb_ref[...]
    for k in range(CONV_WIDTH):
        lo = LRU_HALO - CONV_LEFT + k
        xc = xc + cw_ref[k:k + 1, :] * win[lo:lo + LRU_TC, :]
    gl = jnp.dot(xc.astype(jnp.bfloat16), wg_ref[0], preferred_element_type=jnp.float32) + gb_ref[0]
    r = jax.nn.sigmoid(gl[:, :LRU_WIDTH])
    i = jax.nn.sigmoid(gl[:, LRU_WIDTH:])
    log_a = (-LRU_C) * r * _softplus(-lam_ref[0])
    a = jnp.exp(log_a)
    b = jnp.sqrt(1.0 - jnp.exp(2.0 * log_a)) * i * xc
    a_cum, b_cum = _lru_scan_chunk(a, b, reverse)
    h = b_cum + a_cum * carry_ref[0:1, :]
    last = 0 if reverse else LRU_TC - 1
    carry_ref[0:1, :] = h[last:last + 1, :]
    if reverse:
        o_ref[...] = (_gelu_tanh(ga_ref[...].astype(jnp.float32)) * (hf_ref[...] + h)).astype(o_ref.dtype)
    else:
        o_ref[...] = h


def lru_direction(z, conv_w, conv_b, wg, gb, lam, batch, seq, reverse, h_fwd=None):
    n_t = seq // LRU_TC
    per_halo = LRU_TC // LRU_HALO
    n_halo = batch * seq // LRU_HALO
    d = 1 if reverse else 0

    def tt(s):
        return (n_t - 1 - s) if reverse else s

    xa_col = CB_XA * LANES // LRU_WIDTH
    ga_col = CB_GA * LANES // LRU_WIDTH

    def cur(b, s):
        return (b * n_t + tt(s), 0)

    def prev(b, s):
        return (jnp.maximum((b * n_t + tt(s)) * per_halo - 1, 0), xa_col)

    def nxt(b, s):
        return (jnp.minimum((b * n_t + tt(s) + 1) * per_halo, n_halo - 1), xa_col)

    const2 = lambda b, s: (0, 0)
    in_specs = [
        pl.BlockSpec((LRU_HALO, LRU_WIDTH), prev),
        pl.BlockSpec((LRU_TC, LRU_WIDTH), lambda b, s: (b * n_t + tt(s), xa_col)),
        pl.BlockSpec((LRU_HALO, LRU_WIDTH), nxt),
        pl.BlockSpec((CONV_WIDTH, LRU_WIDTH), const2),
        pl.BlockSpec((1, LRU_WIDTH), const2),
        pl.BlockSpec((1, LRU_WIDTH, 2 * LRU_WIDTH), lambda b, s: (d, 0, 0)),
        pl.BlockSpec((1, 1, 2 * LRU_WIDTH), lambda b, s: (d, 0, 0)),
        pl.BlockSpec((1, 1, LRU_WIDTH), lambda b, s: (d, 0, 0)),
    ]
    args = [z, z, z, conv_w, conv_b.reshape(1, LRU_WIDTH), wg, gb, lam.reshape(2, 1, LRU_WIDTH)]
    if reverse:
        in_specs += [pl.BlockSpec((LRU_TC, LRU_WIDTH), cur),
                     pl.BlockSpec((LRU_TC, LRU_WIDTH), lambda b, s: (b * n_t + tt(s), ga_col))]
        args += [h_fwd, z]
        out_dtype = jnp.bfloat16
    else:
        out_dtype = jnp.float32
    return pl.pallas_call(
        functools.partial(_lru_kernel, reverse=reverse, n_t=n_t),
        grid=(batch, n_t),
        in_specs=in_specs,
        out_specs=pl.BlockSpec((LRU_TC, LRU_WIDTH), cur),
        out_shape=jax.ShapeDtypeStruct((batch * seq, LRU_WIDTH), out_dtype),
        scratch_shapes=[pltpu.VMEM((8, LRU_WIDTH), jnp.float32)],
        compiler_params=_cparams("arbitrary", "arbitrary"),
        name="lru_bwd" if reverse else "lru_fwd",
    )(*args)


def lru_gate_dense(gate_w, gate_b):
    eye = jnp.eye(LRU_BLOCKS, dtype=gate_w.dtype)
    dense = jnp.einsum('dgnij,nm->dgnimj', gate_w, eye).reshape(2, 2, LRU_WIDTH, LRU_WIDTH)
    wg = jnp.concatenate([dense[:, 0], dense[:, 1]], axis=-1).astype(jnp.bfloat16)
    gb = jnp.concatenate([gate_b[:, 0], gate_b[:, 1]], axis=-1).reshape(2, 1, 2 * LRU_WIDTH)
    return wg, gb


MLA_TM = 512
MLA_TQ = 512
MLA_TK = 512
MLA_DP = 128


def _mla_proj_kernel(z_ref, qn_ref, kn_ref, wqa_ref, wqb_ref, wk_ref, wv_ref, ea_ref, eb_ref,
                     cos_ref, sin_ref, q_ref, k_ref, v_ref):
    z = z_ref[...]
    cq = z[:, :MLA_Q_RANK].astype(jnp.float32)
    ckv = z[:, MLA_Q_RANK:MLA_Q_RANK + MLA_KV_RANK].astype(jnp.float32)
    kr = z[:, MLA_Q_RANK + MLA_KV_RANK:]
    cqn = (cq * lax.rsqrt(jnp.mean(cq * cq, axis=-1, keepdims=True) + EPS) * qn_ref[...]).astype(jnp.bfloat16)
    ckn = (ckv * lax.rsqrt(jnp.mean(ckv * ckv, axis=-1, keepdims=True) + EPS) * kn_ref[...]).astype(jnp.bfloat16)
    cos = cos_ref[...]
    sin = sin_ref[...]
    f32 = jnp.float32
    k_rope = (jnp.dot(kr, ea_ref[...], preferred_element_type=f32) * cos
              + jnp.dot(kr, eb_ref[...], preferred_element_type=f32) * sin)
    scale = (MLA_NOPE + MLA_ROPE) ** -0.5
    for h in range(MLA_HEADS):
        qa = jnp.dot(cqn, wqa_ref[h], preferred_element_type=f32)
        qb = jnp.dot(cqn, wqb_ref[h], preferred_element_type=f32)
        q_ref[0, h] = ((qa * cos + qb * sin) * scale).astype(q_ref.dtype)
        k_ref[0, h] = (jnp.dot(ckn, wk_ref[h], preferred_element_type=f32) + k_rope).astype(k_ref.dtype)
        vt = lax.dot_general(wv_ref[h], ckn, (((1,), (1,)), ((), ())), preferred_element_type=f32)
        for c in range(MLA_TM // MLA_TK):
            v_ref[0, h, c] = vt[:, c * MLA_TK:(c + 1) * MLA_TK].astype(v_ref.dtype)


def _rot_half_matrix(n):
    half = n // 2
    r = np.zeros((n, n), np.float32)
    for j in range(half):
        r[half + j, j] = -1.0
        r[j, half + j] = 1.0
    return r


def mla_weights(w_uq, w_ukv):
    rot = jnp.asarray(_rot_half_matrix(MLA_ROPE))
    wq = jnp.transpose(w_uq, (1, 0, 2))
    pad = lambda a, lo, hi: jnp.pad(a, ((0, 0), (0, 0), (lo, hi)))
    wqa = pad(wq, 0, MLA_DP - MLA_NOPE - MLA_ROPE)
    wqb = pad(jnp.einsum('hrd,de->hre', wq[..., MLA_NOPE:], rot), MLA_NOPE, MLA_DP - MLA_NOPE - MLA_ROPE)
    wkv = jnp.transpose(w_ukv, (1, 0, 2))
    wk = pad(wkv[..., :MLA_NOPE], 0, MLA_DP - MLA_NOPE)
    wv_even = pad(wkv[..., MLA_NOPE:], 0, MLA_V)
    wv_odd = pad(wkv[..., MLA_NOPE:], MLA_V, 0)
    wv = jnp.where((jnp.arange(MLA_HEADS) % 2 == 0)[:, None, None], wv_even, wv_odd)
    wv = jnp.transpose(wv, (0, 2, 1))
    ea = np.zeros((LANES, MLA_DP), np.float32)
    for j in range(MLA_ROPE):
        ea[j, MLA_NOPE + j] = 1.0
    eb = np.zeros((LANES, MLA_DP), np.float32)
    eb[:MLA_ROPE, MLA_NOPE:MLA_NOPE + MLA_ROPE] = _rot_half_matrix(MLA_ROPE)
    bf = jnp.bfloat16
    return (wqa.astype(bf), wqb.astype(bf), wk.astype(bf), wv.astype(bf),
            jnp.asarray(ea, bf), jnp.asarray(eb, bf))


def mla_rope_tables(seq):
    half = MLA_ROPE // 2
    inv = ROPE_BASE ** (-jnp.arange(half, dtype=jnp.float32) / half)
    ang = jnp.arange(seq, dtype=jnp.float32)[:, None] * inv[None, :]
    ones = jnp.ones((seq, MLA_NOPE), jnp.float32)
    zeros = jnp.zeros((seq, MLA_DP - MLA_NOPE - MLA_ROPE), jnp.float32)
    cos = jnp.concatenate([ones, jnp.cos(ang), jnp.cos(ang), zeros], axis=1)
    sin = jnp.concatenate([0.0 * ones, jnp.sin(ang), jnp.sin(ang), zeros], axis=1)
    return cos, sin


def mla_project(z, q_norm, kv_norm, weights, tables, batch, seq):
    wqa, wqb, wk, wv, ea, eb = weights
    cos, sin = tables
    n_t = seq // MLA_TM
    hshape = (batch, MLA_HEADS, seq, MLA_DP)
    c3 = lambda b, t: (0, 0, 0)
    c2 = lambda b, t: (0, 0)
    hspec = pl.BlockSpec((1, MLA_HEADS, MLA_TM, MLA_DP), lambda b, t: (b, 0, t, 0))
    return pl.pallas_call(
        _mla_proj_kernel,
        grid=(batch, n_t),
        in_specs=[
            pl.BlockSpec((MLA_TM, 4 * LANES), lambda b, t: (b * n_t + t, CB_MLA // 4)),
            pl.BlockSpec((1, MLA_Q_RANK), c2),
            pl.BlockSpec((1, MLA_KV_RANK), c2),
            pl.BlockSpec((MLA_HEADS, MLA_Q_RANK, MLA_DP), c3),
            pl.BlockSpec((MLA_HEADS, MLA_Q_RANK, MLA_DP), c3),
            pl.BlockSpec((MLA_HEADS, MLA_KV_RANK, MLA_DP), c3),
            pl.BlockSpec((MLA_HEADS, MLA_KV_RANK, MLA_DP), c3),
            pl.BlockSpec((LANES, MLA_DP), c2),
            pl.BlockSpec((LANES, MLA_DP), c2),
            pl.BlockSpec((MLA_TM, MLA_DP), lambda b, t: (t, 0)),
            pl.BlockSpec((MLA_TM, MLA_DP), lambda b, t: (t, 0)),
        ],
        out_specs=[hspec, hspec,
                   pl.BlockSpec((1, MLA_HEADS, MLA_TM // MLA_TK, MLA_DP, MLA_TK), lambda b, t: (b, 0, t, 0, 0))],
        out_shape=[jax.ShapeDtypeStruct(hshape, jnp.bfloat16)] * 2
        + [jax.ShapeDtypeStruct((batch, MLA_HEADS, seq // MLA_TK, MLA_DP, MLA_TK), jnp.bfloat16)],
        compiler_params=_cparams("arbitrary", "arbitrary"),
        name="mla_project",
    )(z, q_norm.reshape(1, -1), kv_norm.reshape(1, -1), wqa, wqb, wk, wv, ea, eb, cos, sin)


def _mla_attn_kernel(q_ref, k_ref, vt_ref, o_ref, acc_ref, *, n_kv):
    acc_ref[...] = jnp.zeros_like(acc_ref)
    f32 = jnp.float32
    nt = (((1,), (1,)), ((), ()))

    def body(j, carry):
        rows = pl.ds(pl.multiple_of(j * MLA_TK, MLA_TK), MLA_TK)
        new = []
        for hh in range(2):
            m_prev, l_prev = carry[2 * hh], carry[2 * hh + 1]
            s = lax.dot_general(k_ref[0, hh, rows, :], q_ref[0, hh], nt, preferred_element_type=f32)
            m_next = jnp.maximum(m_prev, jnp.max(s, axis=0, keepdims=True))
            p = jnp.exp(s - m_next)
            alpha = jnp.exp(m_prev - m_next)
            new += [m_next, alpha * l_prev + jnp.sum(p, axis=0, keepdims=True)]
            acc_ref[hh] = alpha * acc_ref[hh] + jnp.dot(vt_ref[0, hh, j], p.astype(jnp.bfloat16),
                                                         preferred_element_type=f32)
        return tuple(new)

    m0 = jnp.full((1, MLA_TQ), NEG_INF, f32)
    l0 = jnp.zeros((1, MLA_TQ), f32)
    _, l_a, _, l_b = lax.fori_loop(0, n_kv, body, (m0, l0, m0, l0), unroll=2)
    out_t = acc_ref[0] / l_a + acc_ref[1] / l_b
    o_ref[...] = out_t.T.astype(o_ref.dtype)


def mla_attention(q, k, vt, batch, seq):
    n_q = seq // MLA_TQ
    n_kv = seq // MLA_TK
    qspec = pl.BlockSpec((1, 2, MLA_TQ, MLA_DP), lambda b, hp, i: (b, hp, i, 0))
    kspec = pl.BlockSpec((1, 2, seq, MLA_DP), lambda b, hp, i: (b, hp, 0, 0))
    vspec = pl.BlockSpec((1, 2, n_kv, MLA_DP, MLA_TK), lambda b, hp, i: (b, hp, 0, 0, 0))
    return pl.pallas_call(
        functools.partial(_mla_attn_kernel, n_kv=n_kv),
        grid=(batch, MLA_HEADS // 2, n_q),
        in_specs=[qspec, kspec, vspec],
        out_specs=pl.BlockSpec((MLA_TQ, LANES), lambda b, hp, i: (b * n_q + i, hp)),
        out_shape=jax.ShapeDtypeStruct((batch * seq, MLA_HEADS * MLA_V), jnp.bfloat16),
        scratch_shapes=[pltpu.VMEM((2, MLA_DP, MLA_TQ), jnp.float32)],
        compiler_params=_cparams("arbitrary", "arbitrary", "arbitrary"),
        name="mla_attention",
    )(q, k, vt)


BAND_Q = 128
BAND_W = 256
BAND_RADIUS = 64


def band_bias_table(group, dil):
    n = DIL_HEADS
    slopes = np.asarray([2.0 ** (-8.0 * (h + 1) / n) for h in range(n)], np.float32)
    slopes = slopes[group * DIL_HEADS_PER_GROUP:(group + 1) * DIL_HEADS_PER_GROUP]
    iq = np.arange(BAND_Q)[:, None]
    ik = np.arange(BAND_W)[None, :]
    tabs = []
    for d in range(3):
        dist = np.abs(d * BAND_RADIUS + iq - ik)
        bias = -slopes[:, None, None] * (dil * dist).astype(np.float32)[None]
        tabs.append(np.where((dist <= BAND_RADIUS)[None], bias, np.float32(NEG_INF)))
    return jnp.asarray(np.stack(tabs).astype(np.float32))


def _band_kernel(q_ref, k_ref, v_ref, bias_ref, o_ref, lse_ref, *, length):
    lane = lax.broadcasted_iota(jnp.int32, (1, LANES), 1)
    first = lane < DIL_HEAD_DIM
    scale = DIL_HEAD_DIM ** -0.5

    def body(qb, carry):
        q0 = pl.multiple_of(qb * BAND_Q, BAND_Q)
        start = pl.multiple_of(jnp.clip(q0 - BAND_RADIUS, 0, length - BAND_W), BAND_RADIUS)
        didx = (q0 - start) // BAND_RADIUS
        q = q_ref[0, 0, pl.ds(q0, BAND_Q), :]
        kw = k_ref[0, 0, pl.ds(start, BAND_W), :]
        vw = v_ref[0, 0, pl.ds(start, BAND_W), :]
        outs, lses = [], []
        for hh in range(2):
            sel = first if hh == 0 else jnp.logical_not(first)
            qh = jnp.where(sel, q, jnp.zeros_like(q))
            s = lax.dot_general(qh, kw, (((1,), (1,)), ((), ())), preferred_element_type=jnp.float32)
            s = s * scale + bias_ref[didx, hh]
            m = jnp.max(s, axis=1, keepdims=True)
            e = jnp.exp(s - m)
            den = jnp.sum(e, axis=1, keepdims=True)
            p = (e / den).astype(jnp.bfloat16)
            outs.append(jnp.dot(p, vw, preferred_element_type=jnp.float32))
            lses.append(m + jnp.log(den))
        o_ref[0, 0, pl.ds(q0, BAND_Q), :] = jnp.where(first, outs[0], outs[1]).astype(o_ref.dtype)
        lse_ref[0, 0, pl.ds(q0, BAND_Q), :] = jnp.where(first, lses[0], lses[1])
        return carry

    n_blocks = length // BAND_Q
    lax.fori_loop(0, n_blocks, body, 0, unroll=min(4, n_blocks))


def band_group(src, group, dil, batch, seq, col_blocks):
    length = seq // dil
    pairs = DIL_HEADS_PER_GROUP // 2

    def zspec(cb):
        return pl.BlockSpec((1, 1, length, LANES), lambda b, r, hp: (b, r, 0, cb + hp))

    ospec = pl.BlockSpec((1, 1, length, LANES), lambda b, r, hp: (b, r, 0, hp))
    oshape = (batch, dil, length, DIL_GROUP_WIDTH)
    return pl.pallas_call(
        functools.partial(_band_kernel, length=length),
        grid=(batch, dil, pairs),
        in_specs=[zspec(col_blocks[0]), zspec(col_blocks[1]), zspec(col_blocks[2]),
                  pl.BlockSpec((3, 2, BAND_Q, BAND_W), lambda b, r, hp: (0, hp, 0, 0))],
        out_specs=[ospec, ospec],
        out_shape=[jax.ShapeDtypeStruct(oshape, jnp.bfloat16), jax.ShapeDtypeStruct(oshape, jnp.float32)],
        compiler_params=_cparams("arbitrary", "arbitrary", "arbitrary"),
        name=f"band_attention_g{group}",
    )(src, src, src, band_bias_table(group, dil))


RET_C = 256


def _ret_kernel(lg_ref, q_ref, k_ref, v_ref, g_ref, cos_ref, sin_ref, o_ref, qs_ref, ks_ref, o1_ref, *, n_c):
    h = pl.program_id(1)
    lgf = lg_ref[0, h]
    lgb = lg_ref[1, h]
    c = RET_C
    f32 = jnp.float32
    bf = jnp.bfloat16
    ii = lax.broadcasted_iota(jnp.int32, (c, c), 0)
    jj = lax.broadcasted_iota(jnp.int32, (c, c), 1)
    diff = (ii - jj).astype(f32)
    decay = jnp.where(diff >= 0.0, jnp.exp(lgf * jnp.maximum(diff, 0.0)), jnp.exp(lgb * jnp.maximum(-diff, 0.0)))
    idx = lax.broadcasted_iota(jnp.int32, (c, 1), 0).astype(f32)
    xi_f = jnp.exp(lgf * (idx + 1.0))
    zeta_f = jnp.exp(lgf * (c - 1.0 - idx))
    xi_b = jnp.exp(lgb * (c - idx))
    zeta_b = jnp.exp(lgb * idx)
    cd_f = jnp.exp(lgf * c)
    cd_b = jnp.exp(lgb * c)
    kscale = RET_QK ** -0.5

    def rope(x, rows):
        return x * cos_ref[rows, :] + pltpu.roll(x, RET_QK // 2, 1) * sin_ref[rows, :]

    def fwd(n, state):
        rows = pl.ds(pl.multiple_of(n * c, c), c)
        q = rope(q_ref[0, rows, :].astype(f32), rows)
        k = rope(k_ref[0, rows, :].astype(f32), rows) * kscale
        v = v_ref[0, rows, :]
        qs_ref[rows, :] = q
        ks_ref[rows, :] = k
        qb = q.astype(bf)
        s = lax.dot_general(qb, k.astype(bf), (((1,), (1,)), ((), ())), preferred_element_type=f32) * decay
        o = jnp.dot(s.astype(bf), v, preferred_element_type=f32)
        o = o + xi_f * jnp.dot(qb, state.astype(bf), preferred_element_type=f32)
        o1_ref[rows, :] = o
        kz = (k * zeta_f).T.astype(bf)
        return cd_f * state + jnp.dot(kz, v, preferred_element_type=f32)

    lax.fori_loop(0, n_c, fwd, jnp.zeros((RET_QK, RET_V), f32))

    def bwd(step, state):
        n = n_c - 1 - step
        rows = pl.ds(pl.multiple_of(n * c, c), c)
        q = qs_ref[rows, :]
        k = ks_ref[rows, :]
        v = v_ref[0, rows, :]
        of = o1_ref[rows, :] + xi_b * jnp.dot(q.astype(bf), state.astype(bf), preferred_element_type=f32)
        mu = jnp.mean(of, axis=-1, keepdims=True)
        var = jnp.mean(jnp.square(of - mu), axis=-1, keepdims=True)
        of = (of - mu) * lax.rsqrt(var + EPS)
        g = g_ref[0, rows, :].astype(f32)
        o_ref[0, rows, :] = (g * jax.nn.sigmoid(g) * of).astype(o_ref.dtype)
        kz = (k * zeta_b).T.astype(bf)
        return cd_b * state + jnp.dot(kz, v, preferred_element_type=f32)

    lax.fori_loop(0, n_c, bwd, jnp.zeros((RET_QK, RET_V), f32))


def ret_rope_tables(seq):
    half = RET_QK // 2
    inv = ROPE_BASE ** (-jnp.arange(half, dtype=jnp.float32) / half)
    ang = jnp.arange(seq, dtype=jnp.float32)[:, None] * inv[None, :]
    cos = jnp.concatenate([jnp.cos(ang), jnp.cos(ang)], axis=1)
    sin = jnp.concatenate([-jnp.sin(ang), jnp.sin(ang)], axis=1)
    return cos, sin


def retention(z, ret_decay, tables, batch, seq):
    zv = z.reshape(batch, seq, Z_COLS)
    log_gamma = jax.nn.log_sigmoid(ret_decay.astype(jnp.float32))
    cos, sin = tables

    def zspec(cb):
        return pl.BlockSpec((1, seq, LANES), lambda b, h: (b, 0, cb + h))

    tspec = pl.BlockSpec((seq, RET_QK), lambda b, h: (0, 0))
    out = pl.pallas_call(
        functools.partial(_ret_kernel, n_c=seq // RET_C),
        grid=(batch, RET_HEADS),
        in_specs=[pl.BlockSpec(memory_space=pltpu.SMEM),
                  zspec(CB_RQ), zspec(CB_RK), zspec(CB_RV), zspec(CB_RG), tspec, tspec],
        out_specs=pl.BlockSpec((1, seq, LANES), lambda b, h: (b, 0, h)),
        out_shape=jax.ShapeDtypeStruct((batch, seq, RET_HEADS * RET_V), jnp.bfloat16),
        scratch_shapes=[pltpu.VMEM((seq, RET_QK), jnp.float32)] * 3,
        compiler_params=_cparams("arbitrary", "arbitrary"),
        name="retention",
    )(log_gamma, zv, zv, zv, zv, cos, sin)
    return out.reshape(batch * seq, RET_HEADS * RET_V)


MERGE_TM = 512


def _merge_kernel(x_ref, ya_ref, yb_ref, o0_ref, o1_ref, o2_ref, l0_ref, l1_ref, l2_ref, yd_ref,
                  zg_ref, wb_ref, wo_ref, out_ref, tok_ref):
    f32 = jnp.float32

    def token_order(slot, src_ref):
        dil, rows, width = src_ref.shape[1:]
        tiles = width // LANES
        for r in range(dil):
            v = src_ref[0, r].astype(f32)
            for c in range(tiles):
                tok_ref[slot * tiles + c, pl.ds(r, rows, stride=dil), :] = v[:, c * LANES:(c + 1) * LANES]
        return jnp.concatenate([tok_ref[slot * tiles + c] for c in range(tiles)], axis=1)

    l0 = l0_ref[0, 0]
    l1, l2 = token_order(0, l1_ref), token_order(1, l2_ref)
    o1, o2 = token_order(2, o1_ref), token_order(3, o2_ref)
    m = jnp.maximum(jnp.maximum(l0, l1), l2)
    e0, e1, e2 = jnp.exp(l0 - m), jnp.exp(l1 - m), jnp.exp(l2 - m)
    inv = 1.0 / (e0 + e1 + e2)
    yc = ((e0 * inv) * o0_ref[0, 0].astype(f32) + (e1 * inv) * o1 + (e2 * inv) * o2).astype(jnp.bfloat16)
    merged = None
    for i, y in enumerate((ya_ref[...], yb_ref[...], yc, yd_ref[...])):
        gate = jax.nn.sigmoid(zg_ref[:, i * D_MODEL:(i + 1) * D_MODEL].astype(f32))
        term = gate * jnp.dot(y, wb_ref[i], preferred_element_type=f32)
        merged = term if merged is None else merged + term
    out_ref[...] = x_ref[...] + jnp.dot(merged.astype(jnp.bfloat16), wo_ref[...], preferred_element_type=f32)


def merge_project(x, ya, yb, dil_o, dil_lse, yd, z, w_branch, w_out, seq):
    n = x.shape[0]
    tm = MERGE_TM
    n_t = seq // tm
    width = 4 * LANES
    row = lambda i: (i, 0)
    bspec = pl.BlockSpec((tm, width), row)

    def dspec(dil):
        return pl.BlockSpec((1, dil, tm // dil, width), lambda i: (i // n_t, 0, i % n_t, 0))

    dspecs = [dspec(d) for _, d in DIL_PAIRS]
    return pl.pallas_call(
        _merge_kernel,
        grid=(n // tm,),
        in_specs=[pl.BlockSpec((tm, D_MODEL), row), bspec, bspec, *dspecs, *dspecs, bspec,
                  pl.BlockSpec((tm, N_BRANCHES * D_MODEL), lambda i: (i, CB_ZG * LANES // (N_BRANCHES * D_MODEL))),
                  pl.BlockSpec((N_BRANCHES, width, D_MODEL), lambda i: (0, 0, 0)),
                  pl.BlockSpec((D_MODEL, D_MODEL), lambda i: (0, 0))],
        out_specs=pl.BlockSpec((tm, D_MODEL), row),
        out_shape=jax.ShapeDtypeStruct((n, D_MODEL), jnp.float32),
        scratch_shapes=[pltpu.VMEM((4 * width // LANES, tm, LANES), jnp.float32)],
        compiler_params=_cparams("arbitrary"),
        name="merge_project",
    )(x, ya, yb, *dil_o, *dil_lse, yd, z, w_branch, w_out)


MOE_TM = 512


def _expert_kernel(x_ref, g_ref, wg_ref, wu_ref, wd_ref, o_ref):
    f32 = jnp.float32
    x = x_ref[0]
    a = jnp.dot(x, wg_ref[0], preferred_element_type=f32)
    u = jnp.dot(x, wu_ref[0], preferred_element_type=f32)
    he = (a * jax.nn.sigmoid(a) * u).astype(jnp.bfloat16)
    o_ref[0] = jnp.dot(he, wd_ref[0], preferred_element_type=f32) * g_ref[0]


def expert_ffn(xe, gate, w_gate, w_up, w_down):
    e, cap, d = xe.shape
    tm = min(MOE_TM, cap)
    wspec = pl.BlockSpec((1, d, D_EXPERT), lambda i, j: (i, 0, 0))
    return pl.pallas_call(
        _expert_kernel,
        grid=(e, cap // tm),
        in_specs=[pl.BlockSpec((1, tm, d), lambda i, j: (i, j, 0)),
                  pl.BlockSpec((1, tm, 1), lambda i, j: (i, j, 0)),
                  wspec, wspec,
                  pl.BlockSpec((1, D_EXPERT, d), lambda i, j: (i, 0, 0))],
        out_specs=pl.BlockSpec((1, tm, d), lambda i, j: (i, j, 0)),
        out_shape=jax.ShapeDtypeStruct((e, cap, d), jnp.float32),
        compiler_params=_cparams("arbitrary", "arbitrary"),
        name="expert_ffn",
    )(xe, gate[..., None], w_gate, w_up, w_down)


def ec_moe(x, norm_g, w_router, w_gate, w_up, w_down):
    n_tok, d = x.shape
    h = rmsnorm_pallas(x, norm_g, jnp.bfloat16, min(1024, n_tok))
    aff = jax.nn.softmax(jnp.dot(h, w_router.astype(jnp.bfloat16), preferred_element_type=jnp.float32), axis=-1)
    cap = EC_FACTOR * n_tok // N_EXPERTS
    g, idx = lax.top_k(aff.T, cap)
    ye = expert_ffn(h[idx], g, w_gate, w_up, w_down)
    return x.at[idx.reshape(-1)].add(ye.reshape(-1, d))


def split_in_proj(w_in):
    w = w_in.astype(jnp.bfloat16)
    sizes = (LRU_WIDTH, LRU_WIDTH, MLA_Q_RANK, MLA_KV_RANK, MLA_ROPE,
             DIL_HEADS * DIL_HEAD_DIM, DIL_HEADS * DIL_HEAD_DIM, DIL_HEADS * DIL_HEAD_DIM,
             RET_HEADS * RET_QK, RET_HEADS * RET_QK, RET_HEADS * RET_V, RET_HEADS * RET_V,
             N_BRANCHES * D_MODEL)
    parts, off = [], 0
    for s in sizes:
        parts.append(w[:, off:off + s])
        off += s
    xa, ga, cq, ckv, kr, dq, dk, dv, rq, rk, rv, rg, zg = parts
    gw = DIL_GROUP_WIDTH
    grp = lambda a, g: a[:, g * gw:(g + 1) * gw]
    zeros = jnp.zeros((w.shape[0], Z_PAD), w.dtype)
    main = jnp.concatenate([zg, xa, ga, cq, ckv, kr, zeros, grp(dq, 0), grp(dk, 0), grp(dv, 0),
                            rq, rk, rv, rg], axis=1)
    dil = [jnp.concatenate([grp(dq, g), grp(dk, g), grp(dv, g)], axis=1) for g in range(1, len(DIL_PAIRS))]
    return main, dil


def mixer(x, batch, seq, norm_g, w_in, conv_w, conv_b, lru_gate_w, lru_gate_b, lru_lambda, mla_q_norm,
          mla_kv_norm, w_uq, w_ukv, ret_decay, w_branch, w_out, mla_tables, ret_tables):
    n = batch * seq
    w_main, w_dil = split_in_proj(w_in)
    z = norm_matmul(x, norm_g, w_main, jnp.bfloat16, min(1024, n), 1024)
    wg, gb = lru_gate_dense(lru_gate_w, lru_gate_b)
    h_fwd = lru_direction(z, conv_w, conv_b, wg, gb, lru_lambda, batch, seq, False)
    ya = lru_direction(z, conv_w, conv_b, wg, gb, lru_lambda, batch, seq, True, h_fwd)
    q, k, v = mla_project(z, mla_q_norm, mla_kv_norm, mla_weights(w_uq, w_ukv), mla_tables, batch, seq)
    yb = mla_attention(q, k, v, batch, seq)
    dil = [band_group(z.reshape(batch, 1, seq, Z_COLS), 0, 1, batch, seq, (CB_DQ, CB_DK, CB_DV))]
    pairs = DIL_HEADS_PER_GROUP // 2
    for g in range(1, len(DIL_PAIRS)):
        d = DIL_PAIRS[g][1]
        zd = norm_matmul_dil(x, norm_g, w_dil[g - 1], d, batch, seq, min(1024, seq))
        dil.append(band_group(zd, g, d, batch, seq, (0, pairs, 2 * pairs)))
    yd = retention(z, ret_decay, ret_tables, batch, seq)
    return merge_project(x, ya, yb, [o for o, _ in dil], [l for _, l in dil], yd, z,
                         w_branch.astype(jnp.bfloat16), w_out.astype(jnp.bfloat16), seq)


def trunk(x, norm_mix, w_in, conv_w, conv_b, lru_gate_w, lru_gate_b, lru_lambda, mla_q_norm, mla_kv_norm,
          w_uq, w_ukv, ret_decay, w_branch, w_out, norm_ffn, w_router, w_gate, w_up, w_down, norm_final):
    batch, seq, d = x.shape
    x = x.reshape(batch * seq, d)
    mla_tables = mla_rope_tables(seq)
    ret_tables = ret_rope_tables(seq)
    bf = jnp.bfloat16
    for l in range(norm_mix.shape[0]):
        x = mixer(x, batch, seq, norm_mix[l], w_in[l], conv_w[l], conv_b[l], lru_gate_w[l], lru_gate_b[l],
                  lru_lambda[l], mla_q_norm[l], mla_kv_norm[l], w_uq[l], w_ukv[l], ret_decay[l],
                  w_branch[l], w_out[l], mla_tables, ret_tables)
        x = ec_moe(x, norm_ffn[l], w_router[l], w_gate[l].astype(bf), w_up[l].astype(bf), w_down[l].astype(bf))
    return rmsnorm_pallas(x, norm_final, jnp.float32, min(1024, batch * seq)).reshape(batch, seq, d)


def kernel(x_prompt, x_sample, norm_mix, w_in, conv_w, conv_b, lru_gate_w, lru_gate_b, lru_lambda,
           mla_q_norm, mla_kv_norm, w_uq, w_ukv, ret_decay, w_branch, w_out, norm_ffn, w_router,
           w_gate, w_up, w_down, norm_final):
    args = (norm_mix, w_in, conv_w, conv_b, lru_gate_w, lru_gate_b, lru_lambda, mla_q_norm, mla_kv_norm,
            w_uq, w_ukv, ret_decay, w_branch, w_out, norm_ffn, w_router, w_gate, w_up, w_down, norm_final)
    return trunk(x_prompt, *args), trunk(x_sample, *args)
```

```python
import functools
import math

import numpy as np
import jax
import jax.numpy as jnp
from jax import lax
from jax.experimental import pallas as pl
from jax.experimental.pallas import tpu as pltpu

D_MODEL = 1024
DEPTH = 4
EPS = 1e-6
NEG_INF = -1e30
ROPE_BASE = 10000.0
N_BRANCHES = 4
LRU_WIDTH = 512
LRU_BLOCKS = 8
LRU_BLOCK = LRU_WIDTH // LRU_BLOCKS
CONV_WIDTH = 4
CONV_LEFT = 2
LRU_C = 8.0
MLA_HEADS = 8
MLA_NOPE = 64
MLA_ROPE = 32
MLA_V = 64
MLA_Q_RANK = 256
MLA_KV_RANK = 128
DIL_PAIRS = ((128, 1), (512, 4), (2048, 16))
DIL_HEADS_PER_GROUP = 8
DIL_HEADS = DIL_HEADS_PER_GROUP * len(DIL_PAIRS)
DIL_HEAD_DIM = 64
RET_HEADS = 4
RET_QK = 128
RET_V = 128
N_EXPERTS = 16
EC_FACTOR = 2
D_EXPERT = 1024

LANES = 128
VMEM_LIMIT_BYTES = 56 * 1024 * 1024

Z_COLS = 9216
Z_PAD = 96
CB_ZG = 0
CB_XA, CB_GA = 32, 36
CB_MLA = 40
CB_DQ, CB_DK, CB_DV = 44, 48, 52
CB_RQ, CB_RK, CB_RV, CB_RG = 56, 60, 64, 68
DIL_GROUP_WIDTH = DIL_HEADS_PER_GROUP * DIL_HEAD_DIM


def _cparams(*sem):
    return pltpu.CompilerParams(dimension_semantics=sem, vmem_limit_bytes=VMEM_LIMIT_BYTES)


def _norm_matmul_kernel(x_ref, g_ref, w_ref, o_ref, h_ref):
    @pl.when(pl.program_id(1) == 0)
    def _():
        x = x_ref[...]
        y = x * lax.rsqrt(jnp.mean(x * x, axis=-1, keepdims=True) + EPS)
        h_ref[...] = (y * g_ref[...]).astype(h_ref.dtype)

    o_ref[...] = jnp.dot(h_ref[...], w_ref[...], preferred_element_type=jnp.float32).astype(o_ref.dtype)


def norm_matmul(x, g, w, out_dtype, tm, tn):
    n, d = x.shape
    c = w.shape[1]
    return pl.pallas_call(
        _norm_matmul_kernel,
        grid=(n // tm, c // tn),
        in_specs=[
            pl.BlockSpec((tm, d), lambda i, j: (i, 0)),
            pl.BlockSpec((1, d), lambda i, j: (0, 0)),
            pl.BlockSpec((d, tn), lambda i, j: (0, j)),
        ],
        out_specs=pl.BlockSpec((tm, tn), lambda i, j: (i, j)),
        out_shape=jax.ShapeDtypeStruct((n, c), out_dtype),
        scratch_shapes=[pltpu.VMEM((tm, d), jnp.bfloat16)],
        compiler_params=_cparams("arbitrary", "arbitrary"),
        name="norm_matmul",
    )(x, g.reshape(1, d), w)


def _norm_matmul_dil_kernel(x_ref, g_ref, w_ref, o_ref, hf_ref, hp_ref, *, dil):
    x = x_ref[...]
    y = x * lax.rsqrt(jnp.mean(x * x, axis=-1, keepdims=True) + EPS)
    y = y * g_ref[...]
    rows = x.shape[0] // dil
    for c in range(x.shape[1] // LANES):
        cols = slice(c * LANES, (c + 1) * LANES)
        hf_ref[c] = y[:, cols]
        for r in range(dil):
            hp_ref[r * rows:(r + 1) * rows, cols] = hf_ref[c, pl.ds(r, rows, stride=dil), :].astype(hp_ref.dtype)
    out = jnp.dot(hp_ref[...], w_ref[...], preferred_element_type=jnp.float32)
    for r in range(dil):
        o_ref[0, r] = out[r * rows:(r + 1) * rows, :].astype(o_ref.dtype)


def norm_matmul_dil(x, g, w, dil, batch, seq, tm):
    n, d = x.shape
    c = w.shape[1]
    n_t = seq // tm
    return pl.pallas_call(
        functools.partial(_norm_matmul_dil_kernel, dil=dil),
        grid=(n // tm,),
        in_specs=[
            pl.BlockSpec((tm, d), lambda i: (i, 0)),
            pl.BlockSpec((1, d), lambda i: (0, 0)),
            pl.BlockSpec((d, c), lambda i: (0, 0)),
        ],
        out_specs=pl.BlockSpec((1, dil, tm // dil, c), lambda i: (i // n_t, 0, i % n_t, 0)),
        out_shape=jax.ShapeDtypeStruct((batch, dil, seq // dil, c), jnp.bfloat16),
        scratch_shapes=[pltpu.VMEM((d // LANES, tm, LANES), jnp.float32), pltpu.VMEM((tm, d), jnp.bfloat16)],
        compiler_params=_cparams("arbitrary"),
        name=f"norm_matmul_dil{dil}",
    )(x, g.reshape(1, d), w)


def _rmsnorm_kernel(x_ref, g_ref, o_ref):
    x = x_ref[...]
    y = x * lax.rsqrt(jnp.mean(x * x, axis=-1, keepdims=True) + EPS)
    o_ref[...] = (y * g_ref[...]).astype(o_ref.dtype)


def rmsnorm_pallas(x, g, out_dtype, tm):
    n, d = x.shape
    return pl.pallas_call(
        _rmsnorm_kernel,
        grid=(n // tm,),
        in_specs=[pl.BlockSpec((tm, d), lambda i: (i, 0)), pl.BlockSpec((1, d), lambda i: (0, 0))],
        out_specs=pl.BlockSpec((tm, d), lambda i: (i, 0)),
        out_shape=jax.ShapeDtypeStruct((n, d), out_dtype),
        compiler_params=_cparams("arbitrary"),
        name="rmsnorm",
    )(x, g.reshape(1, d))


LRU_TC = 256
LRU_HALO = 16


def _softplus(x):
    return jnp.maximum(x, 0.0) + jnp.log(1.0 + jnp.exp(-jnp.abs(x)))


def _gelu_tanh(x):
    return 0.5 * x * (1.0 + jnp.tanh(math.sqrt(2.0 / math.pi) * (x + 0.044715 * (x * x * x))))


def _lru_scan_chunk(a, b, reverse):
    n = a.shape[0]
    row = lax.broadcasted_iota(jnp.int32, a.shape, 0)
    s = 1
    while s < n:
        if reverse:
            keep = row < (n - s)
            a_s = pltpu.roll(a, n - s, 0)
            b_s = pltpu.roll(b, n - s, 0)
        else:
            keep = row >= s
            a_s = pltpu.roll(a, s, 0)
            b_s = pltpu.roll(b, s, 0)
        b = jnp.where(keep, a * b_s + b, b)
        a = jnp.where(keep, a * a_s, a)
        s *= 2
    return a, b


def _lru_kernel(*refs, reverse, n_t):
    if reverse:
        (xp_ref, xc_ref, xn_ref, cw_ref, cb_ref, wg_ref, gb_ref, lam_ref, hf_ref, ga_ref,
         o_ref, carry_ref) = refs
    else:
        (xp_ref, xc_ref, xn_ref, cw_ref, cb_ref, wg_ref, gb_ref, lam_ref, o_ref, carry_ref) = refs
    step = pl.program_id(1)
    t = (n_t - 1 - step) if reverse else step

    @pl.when(step == 0)
    def _():
        carry_ref[...] = jnp.zeros_like(carry_ref)

    prev = jnp.where(t > 0, xp_ref[...].astype(jnp.float32), 0.0)
    nxt = jnp.where(t < n_t - 1, xn_ref[...].astype(jnp.float32), 0.0)
    win = jnp.concatenate([prev, xc_ref[...].astype(jnp.float32), nxt], axis=0)
    xc = cb_ref[...]
    for k in range(CONV_WIDTH):
        lo = LRU_HALO - CONV_LEFT + k
        xc = xc + cw_ref[k:k + 1, :] * win[lo:lo + LRU_TC, :]
    gl = jnp.dot(xc.astype(jnp.bfloat16), wg_ref[0], preferred_element_type=jnp.float32) + gb_ref[0]
    r = jax.nn.sigmoid(gl[:, :LRU_WIDTH])
    i = jax.nn.sigmoid(gl[:, LRU_WIDTH:])
    log_a = (-LRU_C) * r * _softplus(-lam_ref[0])
    a = jnp.exp(log_a)
    b = jnp.sqrt(1.0 - jnp.exp(2.0 * log_a)) * i * xc
    a_cum, b_cum = _lru_scan_chunk(a, b, reverse)
    h = b_cum + a_cum * carry_ref[0:1, :]
    last = 0 if reverse else LRU_TC - 1
    carry_ref[0:1, :] = h[last:last + 1, :]
    if reverse:
        o_ref[...] = (_gelu_tanh(ga_ref[...].astype(jnp.float32)) * (hf_ref[...] + h)).astype(o_ref.dtype)
    else:
        o_ref[...] = h


def lru_direction(z, conv_w, conv_b, wg, gb, lam, batch, seq, reverse, h_fwd=None):
    n_t = seq // LRU_TC
    per_halo = LRU_TC // LRU_HALO
    n_halo = batch * seq // LRU_HALO
    d = 1 if reverse else 0

    def tt(s):
        return (n_t - 1 - s) if reverse else s

    xa_col = CB_XA * LANES // LRU_WIDTH
    ga_col = CB_GA * LANES // LRU_WIDTH

    def cur(b, s):
        return (b * n_t + tt(s), 0)

    def prev(b, s):
        return (jnp.maximum((b * n_t + tt(s)) * per_halo - 1, 0), xa_col)

    def nxt(b, s):
        return (jnp.minimum((b * n_t + tt(s) + 1) * per_halo, n_halo - 1), xa_col)

    const2 = lambda b, s: (0, 0)
    in_specs = [
        pl.BlockSpec((LRU_HALO, LRU_WIDTH), prev),
        pl.BlockSpec((LRU_TC, LRU_WIDTH), lambda b, s: (b * n_t + tt(s), xa_col)),
        pl.BlockSpec((LRU_HALO, LRU_WIDTH), nxt),
        pl.BlockSpec((CONV_WIDTH, LRU_WIDTH), const2),
        pl.BlockSpec((1, LRU_WIDTH), const2),
        pl.BlockSpec((1, LRU_WIDTH, 2 * LRU_WIDTH), lambda b, s: (d, 0, 0)),
        pl.BlockSpec((1, 1, 2 * LRU_WIDTH), lambda b, s: (d, 0, 0)),
        pl.BlockSpec((1, 1, LRU_WIDTH), lambda b, s: (d, 0, 0)),
    ]
    args = [z, z, z, conv_w, conv_b.reshape(1, LRU_WIDTH), wg, gb, lam.reshape(2, 1, LRU_WIDTH)]
    if reverse:
        in_specs += [pl.BlockSpec((LRU_TC, LRU_WIDTH), cur),
                     pl.BlockSpec((LRU_TC, LRU_WIDTH), lambda b, s: (b * n_t + tt(s), ga_col))]
        args += [h_fwd, z]
        out_dtype = jnp.bfloat16
    else:
        out_dtype = jnp.float32
    return pl.pallas_call(
        functools.partial(_lru_kernel, reverse=reverse, n_t=n_t),
        grid=(batch, n_t),
        in_specs=in_specs,
        out_specs=pl.BlockSpec((LRU_TC, LRU_WIDTH), cur),
        out_shape=jax.ShapeDtypeStruct((batch * seq, LRU_WIDTH), out_dtype),
        scratch_shapes=[pltpu.VMEM((8, LRU_WIDTH), jnp.float32)],
        compiler_params=_cparams("arbitrary", "arbitrary"),
        name="lru_bwd" if reverse else "lru_fwd",
    )(*args)


def lru_gate_dense(gate_w, gate_b):
    eye = jnp.eye(LRU_BLOCKS, dtype=gate_w.dtype)
    dense = jnp.einsum('dgnij,nm->dgnimj', gate_w, eye).reshape(2, 2, LRU_WIDTH, LRU_WIDTH)
    wg = jnp.concatenate([dense[:, 0], dense[:, 1]], axis=-1).astype(jnp.bfloat16)
    gb = jnp.concatenate([gate_b[:, 0], gate_b[:, 1]], axis=-1).reshape(2, 1, 2 * LRU_WIDTH)
    return wg, gb


MLA_TM = 512
MLA_TQ = 512
MLA_TK = 512
MLA_DP = 128


def _mla_proj_kernel(z_ref, qn_ref, kn_ref, wqa_ref, wqb_ref, wk_ref, wv_ref, ea_ref, eb_ref,
                     cos_ref, sin_ref, q_ref, k_ref, v_ref):
    z = z_ref[...]
    cq = z[:, :MLA_Q_RANK].astype(jnp.float32)
    ckv = z[:, MLA_Q_RANK:MLA_Q_RANK + MLA_KV_RANK].astype(jnp.float32)
    kr = z[:, MLA_Q_RANK + MLA_KV_RANK:]
    cqn = (cq * lax.rsqrt(jnp.mean(cq * cq, axis=-1, keepdims=True) + EPS) * qn_ref[...]).astype(jnp.bfloat16)
    ckn = (ckv * lax.rsqrt(jnp.mean(ckv * ckv, axis=-1, keepdims=True) + EPS) * kn_ref[...]).astype(jnp.bfloat16)
    cos = cos_ref[...]
    sin = sin_ref[...]
    f32 = jnp.float32
    k_rope = (jnp.dot(kr, ea_ref[...], preferred_element_type=f32) * cos
              + jnp.dot(kr, eb_ref[...], preferred_element_type=f32) * sin)
    scale = (MLA_NOPE + MLA_ROPE) ** -0.5
    for h in range(MLA_HEADS):
        qa = jnp.dot(cqn, wqa_ref[h], preferred_element_type=f32)
        qb = jnp.dot(cqn, wqb_ref[h], preferred_element_type=f32)
        q_ref[0, h] = ((qa * cos + qb * sin) * scale).astype(q_ref.dtype)
        k_ref[0, h] = (jnp.dot(ckn, wk_ref[h], preferred_element_type=f32) + k_rope).astype(k_ref.dtype)
        vt = lax.dot_general(wv_ref[h], ckn, (((1,), (1,)), ((), ())), preferred_element_type=f32)
        for c in range(MLA_TM // MLA_TK):
            v_ref[0, h, c] = vt[:, c * MLA_TK:(c + 1) * MLA_TK].astype(v_ref.dtype)


def _rot_half_matrix(n):
    half = n // 2
    r = np.zeros((n, n), np.float32)
    for j in range(half):
        r[half + j, j] = -1.0
        r[j, half + j] = 1.0
    return r


def mla_weights(w_uq, w_ukv):
    rot = jnp.asarray(_rot_half_matrix(MLA_ROPE))
    wq = jnp.transpose(w_uq, (1, 0, 2))
    pad = lambda a, lo, hi: jnp.pad(a, ((0, 0), (0, 0), (lo, hi)))
    wqa = pad(wq, 0, MLA_DP - MLA_NOPE - MLA_ROPE)
    wqb = pad(jnp.einsum('hrd,de->hre', wq[..., MLA_NOPE:], rot), MLA_NOPE, MLA_DP - MLA_NOPE - MLA_ROPE)
    wkv = jnp.transpose(w_ukv, (1, 0, 2))
    wk = pad(wkv[..., :MLA_NOPE], 0, MLA_DP - MLA_NOPE)
    wv_even = pad(wkv[..., MLA_NOPE:], 0, MLA_V)
    wv_odd = pad(wkv[..., MLA_NOPE:], MLA_V, 0)
    wv = jnp.where((jnp.arange(MLA_HEADS) % 2 == 0)[:, None, None], wv_even, wv_odd)
    wv = jnp.transpose(wv, (0, 2, 1))
    ea = np.zeros((LANES, MLA_DP), np.float32)
    for j in range(MLA_ROPE):
        ea[j, MLA_NOPE + j] = 1.0
    eb = np.zeros((LANES, MLA_DP), np.float32)
    eb[:MLA_ROPE, MLA_NOPE:MLA_NOPE + MLA_ROPE] = _rot_half_matrix(MLA_ROPE)
    bf = jnp.bfloat16
    return (wqa.astype(bf), wqb.astype(bf), wk.astype(bf), wv.astype(bf),
            jnp.asarray(ea, bf), jnp.asarray(eb, bf))


def mla_rope_tables(seq):
    half = MLA_ROPE // 2
    inv = ROPE_BASE ** (-jnp.arange(half, dtype=jnp.float32) / half)
    ang = jnp.arange(seq, dtype=jnp.float32)[:, None] * inv[None, :]
    ones = jnp.ones((seq, MLA_NOPE), jnp.float32)
    zeros = jnp.zeros((seq, MLA_DP - MLA_NOPE - MLA_ROPE), jnp.float32)
    cos = jnp.concatenate([ones, jnp.cos(ang), jnp.cos(ang), zeros], axis=1)
    sin = jnp.concatenate([0.0 * ones, jnp.sin(ang), jnp.sin(ang), zeros], axis=1)
    return cos, sin


def mla_project(z, q_norm, kv_norm, weights, tables, batch, seq):
    wqa, wqb, wk, wv, ea, eb = weights
    cos, sin = tables
    n_t = seq // MLA_TM
    hshape = (batch, MLA_HEADS, seq, MLA_DP)
    c3 = lambda b, t: (0, 0, 0)
    c2 = lambda b, t: (0, 0)
    hspec = pl.BlockSpec((1, MLA_HEADS, MLA_TM, MLA_DP), lambda b, t: (b, 0, t, 0))
    return pl.pallas_call(
        _mla_proj_kernel,
        grid=(batch, n_t),
        in_specs=[
            pl.BlockSpec((MLA_TM, 4 * LANES), lambda b, t: (b * n_t + t, CB_MLA // 4)),
            pl.BlockSpec((1, MLA_Q_RANK), c2),
            pl.BlockSpec((1, MLA_KV_RANK), c2),
            pl.BlockSpec((MLA_HEADS, MLA_Q_RANK, MLA_DP), c3),
            pl.BlockSpec((MLA_HEADS, MLA_Q_RANK, MLA_DP), c3),
            pl.BlockSpec((MLA_HEADS, MLA_KV_RANK, MLA_DP), c3),
            pl.BlockSpec((MLA_HEADS, MLA_KV_RANK, MLA_DP), c3),
            pl.BlockSpec((LANES, MLA_DP), c2),
            pl.BlockSpec((LANES, MLA_DP), c2),
            pl.BlockSpec((MLA_TM, MLA_DP), lambda b, t: (t, 0)),
            pl.BlockSpec((MLA_TM, MLA_DP), lambda b, t: (t, 0)),
        ],
        out_specs=[hspec, hspec,
                   pl.BlockSpec((1, MLA_HEADS, MLA_TM // MLA_TK, MLA_DP, MLA_TK), lambda b, t: (b, 0, t, 0, 0))],
        out_shape=[jax.ShapeDtypeStruct(hshape, jnp.bfloat16)] * 2
        + [jax.ShapeDtypeStruct((batch, MLA_HEADS, seq // MLA_TK, MLA_DP, MLA_TK), jnp.bfloat16)],
        compiler_params=_cparams("arbitrary", "arbitrary"),
        name="mla_project",
    )(z, q_norm.reshape(1, -1), kv_norm.reshape(1, -1), wqa, wqb, wk, wv, ea, eb, cos, sin)


def _mla_attn_kernel(q_ref, k_ref, vt_ref, o_ref, acc_ref, *, n_kv):
    acc_ref[...] = jnp.zeros_like(acc_ref)
    f32 = jnp.float32
    nt = (((1,), (1,)), ((), ()))

    def body(j, carry):
        rows = pl.ds(pl.multiple_of(j * MLA_TK, MLA_TK), MLA_TK)
        new = []
        for hh in range(2):
            m_prev, l_prev = carry[2 * hh], carry[2 * hh + 1]
            s = lax.dot_general(k_ref[0, hh, rows, :], q_ref[0, hh], nt, preferred_element_type=f32)
            m_next = jnp.maximum(m_prev, jnp.max(s, axis=0, keepdims=True))
            p = jnp.exp(s - m_next)
            alpha = jnp.exp(m_prev - m_next)
            new += [m_next, alpha * l_prev + jnp.sum(p, axis=0, keepdims=True)]
            acc_ref[hh] = alpha * acc_ref[hh] + jnp.dot(vt_ref[0, hh, j], p.astype(jnp.bfloat16),
                                                         preferred_element_type=f32)
        return tuple(new)

    m0 = jnp.full((1, MLA_TQ), NEG_INF, f32)
    l0 = jnp.zeros((1, MLA_TQ), f32)
    _, l_a, _, l_b = lax.fori_loop(0, n_kv, body, (m0, l0, m0, l0), unroll=2)
    out_t = acc_ref[0] / l_a + acc_ref[1] / l_b
    o_ref[...] = out_t.T.astype(o_ref.dtype)


def mla_attention(q, k, vt, batch, seq):
    n_q = seq // MLA_TQ
    n_kv = seq // MLA_TK
    qspec = pl.BlockSpec((1, 2, MLA_TQ, MLA_DP), lambda b, hp, i: (b, hp, i, 0))
    kspec = pl.BlockSpec((1, 2, seq, MLA_DP), lambda b, hp, i: (b, hp, 0, 0))
    vspec = pl.BlockSpec((1, 2, n_kv, MLA_DP, MLA_TK), lambda b, hp, i: (b, hp, 0, 0, 0))
    return pl.pallas_call(
        functools.partial(_mla_attn_kernel, n_kv=n_kv),
        grid=(batch, MLA_HEADS // 2, n_q),
        in_specs=[qspec, kspec, vspec],
        out_specs=pl.BlockSpec((MLA_TQ, LANES), lambda b, hp, i: (b * n_q + i, hp)),
        out_shape=jax.ShapeDtypeStruct((batch * seq, MLA_HEADS * MLA_V), jnp.bfloat16),
        scratch_shapes=[pltpu.VMEM((2, MLA_DP, MLA_TQ), jnp.float32)],
        compiler_params=_cparams("arbitrary", "arbitrary", "arbitrary"),
        name="mla_attention",
    )(q, k, vt)


BAND_Q = 128
BAND_W = 256
BAND_RADIUS = 64


def band_bias_table(group, dil):
    n = DIL_HEADS
    slopes = np.asarray([2.0 ** (-8.0 * (h + 1) / n) for h in range(n)], np.float32)
    slopes = slopes[group * DIL_HEADS_PER_GROUP:(group + 1) * DIL_HEADS_PER_GROUP]
    iq = np.arange(BAND_Q)[:, None]
    ik = np.arange(BAND_W)[None, :]
    tabs = []
    for d in range(3):
        dist = np.abs(d * BAND_RADIUS + iq - ik)
        bias = -slopes[:, None, None] * (dil * dist).astype(np.float32)[None]
        tabs.append(np.where((dist <= BAND_RADIUS)[None], bias, np.float32(NEG_INF)))
    return jnp.asarray(np.stack(tabs).astype(np.float32))


def _band_kernel(q_ref, k_ref, v_ref, bias_ref, o_ref, lse_ref, *, length):
    lane = lax.broadcasted_iota(jnp.int32, (1, LANES), 1)
    first = lane < DIL_HEAD_DIM
    scale = DIL_HEAD_DIM ** -0.5

    def body(qb, carry):
        q0 = pl.multiple_of(qb * BAND_Q, BAND_Q)
        start = pl.multiple_of(jnp.clip(q0 - BAND_RADIUS, 0, length - BAND_W), BAND_RADIUS)
        didx = (q0 - start) // BAND_RADIUS
        q = q_ref[0, 0, pl.ds(q0, BAND_Q), :]
        kw = k_ref[0, 0, pl.ds(start, BAND_W), :]
        vw = v_ref[0, 0, pl.ds(start, BAND_W), :]
        outs, lses = [], []
        for hh in range(2):
            sel = first if hh == 0 else jnp.logical_not(first)
            qh = jnp.where(sel, q, jnp.zeros_like(q))
            s = lax.dot_general(qh, kw, (((1,), (1,)), ((), ())), preferred_element_type=jnp.float32)
            s = s * scale + bias_ref[didx, hh]
            m = jnp.max(s, axis=1, keepdims=True)
            e = jnp.exp(s - m)
            den = jnp.sum(e, axis=1, keepdims=True)
            p = (e / den).astype(jnp.bfloat16)
            outs.append(jnp.dot(p, vw, preferred_element_type=jnp.float32))
            lses.append(m + jnp.log(den))
        o_ref[0, 0, pl.ds(q0, BAND_Q), :] = jnp.where(first, outs[0], outs[1]).astype(o_ref.dtype)
        lse_ref[0, 0, pl.ds(q0, BAND_Q), :] = jnp.where(first, lses[0], lses[1])
        return carry

    n_blocks = length // BAND_Q
    lax.fori_loop(0, n_blocks, body, 0, unroll=min(4, n_blocks))


def band_group(src, group, dil, batch, seq, col_blocks):
    length = seq // dil
    pairs = DIL_HEADS_PER_GROUP // 2

    def zspec(cb):
        return pl.BlockSpec((1, 1, length, LANES), lambda b, r, hp: (b, r, 0, cb + hp))

    ospec = pl.BlockSpec((1, 1, length, LANES), lambda b, r, hp: (b, r, 0, hp))
    oshape = (batch, dil, length, DIL_GROUP_WIDTH)
    return pl.pallas_call(
        functools.partial(_band_kernel, length=length),
        grid=(batch, dil, pairs),
        in_specs=[zspec(col_blocks[0]), zspec(col_blocks[1]), zspec(col_blocks[2]),
                  pl.BlockSpec((3, 2, BAND_Q, BAND_W), lambda b, r, hp: (0, hp, 0, 0))],
        out_specs=[ospec, ospec],
        out_shape=[jax.ShapeDtypeStruct(oshape, jnp.bfloat16), jax.ShapeDtypeStruct(oshape, jnp.float32)],
        compiler_params=_cparams("arbitrary", "arbitrary", "arbitrary"),
        name=f"band_attention_g{group}",
    )(src, src, src, band_bias_table(group, dil))


RET_C = 256


def _ret_kernel(lg_ref, q_ref, k_ref, v_ref, g_ref, cos_ref, sin_ref, o_ref, qs_ref, ks_ref, o1_ref, *, n_c):
    h = pl.program_id(1)
    lgf = lg_ref[0, h]
    lgb = lg_ref[1, h]
    c = RET_C
    f32 = jnp.float32
    bf = jnp.bfloat16
    ii = lax.broadcasted_iota(jnp.int32, (c, c), 0)
    jj = lax.broadcasted_iota(jnp.int32, (c, c), 1)
    diff = (ii - jj).astype(f32)
    decay = jnp.where(diff >= 0.0, jnp.exp(lgf * jnp.maximum(diff, 0.0)), jnp.exp(lgb * jnp.maximum(-diff, 0.0)))
    idx = lax.broadcasted_iota(jnp.int32, (c, 1), 0).astype(f32)
    xi_f = jnp.exp(lgf * (idx + 1.0))
    zeta_f = jnp.exp(lgf * (c - 1.0 - idx))
    xi_b = jnp.exp(lgb * (c - idx))
    zeta_b = jnp.exp(lgb * idx)
    cd_f = jnp.exp(lgf * c)
    cd_b = jnp.exp(lgb * c)
    kscale = RET_QK ** -0.5

    def rope(x, rows):
        return x * cos_ref[rows, :] + pltpu.roll(x, RET_QK // 2, 1) * sin_ref[rows, :]

    def fwd(n, state):
        rows = pl.ds(pl.multiple_of(n * c, c), c)
        q = rope(q_ref[0, rows, :].astype(f32), rows)
        k = rope(k_ref[0, rows, :].astype(f32), rows) * kscale
        v = v_ref[0, rows, :]
        qs_ref[rows, :] = q
        ks_ref[rows, :] = k
        qb = q.astype(bf)
        s = lax.dot_general(qb, k.astype(bf), (((1,), (1,)), ((), ())), preferred_element_type=f32) * decay
        o = jnp.dot(s.astype(bf), v, preferred_element_type=f32)
        o = o + xi_f * jnp.dot(qb, state.astype(bf), preferred_element_type=f32)
        o1_ref[rows, :] = o
        kz = (k * zeta_f).T.astype(bf)
        return cd_f * state + jnp.dot(kz, v, preferred_element_type=f32)

    lax.fori_loop(0, n_c, fwd, jnp.zeros((RET_QK, RET_V), f32))

    def bwd(step, state):
        n = n_c - 1 - step
        rows = pl.ds(pl.multiple_of(n * c, c), c)
        q = qs_ref[rows, :]
        k = ks_ref[rows, :]
        v = v_ref[0, rows, :]
        of = o1_ref[rows, :] + xi_b * jnp.dot(q.astype(bf), state.astype(bf), preferred_element_type=f32)
        mu = jnp.mean(of, axis=-1, keepdims=True)
        var = jnp.mean(jnp.square(of - mu), axis=-1, keepdims=True)
        of = (of - mu) * lax.rsqrt(var + EPS)
        g = g_ref[0, rows, :].astype(f32)
        o_ref[0, rows, :] = (g * jax.nn.sigmoid(g) * of).astype(o_ref.dtype)
        kz = (k * zeta_b).T.astype(bf)
        return cd_b * state + jnp.dot(kz, v, preferred_element_type=f32)

    lax.fori_loop(0, n_c, bwd, jnp.zeros((RET_QK, RET_V), f32))


def ret_rope_tables(seq):
    half = RET_QK // 2
    inv = ROPE_BASE ** (-jnp.arange(half, dtype=jnp.float32) / half)
    ang = jnp.arange(seq, dtype=jnp.float32)[:, None] * inv[None, :]
    cos = jnp.concatenate([jnp.cos(ang), jnp.cos(ang)], axis=1)
    sin = jnp.concatenate([-jnp.sin(ang), jnp.sin(ang)], axis=1)
    return cos, sin


def retention(z, ret_decay, tables, batch, seq):
    zv = z.reshape(batch, seq, Z_COLS)
    log_gamma = jax.nn.log_sigmoid(ret_decay.astype(jnp.float32))
    cos, sin = tables

    def zspec(cb):
        return pl.BlockSpec((1, seq, LANES), lambda b, h: (b, 0, cb + h))

    tspec = pl.BlockSpec((seq, RET_QK), lambda b, h: (0, 0))
    out = pl.pallas_call(
        functools.partial(_ret_kernel, n_c=seq // RET_C),
        grid=(batch, RET_HEADS),
        in_specs=[pl.BlockSpec(memory_space=pltpu.SMEM),
                  zspec(CB_RQ), zspec(CB_RK), zspec(CB_RV), zspec(CB_RG), tspec, tspec],
        out_specs=pl.BlockSpec((1, seq, LANES), lambda b, h: (b, 0, h)),
        out_shape=jax.ShapeDtypeStruct((batch, seq, RET_HEADS * RET_V), jnp.bfloat16),
        scratch_shapes=[pltpu.VMEM((seq, RET_QK), jnp.float32)] * 3,
        compiler_params=_cparams("arbitrary", "arbitrary"),
        name="retention",
    )(log_gamma, zv, zv, zv, zv, cos, sin)
    return out.reshape(batch * seq, RET_HEADS * RET_V)


MERGE_TM = 512


def _merge_kernel(x_ref, ya_ref, yb_ref, o0_ref, o1_ref, o2_ref, l0_ref, l1_ref, l2_ref, yd_ref,
                  zg_ref, wb_ref, wo_ref, out_ref, tok_ref):
    f32 = jnp.float32

    def token_order(slot, src_ref):
        dil, rows, width = src_ref.shape[1:]
        tiles = width // LANES
        for r in range(dil):
            v = src_ref[0, r].astype(f32)
            for c in range(tiles):
                tok_ref[slot * tiles + c, pl.ds(r, rows, stride=dil), :] = v[:, c * LANES:(c + 1) * LANES]
        return jnp.concatenate([tok_ref[slot * tiles + c] for c in range(tiles)], axis=1)

    l0 = l0_ref[0, 0]
    l1, l2 = token_order(0, l1_ref), token_order(1, l2_ref)
    o1, o2 = token_order(2, o1_ref), token_order(3, o2_ref)
    m = jnp.maximum(jnp.maximum(l0, l1), l2)
    e0, e1, e2 = jnp.exp(l0 - m), jnp.exp(l1 - m), jnp.exp(l2 - m)
    inv = 1.0 / (e0 + e1 + e2)
    yc = ((e0 * inv) * o0_ref[0, 0].astype(f32) + (e1 * inv) * o1 + (e2 * inv) * o2).astype(jnp.bfloat16)
    merged = None
    for i, y in enumerate((ya_ref[...], yb_ref[...], yc, yd_ref[...])):
        gate = jax.nn.sigmoid(zg_ref[:, i * D_MODEL:(i + 1) * D_MODEL].astype(f32))
        term = gate * jnp.dot(y, wb_ref[i], preferred_element_type=f32)
        merged = term if merged is None else merged + term
    out_ref[...] = x_ref[...] + jnp.dot(merged.astype(jnp.bfloat16), wo_ref[...], preferred_element_type=f32)


def merge_project(x, ya, yb, dil_o, dil_lse, yd, z, w_branch, w_out, seq):
    n = x.shape[0]
    tm = MERGE_TM
    n_t = seq // tm
    width = 4 * LANES
    row = lambda i: (i, 0)
    bspec = pl.BlockSpec((tm, width), row)

    def dspec(dil):
        return pl.BlockSpec((1, dil, tm // dil, width), lambda i: (i // n_t, 0, i % n_t, 0))

    dspecs = [dspec(d) for _, d in DIL_PAIRS]
    return pl.pallas_call(
        _merge_kernel,
        grid=(n // tm,),
        in_specs=[pl.BlockSpec((tm, D_MODEL), row), bspec, bspec, *dspecs, *dspecs, bspec,
                  pl.BlockSpec((tm, N_BRANCHES * D_MODEL), lambda i: (i, CB_ZG * LANES // (N_BRANCHES * D_MODEL))),
                  pl.BlockSpec((N_BRANCHES, width, D_MODEL), lambda i: (0, 0, 0)),
                  pl.BlockSpec((D_MODEL, D_MODEL), lambda i: (0, 0))],
        out_specs=pl.BlockSpec((tm, D_MODEL), row),
        out_shape=jax.ShapeDtypeStruct((n, D_MODEL), jnp.float32),
        scratch_shapes=[pltpu.VMEM((4 * width // LANES, tm, LANES), jnp.float32)],
        compiler_params=_cparams("arbitrary"),
        name="merge_project",
    )(x, ya, yb, *dil_o, *dil_lse, yd, z, w_branch, w_out)


MOE_TM = 512
MOE_TT = 256
MOE_W_SMALL = 64
MOE_ROWS = 1024
MOE_ALIGN = 8
MOE_XW = D_MODEL + LANES


def _router_kernel(x_ref, g_ref, w_ref, h_ref, aff_ref):
    x = x_ref[...]
    y = x * lax.rsqrt(jnp.mean(x * x, axis=-1, keepdims=True) + EPS)
    h = (y * g_ref[...]).astype(jnp.bfloat16)
    h_ref[...] = h
    logits = jnp.dot(h, w_ref[...], preferred_element_type=jnp.float32)
    lane = lax.broadcasted_iota(jnp.int32, logits.shape, 1)
    logits = jnp.where(lane < N_EXPERTS, logits, NEG_INF)
    e = jnp.exp(logits - jnp.max(logits, axis=-1, keepdims=True))
    aff_t = (e / jnp.sum(e, axis=-1, keepdims=True)).T
    for c in range(x.shape[0] // MOE_TT):
        aff_ref[c] = aff_t[:N_EXPERTS, c * MOE_TT:(c + 1) * MOE_TT]


def moe_router(x, norm_g, w_router):
    n, d = x.shape
    tm = 2 * MOE_TT
    w = jnp.pad(w_router.astype(jnp.bfloat16), ((0, 0), (0, LANES - N_EXPERTS)))
    return pl.pallas_call(
        _router_kernel,
        grid=(n // tm,),
        in_specs=[pl.BlockSpec((tm, d), lambda i: (i, 0)), pl.BlockSpec((1, d), lambda i: (0, 0)),
                  pl.BlockSpec((d, LANES), lambda i: (0, 0))],
        out_specs=[pl.BlockSpec((tm, d), lambda i: (i, 0)),
                   pl.BlockSpec((tm // MOE_TT, N_EXPERTS, MOE_TT), lambda i: (i, 0, 0))],
        out_shape=[jax.ShapeDtypeStruct((n, d), jnp.bfloat16),
                   jax.ShapeDtypeStruct((n // MOE_TT, N_EXPERTS, MOE_TT), jnp.float32)],
        compiler_params=_cparams("arbitrary"),
        name="moe_router",
    )(x, norm_g.reshape(1, d), w)


def _affinity_bits(a):
    return lax.bitcast_convert_type(a, jnp.int32)


def _threshold_kernel(aff_ref, thr_ref, need_ref, *, cap, n_tiles):
    def count(pred, thr):
        def tile(c, acc):
            return acc + jnp.where(pred(_affinity_bits(aff_ref[c]), thr), 1.0, 0.0)
        acc = lax.fori_loop(0, n_tiles, tile, jnp.zeros((N_EXPERTS, MOE_TT), jnp.float32), unroll=8)
        return jnp.sum(acc, axis=1, keepdims=True)

    def bit(i, thr):
        cand = thr | jnp.left_shift(jnp.int32(1), 30 - i)
        return jnp.where(count(lambda b, t: b >= t, cand) >= cap, cand, thr)

    thr = lax.fori_loop(0, 31, bit, jnp.zeros((N_EXPERTS, 1), jnp.int32))
    thr_ref[...] = jnp.broadcast_to(thr, thr_ref.shape)
    need_ref[...] = jnp.broadcast_to(cap - count(lambda b, t: b > t, thr), need_ref.shape)


def moe_threshold(aff, cap):
    n_tiles = aff.shape[0]
    full = pl.BlockSpec((N_EXPERTS, LANES), lambda i: (0, 0))
    return pl.pallas_call(
        functools.partial(_threshold_kernel, cap=cap, n_tiles=n_tiles),
        grid=(1,),
        in_specs=[pl.BlockSpec(aff.shape, lambda i: (0, 0, 0))],
        out_specs=[full, full],
        out_shape=[jax.ShapeDtypeStruct((N_EXPERTS, LANES), jnp.int32),
                   jax.ShapeDtypeStruct((N_EXPERTS, LANES), jnp.float32)],
        compiler_params=_cparams("arbitrary"),
        name="moe_threshold",
    )(aff)


def _assign_kernel(aff_ref, thr_ref, need_ref, codet_ref, coden_ref, start_ref, total_ref, ties_ref, run_ref):
    @pl.when(pl.program_id(0) == 0)
    def _():
        ties_ref[...] = jnp.zeros_like(ties_ref)
        run_ref[...] = jnp.zeros_like(run_ref)

    f32 = jnp.float32
    bits = _affinity_bits(aff_ref[0])
    thr = thr_ref[:, :1]
    ii = lax.broadcasted_iota(jnp.int32, (MOE_TT, MOE_TT), 0)
    jj = lax.broadcasted_iota(jnp.int32, (MOE_TT, MOE_TT), 1)
    tri = jnp.where(ii <= jj, 1.0, 0.0).astype(jnp.bfloat16)
    eq = bits == thr
    eq_f = jnp.where(eq, 1.0, 0.0)
    tie_rank = jnp.dot(eq_f.astype(jnp.bfloat16), tri, preferred_element_type=f32) + ties_ref[:, :1]
    sel = jnp.logical_or(bits > thr, jnp.logical_and(eq, tie_rank <= need_ref[:, :1]))
    sel_f = jnp.where(sel, 1.0, 0.0)
    incl = jnp.dot(sel_f.astype(jnp.bfloat16), tri, preferred_element_type=f32)
    code = jnp.where(sel, incl - 1.0, -1.0)
    codet_ref[0] = code.astype(jnp.int32)
    padded = jnp.concatenate([code, jnp.zeros((LANES - N_EXPERTS, MOE_TT), f32)], axis=0)
    coden_ref[...] = padded.T.astype(jnp.int32)
    start_ref[0] = run_ref[...].astype(jnp.int32)
    taken = jnp.sum(sel_f, axis=1, keepdims=True)
    run_ref[...] = run_ref[...] + jnp.floor((taken + (MOE_ALIGN - 1)) * (1.0 / MOE_ALIGN)) * MOE_ALIGN
    total_ref[...] = run_ref[...].astype(jnp.int32)
    ties_ref[...] = ties_ref[...] + jnp.sum(eq_f, axis=1, keepdims=True)


def moe_assign(aff, thr, need):
    n_tiles = aff.shape[0]
    tile = pl.BlockSpec((1, N_EXPERTS, MOE_TT), lambda c: (c, 0, 0))
    full = pl.BlockSpec((N_EXPERTS, LANES), lambda c: (0, 0))
    return pl.pallas_call(
        _assign_kernel,
        grid=(n_tiles,),
        in_specs=[tile, full, full],
        out_specs=[tile, pl.BlockSpec((MOE_TT, LANES), lambda c: (c, 0)),
                   pl.BlockSpec((1, N_EXPERTS, LANES), lambda c: (c, 0, 0)), full],
        out_shape=[jax.ShapeDtypeStruct((n_tiles, N_EXPERTS, MOE_TT), jnp.int32),
                   jax.ShapeDtypeStruct((n_tiles * MOE_TT, LANES), jnp.int32),
                   jax.ShapeDtypeStruct((n_tiles, N_EXPERTS, LANES), jnp.int32),
                   jax.ShapeDtypeStruct((N_EXPERTS, LANES), jnp.int32)],
        scratch_shapes=[pltpu.VMEM((N_EXPERTS, LANES), jnp.float32)] * 2,
        compiler_params=_cparams("arbitrary"),
        name="moe_assign",
    )(aff, thr, need)


def _dispatch_kernel(start_ref, big_ref, h_ref, code_ref, aff_ref, init_hbm, x_hbm, buf_ref, sem):
    del init_hbm
    c = pl.program_id(0)
    f32 = jnp.float32
    h = h_ref[...]
    code = code_ref[0]
    gate = aff_ref[0]

    def run(width):
        group = MOE_ROWS // width
        slot = lax.broadcasted_iota(jnp.int32, (width, MOE_TT), 0)
        for first in range(0, N_EXPERTS, group):
            onehots = []
            for k in range(group):
                e = first + k
                hit = slot == code[e:e + 1, :]
                onehots.append(jnp.where(hit, 1.0, 0.0).astype(jnp.bfloat16))
                gwin = jnp.sum(jnp.where(hit, gate[e:e + 1, :], 0.0), axis=1, keepdims=True)
                buf_ref[k * width:(k + 1) * width, D_MODEL:] = jnp.broadcast_to(gwin, (width, LANES))
            buf_ref[:, :D_MODEL] = jnp.dot(jnp.concatenate(onehots, axis=0), h, preferred_element_type=f32)
            copies = [pltpu.make_async_copy(
                buf_ref.at[pl.ds(k * width, width)],
                x_hbm.at[first + k, pl.ds(_slot_start(start_ref, c, first + k), width)],
                sem.at[k]) for k in range(group)]
            for cp in copies:
                cp.start()
            for cp in copies:
                cp.wait()

    @pl.when(big_ref[c] == 0)
    def _():
        run(MOE_W_SMALL)

    @pl.when(big_ref[c] != 0)
    def _():
        run(MOE_TT)


def _slot_start(start_ref, tile, expert):
    return pl.multiple_of(start_ref[tile * N_EXPERTS + expert], MOE_ALIGN)


def moe_dispatch(starts, big, h, code_t, aff, slots):
    n, d = h.shape
    n_tiles = n // MOE_TT
    tile = pl.BlockSpec((1, N_EXPERTS, MOE_TT), lambda c, s, b: (c, 0, 0))
    shape = (N_EXPERTS, slots, MOE_XW)
    return pl.pallas_call(
        _dispatch_kernel,
        grid_spec=pltpu.PrefetchScalarGridSpec(
            num_scalar_prefetch=2,
            grid=(n_tiles,),
            in_specs=[pl.BlockSpec((MOE_TT, d), lambda c, s, b: (c, 0)), tile, tile,
                      pl.BlockSpec(memory_space=pl.ANY)],
            out_specs=pl.BlockSpec(memory_space=pl.ANY),
            scratch_shapes=[pltpu.VMEM((MOE_ROWS, MOE_XW), jnp.float32),
                            pltpu.SemaphoreType.DMA((MOE_ROWS // MOE_W_SMALL,))],
        ),
        out_shape=jax.ShapeDtypeStruct(shape, jnp.float32),
        input_output_aliases={5: 0},
        compiler_params=_cparams("arbitrary"),
        name="moe_dispatch",
    )(starts, big, h, code_t, aff, jnp.zeros(shape, jnp.float32))


def _expert_kernel(total_ref, x_ref, wg_ref, wu_ref, wd_ref, o_ref):
    f32 = jnp.float32
    used = pl.program_id(1) * MOE_TM < total_ref[pl.program_id(0)]

    @pl.when(used)
    def _():
        x = x_ref[0, :, :D_MODEL].astype(jnp.bfloat16)
        a = jnp.dot(x, wg_ref[0], preferred_element_type=f32)
        u = jnp.dot(x, wu_ref[0], preferred_element_type=f32)
        he = (a * jax.nn.sigmoid(a) * u).astype(jnp.bfloat16)
        o_ref[0] = jnp.dot(he, wd_ref[0], preferred_element_type=f32) * x_ref[0, :, D_MODEL:D_MODEL + 1]

    @pl.when(jnp.logical_not(used))
    def _():
        o_ref[0] = jnp.zeros_like(o_ref[0])


def expert_ffn(total, xd, w_gate, w_up, w_down):
    e, slots, _ = xd.shape
    d = D_MODEL
    tm = MOE_TM
    wspec = pl.BlockSpec((1, d, D_EXPERT), lambda i, j, t: (i, 0, 0))
    return pl.pallas_call(
        _expert_kernel,
        grid_spec=pltpu.PrefetchScalarGridSpec(
            num_scalar_prefetch=1,
            grid=(e, slots // tm),
            in_specs=[pl.BlockSpec((1, tm, MOE_XW), lambda i, j, t: (i, j, 0)),
                      wspec, wspec,
                      pl.BlockSpec((1, D_EXPERT, d), lambda i, j, t: (i, 0, 0))],
            out_specs=pl.BlockSpec((1, tm, d), lambda i, j, t: (i, j, 0)),
        ),
        out_shape=jax.ShapeDtypeStruct((e, slots, d), jnp.float32),
        compiler_params=_cparams("arbitrary", "arbitrary"),
        name="expert_ffn",
    )(total, xd, w_gate, w_up, w_down)


def _combine_kernel(start_ref, big_ref, x_ref, code_ref, y_hbm, o_ref, buf_ref, sem):
    c = pl.program_id(0)
    f32 = jnp.float32
    bf = jnp.bfloat16
    code = code_ref[...]
    o_ref[...] = x_ref[...]

    def run(width):
        group = MOE_ROWS // width
        for first in range(0, N_EXPERTS, group):
            copies = [pltpu.make_async_copy(
                y_hbm.at[first + k, pl.ds(_slot_start(start_ref, c, first + k), width)],
                buf_ref.at[pl.ds(k * width, width)],
                sem.at[k]) for k in range(group)]
            for cp in copies:
                cp.start()
            pieces = []
            if width < LANES:
                lane = lax.broadcasted_iota(jnp.int32, (MOE_TT, LANES), 1)
                for k in range(0, group, 2):
                    e = first + k
                    target = jnp.where(lane < width, code[:, e:e + 1], code[:, e + 1:e + 2] + width)
                    pieces.append(jnp.where(target == lane, 1.0, 0.0).astype(bf))
            else:
                lane = lax.broadcasted_iota(jnp.int32, (MOE_TT, width), 1)
                for k in range(group):
                    e = first + k
                    pieces.append(jnp.where(code[:, e:e + 1] == lane, 1.0, 0.0).astype(bf))
            onehot = jnp.concatenate(pieces, axis=1)
            for cp in copies:
                cp.wait()
            y = buf_ref[...]
            y_hi = y.astype(bf)
            y_lo = (y - y_hi.astype(f32)).astype(bf)
            o_ref[...] += (jnp.dot(onehot, y_hi, preferred_element_type=f32)
                           + jnp.dot(onehot, y_lo, preferred_element_type=f32))

    @pl.when(big_ref[c] == 0)
    def _():
        run(MOE_W_SMALL)

    @pl.when(big_ref[c] != 0)
    def _():
        run(MOE_TT)


def moe_combine(starts, big, x, code_n, y):
    n, d = x.shape
    row = pl.BlockSpec((MOE_TT, d), lambda c, s, b: (c, 0))
    return pl.pallas_call(
        _combine_kernel,
        grid_spec=pltpu.PrefetchScalarGridSpec(
            num_scalar_prefetch=2,
            grid=(n // MOE_TT,),
            in_specs=[row, pl.BlockSpec((MOE_TT, LANES), lambda c, s, b: (c, 0)),
                      pl.BlockSpec(memory_space=pl.ANY)],
            out_specs=row,
            scratch_shapes=[pltpu.VMEM((MOE_ROWS, d), jnp.float32),
                            pltpu.SemaphoreType.DMA((MOE_ROWS // MOE_W_SMALL,))],
        ),
        out_shape=jax.ShapeDtypeStruct((n, d), jnp.float32),
        compiler_params=_cparams("arbitrary"),
        name="moe_combine",
    )(starts, big, x, code_n, y)


def ec_moe(x, norm_g, w_router, w_gate, w_up, w_down):
    n_tok, _ = x.shape
    cap = EC_FACTOR * n_tok // N_EXPERTS
    h, aff = moe_router(x, norm_g, w_router)
    thr, need = moe_threshold(aff, cap)
    code_t, code_n, starts, total = moe_assign(aff, thr, need)
    starts = starts[:, :, 0]
    total = total[:, 0]
    ends = jnp.concatenate([starts[1:], total[None]], axis=0)
    big = (jnp.max(ends - starts, axis=1) > MOE_W_SMALL).astype(jnp.int32)
    starts = starts.reshape(-1)
    n_tiles = n_tok // MOE_TT
    slots = -(-(cap + MOE_ALIGN * n_tiles + MOE_TT) // MOE_TM) * MOE_TM
    xd = moe_dispatch(starts, big, h, code_t, aff, slots)
    y = expert_ffn(total, xd, w_gate, w_up, w_down)
    return moe_combine(starts, big, x, code_n, y)


def split_in_proj(w_in):
    w = w_in.astype(jnp.bfloat16)
    sizes = (LRU_WIDTH, LRU_WIDTH, MLA_Q_RANK, MLA_KV_RANK, MLA_ROPE,
             DIL_HEADS * DIL_HEAD_DIM, DIL_HEADS * DIL_HEAD_DIM, DIL_HEADS * DIL_HEAD_DIM,
             RET_HEADS * RET_QK, RET_HEADS * RET_QK, RET_HEADS * RET_V, RET_HEADS * RET_V,
             N_BRANCHES * D_MODEL)
    parts, off = [], 0
    for s in sizes:
        parts.append(w[:, off:off + s])
        off += s
    xa, ga, cq, ckv, kr, dq, dk, dv, rq, rk, rv, rg, zg = parts
    gw = DIL_GROUP_WIDTH
    grp = lambda a, g: a[:, g * gw:(g + 1) * gw]
    zeros = jnp.zeros((w.shape[0], Z_PAD), w.dtype)
    main = jnp.concatenate([zg, xa, ga, cq, ckv, kr, zeros, grp(dq, 0), grp(dk, 0), grp(dv, 0),
                            rq, rk, rv, rg], axis=1)
    dil = [jnp.concatenate([grp(dq, g), grp(dk, g), grp(dv, g)], axis=1) for g in range(1, len(DIL_PAIRS))]
    return main, dil


def mixer(x, batch, seq, norm_g, w_in, conv_w, conv_b, lru_gate_w, lru_gate_b, lru_lambda, mla_q_norm,
          mla_kv_norm, w_uq, w_ukv, ret_decay, w_branch, w_out, mla_tables, ret_tables):
    n = batch * seq
    w_main, w_dil = split_in_proj(w_in)
    z = norm_matmul(x, norm_g, w_main, jnp.bfloat16, min(1024, n), 1024)
    wg, gb = lru_gate_dense(lru_gate_w, lru_gate_b)
    h_fwd = lru_direction(z, conv_w, conv_b, wg, gb, lru_lambda, batch, seq, False)
    ya = lru_direction(z, conv_w, conv_b, wg, gb, lru_lambda, batch, seq, True, h_fwd)
    q, k, v = mla_project(z, mla_q_norm, mla_kv_norm, mla_weights(w_uq, w_ukv), mla_tables, batch, seq)
    yb = mla_attention(q, k, v, batch, seq)
    dil = [band_group(z.reshape(batch, 1, seq, Z_COLS), 0, 1, batch, seq, (CB_DQ, CB_DK, CB_DV))]
    pairs = DIL_HEADS_PER_GROUP // 2
    for g in range(1, len(DIL_PAIRS)):
        d = DIL_PAIRS[g][1]
        zd = norm_matmul_dil(x, norm_g, w_dil[g - 1], d, batch, seq, min(1024, seq))
        dil.append(band_group(zd, g, d, batch, seq, (0, pairs, 2 * pairs)))
    yd = retention(z, ret_decay, ret_tables, batch, seq)
    return merge_project(x, ya, yb, [o for o, _ in dil], [l for _, l in dil], yd, z,
                         w_branch.astype(jnp.bfloat16), w_out.astype(jnp.bfloat16), seq)


def trunk(x, norm_mix, w_in, conv_w, conv_b, lru_gate_w, lru_gate_b, lru_lambda, mla_q_norm, mla_kv_norm,
          w_uq, w_ukv, ret_decay, w_branch, w_out, norm_ffn, w_router, w_gate, w_up, w_down, norm_final):
    batch, seq, d = x.shape
    x = x.reshape(batch * seq, d)
    mla_tables = mla_rope_tables(seq)
    ret_tables = ret_rope_tables(seq)
    bf = jnp.bfloat16
    for l in range(norm_mix.shape[0]):
        x = mixer(x, batch, seq, norm_mix[l], w_in[l], conv_w[l], conv_b[l], lru_gate_w[l], lru_gate_b[l],
                  lru_lambda[l], mla_q_norm[l], mla_kv_norm[l], w_uq[l], w_ukv[l], ret_decay[l],
                  w_branch[l], w_out[l], mla_tables, ret_tables)
        x = ec_moe(x, norm_ffn[l], w_router[l], w_gate[l].astype(bf), w_up[l].astype(bf), w_down[l].astype(bf))
    return rmsnorm_pallas(x, norm_final, jnp.float32, min(1024, batch * seq)).reshape(batch, seq, d)


def kernel(x_prompt, x_sample, norm_mix, w_in, conv_w, conv_b, lru_gate_w, lru_gate_b, lru_lambda,
           mla_q_norm, mla_kv_norm, w_uq, w_ukv, ret_decay, w_branch, w_out, norm_ffn, w_router,
           w_gate, w_up, w_down, norm_final):
    args = (norm_mix, w_in, conv_w, conv_b, lru_gate_w, lru_gate_b, lru_lambda, mla_q_norm, mla_kv_norm,
            w_uq, w_ukv, ret_decay, w_branch, w_out, norm_ffn, w_router, w_gate, w_up, w_down, norm_final)
    return trunk(x_prompt, *args), trunk(x_sample, *args)
```

```python
import functools
import math

import numpy as np
import jax
import jax.numpy as jnp
from jax import lax
from jax.experimental import pallas as pl
from jax.experimental.pallas import tpu as pltpu

D_MODEL = 1024
DEPTH = 4
EPS = 1e-6
NEG_INF = -1e30
ROPE_BASE = 10000.0
N_BRANCHES = 4
LRU_WIDTH = 512
LRU_BLOCKS = 8
LRU_BLOCK = LRU_WIDTH // LRU_BLOCKS
CONV_WIDTH = 4
CONV_LEFT = 2
LRU_C = 8.0
MLA_HEADS = 8
MLA_NOPE = 64
MLA_ROPE = 32
MLA_V = 64
MLA_Q_RANK = 256
MLA_KV_RANK = 128
DIL_PAIRS = ((128, 1), (512, 4), (2048, 16))
DIL_HEADS_PER_GROUP = 8
DIL_HEADS = DIL_HEADS_PER_GROUP * len(DIL_PAIRS)
DIL_HEAD_DIM = 64
RET_HEADS = 4
RET_QK = 128
RET_V = 128
N_EXPERTS = 16
EC_FACTOR = 2
D_EXPERT = 1024

LANES = 128
VMEM_LIMIT_BYTES = 56 * 1024 * 1024

Z_COLS = 9216
Z_PAD = 96
CB_ZG = 0
CB_XA, CB_GA = 32, 36
CB_MLA = 40
CB_DQ, CB_DK, CB_DV = 44, 48, 52
CB_RQ, CB_RK, CB_RV, CB_RG = 56, 60, 64, 68
DIL_GROUP_WIDTH = DIL_HEADS_PER_GROUP * DIL_HEAD_DIM


def _cparams(*sem):
    return pltpu.CompilerParams(dimension_semantics=sem, vmem_limit_bytes=VMEM_LIMIT_BYTES)


def _norm_matmul_kernel(x_ref, g_ref, w_ref, o_ref, h_ref):
    @pl.when(pl.program_id(1) == 0)
    def _():
        x = x_ref[...]
        y = x * lax.rsqrt(jnp.mean(x * x, axis=-1, keepdims=True) + EPS)
        h_ref[...] = (y * g_ref[...]).astype(h_ref.dtype)

    o_ref[...] = jnp.dot(h_ref[...], w_ref[...], preferred_element_type=jnp.float32).astype(o_ref.dtype)


def norm_matmul(x, g, w, out_dtype, tm, tn):
    n, d = x.shape
    c = w.shape[1]
    return pl.pallas_call(
        _norm_matmul_kernel,
        grid=(n // tm, c // tn),
        in_specs=[
            pl.BlockSpec((tm, d), lambda i, j: (i, 0)),
            pl.BlockSpec((1, d), lambda i, j: (0, 0)),
            pl.BlockSpec((d, tn), lambda i, j: (0, j)),
        ],
        out_specs=pl.BlockSpec((tm, tn), lambda i, j: (i, j)),
        out_shape=jax.ShapeDtypeStruct((n, c), out_dtype),
        scratch_shapes=[pltpu.VMEM((tm, d), jnp.bfloat16)],
        compiler_params=_cparams("arbitrary", "arbitrary"),
        name="norm_matmul",
    )(x, g.reshape(1, d), w)


def _norm_matmul_dil_kernel(x_ref, g_ref, w_ref, o_ref, hf_ref, hp_ref, *, dil):
    x = x_ref[...]
    y = x * lax.rsqrt(jnp.mean(x * x, axis=-1, keepdims=True) + EPS)
    y = y * g_ref[...]
    rows = x.shape[0] // dil
    for c in range(x.shape[1] // LANES):
        cols = slice(c * LANES, (c + 1) * LANES)
        hf_ref[c] = y[:, cols]
        for r in range(dil):
            hp_ref[r * rows:(r + 1) * rows, cols] = hf_ref[c, pl.ds(r, rows, stride=dil), :].astype(hp_ref.dtype)
    out = jnp.dot(hp_ref[...], w_ref[...], preferred_element_type=jnp.float32)
    for r in range(dil):
        o_ref[0, r] = out[r * rows:(r + 1) * rows, :].astype(o_ref.dtype)


def norm_matmul_dil(x, g, w, dil, batch, seq, tm):
    n, d = x.shape
    c = w.shape[1]
    n_t = seq // tm
    return pl.pallas_call(
        functools.partial(_norm_matmul_dil_kernel, dil=dil),
        grid=(n // tm,),
        in_specs=[
            pl.BlockSpec((tm, d), lambda i: (i, 0)),
            pl.BlockSpec((1, d), lambda i: (0, 0)),
            pl.BlockSpec((d, c), lambda i: (0, 0)),
        ],
        out_specs=pl.BlockSpec((1, dil, tm // dil, c), lambda i: (i // n_t, 0, i % n_t, 0)),
        out_shape=jax.ShapeDtypeStruct((batch, dil, seq // dil, c), jnp.bfloat16),
        scratch_shapes=[pltpu.VMEM((d // LANES, tm, LANES), jnp.float32), pltpu.VMEM((tm, d), jnp.bfloat16)],
        compiler_params=_cparams("arbitrary"),
        name=f"norm_matmul_dil{dil}",
    )(x, g.reshape(1, d), w)


def _rmsnorm_kernel(x_ref, g_ref, o_ref):
    x = x_ref[...]
    y = x * lax.rsqrt(jnp.mean(x * x, axis=-1, keepdims=True) + EPS)
    o_ref[...] = (y * g_ref[...]).astype(o_ref.dtype)


def rmsnorm_pallas(x, g, out_dtype, tm):
    n, d = x.shape
    return pl.pallas_call(
        _rmsnorm_kernel,
        grid=(n // tm,),
        in_specs=[pl.BlockSpec((tm, d), lambda i: (i, 0)), pl.BlockSpec((1, d), lambda i: (0, 0))],
        out_specs=pl.BlockSpec((tm, d), lambda i: (i, 0)),
        out_shape=jax.ShapeDtypeStruct((n, d), out_dtype),
        compiler_params=_cparams("arbitrary"),
        name="rmsnorm",
    )(x, g.reshape(1, d))


LRU_TC = 256
LRU_HALO = 16


def _softplus(x):
    return jnp.maximum(x, 0.0) + jnp.log(1.0 + jnp.exp(-jnp.abs(x)))


def _gelu_tanh(x):
    return 0.5 * x * (1.0 + jnp.tanh(math.sqrt(2.0 / math.pi) * (x + 0.044715 * (x * x * x))))


def _lru_scan_chunk(a, b, reverse):
    n = a.shape[0]
    row = lax.broadcasted_iota(jnp.int32, a.shape, 0)
    s = 1
    while s < n:
        if reverse:
            keep = row < (n - s)
            a_s = pltpu.roll(a, n - s, 0)
            b_s = pltpu.roll(b, n - s, 0)
        else:
            keep = row >= s
            a_s = pltpu.roll(a, s, 0)
            b_s = pltpu.roll(b, s, 0)
        b = jnp.where(keep, a * b_s + b, b)
        a = jnp.where(keep, a * a_s, a)
        s *= 2
    return a, b


def _lru_kernel(*refs, reverse, n_t):
    if reverse:
        (xp_ref, xc_ref, xn_ref, cw_ref, cb_ref, wg_ref, gb_ref, lam_ref, hf_ref, ga_ref,
         o_ref, carry_ref) = refs
    else:
        (xp_ref, xc_ref, xn_ref, cw_ref, cb_ref, wg_ref, gb_ref, lam_ref, o_ref, carry_ref) = refs
    step = pl.program_id(1)
    t = (n_t - 1 - step) if reverse else step

    @pl.when(step == 0)
    def _():
        carry_ref[...] = jnp.zeros_like(carry_ref)

    prev = jnp.where(t > 0, xp_ref[...].astype(jnp.float32), 0.0)
    nxt = jnp.where(t < n_t - 1, xn_ref[...].astype(jnp.float32), 0.0)
    win = jnp.concatenate([prev, xc_ref[...].astype(jnp.float32), nxt], axis=0)
    xc = cb_ref[...]
    for k in range(CONV_WIDTH):
        lo = LRU_HALO - CONV_LEFT + k
        xc = xc + cw_ref[k:k + 1, :] * win[lo:lo + LRU_TC, :]
    gl = jnp.dot(xc.astype(jnp.bfloat16), wg_ref[0], preferred_element_type=jnp.float32) + gb_ref[0]
    r = jax.nn.sigmoid(gl[:, :LRU_WIDTH])
    i = jax.nn.sigmoid(gl[:, LRU_WIDTH:])
    log_a = (-LRU_C) * r * _softplus(-lam_ref[0])
    a = jnp.exp(log_a)
    b = jnp.sqrt(1.0 - jnp.exp(2.0 * log_a)) * i * xc
    a_cum, b_cum = _lru_scan_chunk(a, b, reverse)
    h = b_cum + a_cum * carry_ref[0:1, :]
    last = 0 if reverse else LRU_TC - 1
    carry_ref[0:1, :] = h[last:last + 1, :]
    if reverse:
        o_ref[...] = (_gelu_tanh(ga_ref[...].astype(jnp.float32)) * (hf_ref[...] + h)).astype(o_ref.dtype)
    else:
        o_ref[...] = h


def lru_direction(z, conv_w, conv_b, wg, gb, lam, batch, seq, reverse, h_fwd=None):
    n_t = seq // LRU_TC
    per_halo = LRU_TC // LRU_HALO
    n_halo = batch * seq // LRU_HALO
    d = 1 if reverse else 0

    def tt(s):
        return (n_t - 1 - s) if reverse else s

    xa_col = CB_XA * LANES // LRU_WIDTH
    ga_col = CB_GA * LANES // LRU_WIDTH

    def cur(b, s):
        return (b * n_t + tt(s), 0)

    def prev(b, s):
        return (jnp.maximum((b * n_t + tt(s)) * per_halo - 1, 0), xa_col)

    def nxt(b, s):
        return (jnp.minimum((b * n_t + tt(s) + 1) * per_halo, n_halo - 1), xa_col)

    const2 = lambda b, s: (0, 0)
    in_specs = [
        pl.BlockSpec((LRU_HALO, LRU_WIDTH), prev),
        pl.BlockSpec((LRU_TC, LRU_WIDTH), lambda b, s: (b * n_t + tt(s), xa_col)),
        pl.BlockSpec((LRU_HALO, LRU_WIDTH), nxt),
        pl.BlockSpec((CONV_WIDTH, LRU_WIDTH), const2),
        pl.BlockSpec((1, LRU_WIDTH), const2),
        pl.BlockSpec((1, LRU_WIDTH, 2 * LRU_WIDTH), lambda b, s: (d, 0, 0)),
        pl.BlockSpec((1, 1, 2 * LRU_WIDTH), lambda b, s: (d, 0, 0)),
        pl.BlockSpec((1, 1, LRU_WIDTH), lambda b, s: (d, 0, 0)),
    ]
    args = [z, z, z, conv_w, conv_b.reshape(1, LRU_WIDTH), wg, gb, lam.reshape(2, 1, LRU_WIDTH)]
    if reverse:
        in_specs += [pl.BlockSpec((LRU_TC, LRU_WIDTH), cur),
                     pl.BlockSpec((LRU_TC, LRU_WIDTH), lambda b, s: (b * n_t + tt(s), ga_col))]
        args += [h_fwd, z]
        out_dtype = jnp.bfloat16
    else:
        out_dtype = jnp.float32
    return pl.pallas_call(
        functools.partial(_lru_kernel, reverse=reverse, n_t=n_t),
        grid=(batch, n_t),
        in_specs=in_specs,
        out_specs=pl.BlockSpec((LRU_TC, LRU_WIDTH), cur),
        out_shape=jax.ShapeDtypeStruct((batch * seq, LRU_WIDTH), out_dtype),
        scratch_shapes=[pltpu.VMEM((8, LRU_WIDTH), jnp.float32)],
        compiler_params=_cparams("arbitrary", "arbitrary"),
        name="lru_bwd" if reverse else "lru_fwd",
    )(*args)


def lru_gate_dense(gate_w, gate_b):
    eye = jnp.eye(LRU_BLOCKS, dtype=gate_w.dtype)
    dense = jnp.einsum('dgnij,nm->dgnimj', gate_w, eye).reshape(2, 2, LRU_WIDTH, LRU_WIDTH)
    wg = jnp.concatenate([dense[:, 0], dense[:, 1]], axis=-1).astype(jnp.bfloat16)
    gb = jnp.concatenate([gate_b[:, 0], gate_b[:, 1]], axis=-1).reshape(2, 1, 2 * LRU_WIDTH)
    return wg, gb


MLA_TM = 512
MLA_TQ = 512
MLA_TK = 512
MLA_DP = 128


def _mla_proj_kernel(z_ref, qn_ref, kn_ref, wqa_ref, wqb_ref, wk_ref, wv_ref, ea_ref, eb_ref,
                     cos_ref, sin_ref, q_ref, k_ref, v_ref):
    z = z_ref[...]
    cq = z[:, :MLA_Q_RANK].astype(jnp.float32)
    ckv = z[:, MLA_Q_RANK:MLA_Q_RANK + MLA_KV_RANK].astype(jnp.float32)
    kr = z[:, MLA_Q_RANK + MLA_KV_RANK:]
    cqn = (cq * lax.rsqrt(jnp.mean(cq * cq, axis=-1, keepdims=True) + EPS) * qn_ref[...]).astype(jnp.bfloat16)
    ckn = (ckv * lax.rsqrt(jnp.mean(ckv * ckv, axis=-1, keepdims=True) + EPS) * kn_ref[...]).astype(jnp.bfloat16)
    cos = cos_ref[...]
    sin = sin_ref[...]
    f32 = jnp.float32
    k_rope = (jnp.dot(kr, ea_ref[...], preferred_element_type=f32) * cos
              + jnp.dot(kr, eb_ref[...], preferred_element_type=f32) * sin)
    scale = (MLA_NOPE + MLA_ROPE) ** -0.5 * math.log2(math.e)
    row = lax.broadcasted_iota(jnp.int32, (MLA_DP, MLA_TM), 0)
    for h in range(MLA_HEADS):
        qa = jnp.dot(cqn, wqa_ref[h], preferred_element_type=f32)
        qb = jnp.dot(cqn, wqb_ref[h], preferred_element_type=f32)
        q_ref[0, h] = ((qa * cos + qb * sin) * scale).astype(q_ref.dtype)
        k_ref[0, h] = (jnp.dot(ckn, wk_ref[h], preferred_element_type=f32) + k_rope).astype(k_ref.dtype)
        vt = lax.dot_general(wv_ref[h], ckn, (((1,), (1,)), ((), ())), preferred_element_type=f32)
        vt = jnp.where(row == _mla_ones_row(h), 1.0, vt)
        for c in range(MLA_TM // MLA_TK):
            v_ref[0, h, c] = vt[:, c * MLA_TK:(c + 1) * MLA_TK].astype(v_ref.dtype)


def _mla_ones_row(head):
    return MLA_V if head % 2 == 0 else 0


def _rot_half_matrix(n):
    half = n // 2
    r = np.zeros((n, n), np.float32)
    for j in range(half):
        r[half + j, j] = -1.0
        r[j, half + j] = 1.0
    return r


def mla_weights(w_uq, w_ukv):
    rot = jnp.asarray(_rot_half_matrix(MLA_ROPE))
    wq = jnp.transpose(w_uq, (1, 0, 2))
    pad = lambda a, lo, hi: jnp.pad(a, ((0, 0), (0, 0), (lo, hi)))
    wqa = pad(wq, 0, MLA_DP - MLA_NOPE - MLA_ROPE)
    wqb = pad(jnp.einsum('hrd,de->hre', wq[..., MLA_NOPE:], rot), MLA_NOPE, MLA_DP - MLA_NOPE - MLA_ROPE)
    wkv = jnp.transpose(w_ukv, (1, 0, 2))
    wk = pad(wkv[..., :MLA_NOPE], 0, MLA_DP - MLA_NOPE)
    wv_even = pad(wkv[..., MLA_NOPE:], 0, MLA_V)
    wv_odd = pad(wkv[..., MLA_NOPE:], MLA_V, 0)
    wv = jnp.where((jnp.arange(MLA_HEADS) % 2 == 0)[:, None, None], wv_even, wv_odd)
    wv = jnp.transpose(wv, (0, 2, 1))
    ea = np.zeros((LANES, MLA_DP), np.float32)
    for j in range(MLA_ROPE):
        ea[j, MLA_NOPE + j] = 1.0
    eb = np.zeros((LANES, MLA_DP), np.float32)
    eb[:MLA_ROPE, MLA_NOPE:MLA_NOPE + MLA_ROPE] = _rot_half_matrix(MLA_ROPE)
    bf = jnp.bfloat16
    return (wqa.astype(bf), wqb.astype(bf), wk.astype(bf), wv.astype(bf),
            jnp.asarray(ea, bf), jnp.asarray(eb, bf))


def mla_rope_tables(seq):
    half = MLA_ROPE // 2
    inv = ROPE_BASE ** (-jnp.arange(half, dtype=jnp.float32) / half)
    ang = jnp.arange(seq, dtype=jnp.float32)[:, None] * inv[None, :]
    ones = jnp.ones((seq, MLA_NOPE), jnp.float32)
    zeros = jnp.zeros((seq, MLA_DP - MLA_NOPE - MLA_ROPE), jnp.float32)
    cos = jnp.concatenate([ones, jnp.cos(ang), jnp.cos(ang), zeros], axis=1)
    sin = jnp.concatenate([0.0 * ones, jnp.sin(ang), jnp.sin(ang), zeros], axis=1)
    return cos, sin


def mla_project(z, q_norm, kv_norm, weights, tables, batch, seq):
    wqa, wqb, wk, wv, ea, eb = weights
    cos, sin = tables
    n_t = seq // MLA_TM
    hshape = (batch, MLA_HEADS, seq, MLA_DP)
    c3 = lambda b, t: (0, 0, 0)
    c2 = lambda b, t: (0, 0)
    hspec = pl.BlockSpec((1, MLA_HEADS, MLA_TM, MLA_DP), lambda b, t: (b, 0, t, 0))
    return pl.pallas_call(
        _mla_proj_kernel,
        grid=(batch, n_t),
        in_specs=[
            pl.BlockSpec((MLA_TM, 4 * LANES), lambda b, t: (b * n_t + t, CB_MLA // 4)),
            pl.BlockSpec((1, MLA_Q_RANK), c2),
            pl.BlockSpec((1, MLA_KV_RANK), c2),
            pl.BlockSpec((MLA_HEADS, MLA_Q_RANK, MLA_DP), c3),
            pl.BlockSpec((MLA_HEADS, MLA_Q_RANK, MLA_DP), c3),
            pl.BlockSpec((MLA_HEADS, MLA_KV_RANK, MLA_DP), c3),
            pl.BlockSpec((MLA_HEADS, MLA_KV_RANK, MLA_DP), c3),
            pl.BlockSpec((LANES, MLA_DP), c2),
            pl.BlockSpec((LANES, MLA_DP), c2),
            pl.BlockSpec((MLA_TM, MLA_DP), lambda b, t: (t, 0)),
            pl.BlockSpec((MLA_TM, MLA_DP), lambda b, t: (t, 0)),
        ],
        out_specs=[hspec, hspec,
                   pl.BlockSpec((1, MLA_HEADS, MLA_TM // MLA_TK, MLA_DP, MLA_TK), lambda b, t: (b, 0, t, 0, 0))],
        out_shape=[jax.ShapeDtypeStruct(hshape, jnp.bfloat16)] * 2
        + [jax.ShapeDtypeStruct((batch, MLA_HEADS, seq // MLA_TK, MLA_DP, MLA_TK), jnp.bfloat16)],
        compiler_params=_cparams("arbitrary", "arbitrary"),
        name="mla_project",
    )(z, q_norm.reshape(1, -1), kv_norm.reshape(1, -1), wqa, wqb, wk, wv, ea, eb, cos, sin)


def _mla_attn_kernel(q_ref, k_ref, vt_ref, o_ref, acc_ref, *, n_kv):
    acc_ref[...] = jnp.zeros_like(acc_ref)
    f32 = jnp.float32
    nt = (((1,), (1,)), ((), ()))

    def body(j, carry):
        rows = pl.ds(pl.multiple_of(j * MLA_TK, MLA_TK), MLA_TK)
        new = []
        for hh in range(2):
            m_prev = carry[hh]
            s = lax.dot_general(k_ref[0, hh, rows, :], q_ref[0, hh], nt, preferred_element_type=f32)
            m_next = jnp.maximum(m_prev, jnp.max(s, axis=0, keepdims=True))
            p = jnp.exp2(s - m_next)
            alpha = jnp.exp2(m_prev - m_next)
            new.append(m_next)
            acc_ref[hh] = alpha * acc_ref[hh] + jnp.dot(vt_ref[0, hh, j], p.astype(jnp.bfloat16),
                                                         preferred_element_type=f32)
        return tuple(new)

    m0 = jnp.full((1, MLA_TQ), NEG_INF, f32)
    lax.fori_loop(0, n_kv, body, (m0, m0), unroll=min(4, n_kv))
    acc_a, acc_b = acc_ref[0], acc_ref[1]
    l_a = acc_a[_mla_ones_row(0):_mla_ones_row(0) + 1, :]
    l_b = acc_b[_mla_ones_row(1):_mla_ones_row(1) + 1, :]
    row = lax.broadcasted_iota(jnp.int32, acc_a.shape, 0)
    out_t = jnp.where(row < MLA_V, acc_a / l_a, acc_b / l_b)
    o_ref[...] = out_t.T.astype(o_ref.dtype)


def mla_attention(q, k, vt, batch, seq):
    n_q = seq // MLA_TQ
    n_kv = seq // MLA_TK
    qspec = pl.BlockSpec((1, 2, MLA_TQ, MLA_DP), lambda b, hp, i: (b, hp, i, 0))
    kspec = pl.BlockSpec((1, 2, seq, MLA_DP), lambda b, hp, i: (b, hp, 0, 0))
    vspec = pl.BlockSpec((1, 2, n_kv, MLA_DP, MLA_TK), lambda b, hp, i: (b, hp, 0, 0, 0))
    return pl.pallas_call(
        functools.partial(_mla_attn_kernel, n_kv=n_kv),
        grid=(batch, MLA_HEADS // 2, n_q),
        in_specs=[qspec, kspec, vspec],
        out_specs=pl.BlockSpec((MLA_TQ, LANES), lambda b, hp, i: (b * n_q + i, hp)),
        out_shape=jax.ShapeDtypeStruct((batch * seq, MLA_HEADS * MLA_V), jnp.bfloat16),
        scratch_shapes=[pltpu.VMEM((2, MLA_DP, MLA_TQ), jnp.float32)],
        compiler_params=_cparams("arbitrary", "arbitrary", "arbitrary"),
        name="mla_attention",
    )(q, k, vt)


BAND_Q = 128
BAND_W = 256
BAND_RADIUS = 64
BAND_ROWS_PER_STEP = 2048


def band_bias_table(group, dil):
    n = DIL_HEADS
    slopes = np.asarray([2.0 ** (-8.0 * (h + 1) / n) for h in range(n)], np.float32)
    slopes = slopes[group * DIL_HEADS_PER_GROUP:(group + 1) * DIL_HEADS_PER_GROUP]
    iq = np.arange(BAND_Q)[:, None]
    ik = np.arange(BAND_W)[None, :]
    tabs = []
    for d in range(3):
        dist = np.abs(d * BAND_RADIUS + iq - ik)
        bias = -slopes[:, None, None] * (dil * dist).astype(np.float32)[None]
        tabs.append(np.where((dist <= BAND_RADIUS)[None], bias, np.float32(NEG_INF)))
    return jnp.asarray(np.stack(tabs).astype(np.float32))


def _band_kernel(q_ref, k_ref, v_ref, bias_ref, o_ref, lse_ref, *, length):
    lane = lax.broadcasted_iota(jnp.int32, (1, LANES), 1)
    first = lane < DIL_HEAD_DIM
    scale = DIL_HEAD_DIM ** -0.5

    n_blocks = length // BAND_Q
    n_res = q_ref.shape[1]

    def body(idx, carry):
        rr = idx // n_blocks
        q0 = pl.multiple_of((idx % n_blocks) * BAND_Q, BAND_Q)
        start = pl.multiple_of(jnp.clip(q0 - BAND_RADIUS, 0, length - BAND_W), BAND_RADIUS)
        didx = (q0 - start) // BAND_RADIUS
        q = q_ref[0, rr, pl.ds(q0, BAND_Q), :]
        kw = k_ref[0, rr, pl.ds(start, BAND_W), :]
        vw = v_ref[0, rr, pl.ds(start, BAND_W), :]
        outs, lses = [], []
        for hh in range(2):
            sel = first if hh == 0 else jnp.logical_not(first)
            qh = jnp.where(sel, q, jnp.zeros_like(q))
            s = lax.dot_general(qh, kw, (((1,), (1,)), ((), ())), preferred_element_type=jnp.float32)
            s = s * scale + bias_ref[didx, hh]
            m = jnp.max(s, axis=1, keepdims=True)
            e = jnp.exp(s - m)
            den = jnp.sum(e, axis=1, keepdims=True)
            p = (e / den).astype(jnp.bfloat16)
            outs.append(jnp.dot(p, vw, preferred_element_type=jnp.float32))
            lses.append(m + jnp.log(den))
        o_ref[0, rr, pl.ds(q0, BAND_Q), :] = jnp.where(first, outs[0], outs[1]).astype(o_ref.dtype)
        lse_ref[0, rr, pl.ds(q0, BAND_Q), :] = jnp.where(first, lses[0], lses[1])
        return carry

    lax.fori_loop(0, n_res * n_blocks, body, 0, unroll=min(4, n_res * n_blocks))


def band_group(src, group, dil, batch, seq, col_blocks):
    length = seq // dil
    pairs = DIL_HEADS_PER_GROUP // 2
    n_res = max(1, min(dil, BAND_ROWS_PER_STEP // length))

    def zspec(cb):
        return pl.BlockSpec((1, n_res, length, LANES), lambda b, r, hp: (b, r, 0, cb + hp))

    ospec = pl.BlockSpec((1, n_res, length, LANES), lambda b, r, hp: (b, r, 0, hp))
    oshape = (batch, dil, length, DIL_GROUP_WIDTH)
    return pl.pallas_call(
        functools.partial(_band_kernel, length=length),
        grid=(batch, dil // n_res, pairs),
        in_specs=[zspec(col_blocks[0]), zspec(col_blocks[1]), zspec(col_blocks[2]),
                  pl.BlockSpec((3, 2, BAND_Q, BAND_W), lambda b, r, hp: (0, hp, 0, 0))],
        out_specs=[ospec, ospec],
        out_shape=[jax.ShapeDtypeStruct(oshape, jnp.bfloat16), jax.ShapeDtypeStruct(oshape, jnp.float32)],
        compiler_params=_cparams("arbitrary", "arbitrary", "arbitrary"),
        name=f"band_attention_g{group}",
    )(src, src, src, band_bias_table(group, dil))


RET_C = 256


def _ret_kernel(lg_ref, q_ref, k_ref, v_ref, g_ref, cos_ref, sin_ref, o_ref, qs_ref, ks_ref, o1_ref, *, n_c):
    h = pl.program_id(1)
    lgf = lg_ref[0, h]
    lgb = lg_ref[1, h]
    c = RET_C
    f32 = jnp.float32
    bf = jnp.bfloat16
    ii = lax.broadcasted_iota(jnp.int32, (c, c), 0)
    jj = lax.broadcasted_iota(jnp.int32, (c, c), 1)
    diff = (ii - jj).astype(f32)
    decay = jnp.where(diff >= 0.0, jnp.exp(lgf * jnp.maximum(diff, 0.0)), jnp.exp(lgb * jnp.maximum(-diff, 0.0)))
    idx = lax.broadcasted_iota(jnp.int32, (c, 1), 0).astype(f32)
    xi_f = jnp.exp(lgf * (idx + 1.0))
    zeta_f = jnp.exp(lgf * (c - 1.0 - idx))
    xi_b = jnp.exp(lgb * (c - idx))
    zeta_b = jnp.exp(lgb * idx)
    cd_f = jnp.exp(lgf * c)
    cd_b = jnp.exp(lgb * c)
    kscale = RET_QK ** -0.5

    def rope(x, rows):
        return x * cos_ref[rows, :] + pltpu.roll(x, RET_QK // 2, 1) * sin_ref[rows, :]

    def fwd(n, state):
        rows = pl.ds(pl.multiple_of(n * c, c), c)
        q = rope(q_ref[0, rows, :].astype(f32), rows)
        k = rope(k_ref[0, rows, :].astype(f32), rows) * kscale
        v = v_ref[0, rows, :]
        qs_ref[rows, :] = q
        ks_ref[rows, :] = k
        qb = q.astype(bf)
        s = lax.dot_general(qb, k.astype(bf), (((1,), (1,)), ((), ())), preferred_element_type=f32) * decay
        o = jnp.dot(s.astype(bf), v, preferred_element_type=f32)
        o = o + xi_f * jnp.dot(qb, state.astype(bf), preferred_element_type=f32)
        o1_ref[rows, :] = o
        kz = (k * zeta_f).T.astype(bf)
        return cd_f * state + jnp.dot(kz, v, preferred_element_type=f32)

    lax.fori_loop(0, n_c, fwd, jnp.zeros((RET_QK, RET_V), f32))

    def bwd(step, state):
        n = n_c - 1 - step
        rows = pl.ds(pl.multiple_of(n * c, c), c)
        q = qs_ref[rows, :]
        k = ks_ref[rows, :]
        v = v_ref[0, rows, :]
        of = o1_ref[rows, :] + xi_b * jnp.dot(q.astype(bf), state.astype(bf), preferred_element_type=f32)
        mu = jnp.mean(of, axis=-1, keepdims=True)
        var = jnp.mean(jnp.square(of - mu), axis=-1, keepdims=True)
        of = (of - mu) * lax.rsqrt(var + EPS)
        g = g_ref[0, rows, :].astype(f32)
        o_ref[0, rows, :] = (g * jax.nn.sigmoid(g) * of).astype(o_ref.dtype)
        kz = (k * zeta_b).T.astype(bf)
        return cd_b * state + jnp.dot(kz, v, preferred_element_type=f32)

    lax.fori_loop(0, n_c, bwd, jnp.zeros((RET_QK, RET_V), f32))


def ret_rope_tables(seq):
    half = RET_QK // 2
    inv = ROPE_BASE ** (-jnp.arange(half, dtype=jnp.float32) / half)
    ang = jnp.arange(seq, dtype=jnp.float32)[:, None] * inv[None, :]
    cos = jnp.concatenate([jnp.cos(ang), jnp.cos(ang)], axis=1)
    sin = jnp.concatenate([-jnp.sin(ang), jnp.sin(ang)], axis=1)
    return cos, sin


def retention(z, ret_decay, tables, batch, seq):
    zv = z.reshape(batch, seq, Z_COLS)
    log_gamma = jax.nn.log_sigmoid(ret_decay.astype(jnp.float32))
    cos, sin = tables

    def zspec(cb):
        return pl.BlockSpec((1, seq, LANES), lambda b, h: (b, 0, cb + h))

    tspec = pl.BlockSpec((seq, RET_QK), lambda b, h: (0, 0))
    out = pl.pallas_call(
        functools.partial(_ret_kernel, n_c=seq // RET_C),
        grid=(batch, RET_HEADS),
        in_specs=[pl.BlockSpec(memory_space=pltpu.SMEM),
                  zspec(CB_RQ), zspec(CB_RK), zspec(CB_RV), zspec(CB_RG), tspec, tspec],
        out_specs=pl.BlockSpec((1, seq, LANES), lambda b, h: (b, 0, h)),
        out_shape=jax.ShapeDtypeStruct((batch, seq, RET_HEADS * RET_V), jnp.bfloat16),
        scratch_shapes=[pltpu.VMEM((seq, RET_QK), jnp.float32)] * 3,
        compiler_params=_cparams("arbitrary", "arbitrary"),
        name="retention",
    )(log_gamma, zv, zv, zv, zv, cos, sin)
    return out.reshape(batch * seq, RET_HEADS * RET_V)


MERGE_TM = 512


def _merge_kernel(x_ref, ya_ref, yb_ref, o0_ref, o1_ref, o2_ref, l0_ref, l1_ref, l2_ref, yd_ref,
                  zg_ref, wb_ref, wo_ref, out_ref, tok_ref):
    f32 = jnp.float32

    def token_order(slot, src_ref):
        dil, rows, width = src_ref.shape[1:]
        tiles = width // LANES
        for r in range(dil):
            v = src_ref[0, r].astype(f32)
            for c in range(tiles):
                tok_ref[slot * tiles + c, pl.ds(r, rows, stride=dil), :] = v[:, c * LANES:(c + 1) * LANES]
        return jnp.concatenate([tok_ref[slot * tiles + c] for c in range(tiles)], axis=1)

    l0 = l0_ref[0, 0]
    l1, l2 = token_order(0, l1_ref), token_order(1, l2_ref)
    o1, o2 = token_order(2, o1_ref), token_order(3, o2_ref)
    m = jnp.maximum(jnp.maximum(l0, l1), l2)
    e0, e1, e2 = jnp.exp(l0 - m), jnp.exp(l1 - m), jnp.exp(l2 - m)
    inv = 1.0 / (e0 + e1 + e2)
    yc = ((e0 * inv) * o0_ref[0, 0].astype(f32) + (e1 * inv) * o1 + (e2 * inv) * o2).astype(jnp.bfloat16)
    merged = None
    for i, y in enumerate((ya_ref[...], yb_ref[...], yc, yd_ref[...])):
        gate = jax.nn.sigmoid(zg_ref[:, i * D_MODEL:(i + 1) * D_MODEL].astype(f32))
        term = gate * jnp.dot(y, wb_ref[i], preferred_element_type=f32)
        merged = term if merged is None else merged + term
    out_ref[...] = x_ref[...] + jnp.dot(merged.astype(jnp.bfloat16), wo_ref[...], preferred_element_type=f32)


def merge_project(x, ya, yb, dil_o, dil_lse, yd, z, w_branch, w_out, seq):
    n = x.shape[0]
    tm = MERGE_TM
    n_t = seq // tm
    width = 4 * LANES
    row = lambda i: (i, 0)
    bspec = pl.BlockSpec((tm, width), row)

    def dspec(dil):
        return pl.BlockSpec((1, dil, tm // dil, width), lambda i: (i // n_t, 0, i % n_t, 0))

    dspecs = [dspec(d) for _, d in DIL_PAIRS]
    return pl.pallas_call(
        _merge_kernel,
        grid=(n // tm,),
        in_specs=[pl.BlockSpec((tm, D_MODEL), row), bspec, bspec, *dspecs, *dspecs, bspec,
                  pl.BlockSpec((tm, N_BRANCHES * D_MODEL), lambda i: (i, CB_ZG * LANES // (N_BRANCHES * D_MODEL))),
                  pl.BlockSpec((N_BRANCHES, width, D_MODEL), lambda i: (0, 0, 0)),
                  pl.BlockSpec((D_MODEL, D_MODEL), lambda i: (0, 0))],
        out_specs=pl.BlockSpec((tm, D_MODEL), row),
        out_shape=jax.ShapeDtypeStruct((n, D_MODEL), jnp.float32),
        scratch_shapes=[pltpu.VMEM((4 * width // LANES, tm, LANES), jnp.float32)],
        compiler_params=_cparams("arbitrary"),
        name="merge_project",
    )(x, ya, yb, *dil_o, *dil_lse, yd, z, w_branch, w_out)


MOE_TM = 512
MOE_TT = 256
MOE_W_SMALL = 64
MOE_ROWS = 1024
MOE_ALIGN = 8
MOE_XW = D_MODEL + LANES


def _router_kernel(x_ref, g_ref, w_ref, h_ref, aff_ref):
    x = x_ref[...]
    y = x * lax.rsqrt(jnp.mean(x * x, axis=-1, keepdims=True) + EPS)
    h = (y * g_ref[...]).astype(jnp.bfloat16)
    h_ref[...] = h
    logits = jnp.dot(h, w_ref[...], preferred_element_type=jnp.float32)
    lane = lax.broadcasted_iota(jnp.int32, logits.shape, 1)
    logits = jnp.where(lane < N_EXPERTS, logits, NEG_INF)
    e = jnp.exp(logits - jnp.max(logits, axis=-1, keepdims=True))
    aff_t = (e / jnp.sum(e, axis=-1, keepdims=True)).T
    for c in range(x.shape[0] // MOE_TT):
        aff_ref[c] = aff_t[:N_EXPERTS, c * MOE_TT:(c + 1) * MOE_TT]


def moe_router(x, norm_g, w_router):
    n, d = x.shape
    tm = 2 * MOE_TT
    w = jnp.pad(w_router.astype(jnp.bfloat16), ((0, 0), (0, LANES - N_EXPERTS)))
    return pl.pallas_call(
        _router_kernel,
        grid=(n // tm,),
        in_specs=[pl.BlockSpec((tm, d), lambda i: (i, 0)), pl.BlockSpec((1, d), lambda i: (0, 0)),
                  pl.BlockSpec((d, LANES), lambda i: (0, 0))],
        out_specs=[pl.BlockSpec((tm, d), lambda i: (i, 0)),
                   pl.BlockSpec((tm // MOE_TT, N_EXPERTS, MOE_TT), lambda i: (i, 0, 0))],
        out_shape=[jax.ShapeDtypeStruct((n, d), jnp.bfloat16),
                   jax.ShapeDtypeStruct((n // MOE_TT, N_EXPERTS, MOE_TT), jnp.float32)],
        compiler_params=_cparams("arbitrary"),
        name="moe_router",
    )(x, norm_g.reshape(1, d), w)


def _affinity_bits(a):
    return lax.bitcast_convert_type(a, jnp.int32)


def _threshold_kernel(aff_ref, thr_ref, need_ref, *, cap, n_tiles):
    def count(pred, thr):
        def tile(c, acc):
            return acc + jnp.where(pred(_affinity_bits(aff_ref[c]), thr), 1.0, 0.0)
        acc = lax.fori_loop(0, n_tiles, tile, jnp.zeros((N_EXPERTS, MOE_TT), jnp.float32), unroll=8)
        return jnp.sum(acc, axis=1, keepdims=True)

    def bit(i, thr):
        cand = thr | jnp.left_shift(jnp.int32(1), 30 - i)
        return jnp.where(count(lambda b, t: b >= t, cand) >= cap, cand, thr)

    thr = lax.fori_loop(0, 31, bit, jnp.zeros((N_EXPERTS, 1), jnp.int32))
    thr_ref[...] = jnp.broadcast_to(thr, thr_ref.shape)
    need_ref[...] = jnp.broadcast_to(cap - count(lambda b, t: b > t, thr), need_ref.shape)


def moe_threshold(aff, cap):
    n_tiles = aff.shape[0]
    full = pl.BlockSpec((N_EXPERTS, LANES), lambda i: (0, 0))
    return pl.pallas_call(
        functools.partial(_threshold_kernel, cap=cap, n_tiles=n_tiles),
        grid=(1,),
        in_specs=[pl.BlockSpec(aff.shape, lambda i: (0, 0, 0))],
        out_specs=[full, full],
        out_shape=[jax.ShapeDtypeStruct((N_EXPERTS, LANES), jnp.int32),
                   jax.ShapeDtypeStruct((N_EXPERTS, LANES), jnp.float32)],
        compiler_params=_cparams("arbitrary"),
        name="moe_threshold",
    )(aff)


def _assign_kernel(aff_ref, thr_ref, need_ref, codet_ref, coden_ref, start_ref, total_ref, ties_ref, run_ref):
    @pl.when(pl.program_id(0) == 0)
    def _():
        ties_ref[...] = jnp.zeros_like(ties_ref)
        run_ref[...] = jnp.zeros_like(run_ref)

    f32 = jnp.float32
    bits = _affinity_bits(aff_ref[0])
    thr = thr_ref[:, :1]
    ii = lax.broadcasted_iota(jnp.int32, (MOE_TT, MOE_TT), 0)
    jj = lax.broadcasted_iota(jnp.int32, (MOE_TT, MOE_TT), 1)
    tri = jnp.where(ii <= jj, 1.0, 0.0).astype(jnp.bfloat16)
    eq = bits == thr
    eq_f = jnp.where(eq, 1.0, 0.0)
    tie_rank = jnp.dot(eq_f.astype(jnp.bfloat16), tri, preferred_element_type=f32) + ties_ref[:, :1]
    sel = jnp.logical_or(bits > thr, jnp.logical_and(eq, tie_rank <= need_ref[:, :1]))
    sel_f = jnp.where(sel, 1.0, 0.0)
    incl = jnp.dot(sel_f.astype(jnp.bfloat16), tri, preferred_element_type=f32)
    code = jnp.where(sel, incl - 1.0, -1.0)
    codet_ref[0] = code.astype(jnp.int32)
    padded = jnp.concatenate([code, jnp.zeros((LANES - N_EXPERTS, MOE_TT), f32)], axis=0)
    coden_ref[...] = padded.T.astype(jnp.int32)
    start_ref[0] = run_ref[...].astype(jnp.int32)
    taken = jnp.sum(sel_f, axis=1, keepdims=True)
    run_ref[...] = run_ref[...] + jnp.floor((taken + (MOE_ALIGN - 1)) * (1.0 / MOE_ALIGN)) * MOE_ALIGN
    total_ref[...] = run_ref[...].astype(jnp.int32)
    ties_ref[...] = ties_ref[...] + jnp.sum(eq_f, axis=1, keepdims=True)


def moe_assign(aff, thr, need):
    n_tiles = aff.shape[0]
    tile = pl.BlockSpec((1, N_EXPERTS, MOE_TT), lambda c: (c, 0, 0))
    full = pl.BlockSpec((N_EXPERTS, LANES), lambda c: (0, 0))
    return pl.pallas_call(
        _assign_kernel,
        grid=(n_tiles,),
        in_specs=[tile, full, full],
        out_specs=[tile, pl.BlockSpec((MOE_TT, LANES), lambda c: (c, 0)),
                   pl.BlockSpec((1, N_EXPERTS, LANES), lambda c: (c, 0, 0)), full],
        out_shape=[jax.ShapeDtypeStruct((n_tiles, N_EXPERTS, MOE_TT), jnp.int32),
                   jax.ShapeDtypeStruct((n_tiles * MOE_TT, LANES), jnp.int32),
                   jax.ShapeDtypeStruct((n_tiles, N_EXPERTS, LANES), jnp.int32),
                   jax.ShapeDtypeStruct((N_EXPERTS, LANES), jnp.int32)],
        scratch_shapes=[pltpu.VMEM((N_EXPERTS, LANES), jnp.float32)] * 2,
        compiler_params=_cparams("arbitrary"),
        name="moe_assign",
    )(aff, thr, need)


def _dispatch_kernel(start_ref, big_ref, h_ref, code_ref, aff_ref, init_hbm, x_hbm, buf_ref, sem):
    del init_hbm
    c = pl.program_id(0)
    f32 = jnp.float32
    h = h_ref[...]
    code = code_ref[0]
    gate = aff_ref[0]

    def run(width):
        group = MOE_ROWS // width
        slot = lax.broadcasted_iota(jnp.int32, (width, MOE_TT), 0)
        for first in range(0, N_EXPERTS, group):
            onehots = []
            for k in range(group):
                e = first + k
                hit = slot == code[e:e + 1, :]
                onehots.append(jnp.where(hit, 1.0, 0.0).astype(jnp.bfloat16))
                gwin = jnp.sum(jnp.where(hit, gate[e:e + 1, :], 0.0), axis=1, keepdims=True)
                buf_ref[k * width:(k + 1) * width, D_MODEL:] = jnp.broadcast_to(gwin, (width, LANES))
            buf_ref[:, :D_MODEL] = jnp.dot(jnp.concatenate(onehots, axis=0), h, preferred_element_type=f32)
            copies = [pltpu.make_async_copy(
                buf_ref.at[pl.ds(k * width, width)],
                x_hbm.at[first + k, pl.ds(_slot_start(start_ref, c, first + k), width)],
                sem.at[k]) for k in range(group)]
            for cp in copies:
                cp.start()
            for cp in copies:
                cp.wait()

    @pl.when(big_ref[c] == 0)
    def _():
        run(MOE_W_SMALL)

    @pl.when(big_ref[c] != 0)
    def _():
        run(MOE_TT)


def _slot_start(start_ref, tile, expert):
    return pl.multiple_of(start_ref[tile * N_EXPERTS + expert], MOE_ALIGN)


def moe_dispatch(starts, big, h, code_t, aff, slots):
    n, d = h.shape
    n_tiles = n // MOE_TT
    tile = pl.BlockSpec((1, N_EXPERTS, MOE_TT), lambda c, s, b: (c, 0, 0))
    shape = (N_EXPERTS, slots, MOE_XW)
    return pl.pallas_call(
        _dispatch_kernel,
        grid_spec=pltpu.PrefetchScalarGridSpec(
            num_scalar_prefetch=2,
            grid=(n_tiles,),
            in_specs=[pl.BlockSpec((MOE_TT, d), lambda c, s, b: (c, 0)), tile, tile,
                      pl.BlockSpec(memory_space=pl.ANY)],
            out_specs=pl.BlockSpec(memory_space=pl.ANY),
            scratch_shapes=[pltpu.VMEM((MOE_ROWS, MOE_XW), jnp.float32),
                            pltpu.SemaphoreType.DMA((MOE_ROWS // MOE_W_SMALL,))],
        ),
        out_shape=jax.ShapeDtypeStruct(shape, jnp.float32),
        input_output_aliases={5: 0},
        compiler_params=_cparams("arbitrary"),
        name="moe_dispatch",
    )(starts, big, h, code_t, aff, jnp.zeros(shape, jnp.float32))


def _expert_kernel(total_ref, x_ref, wg_ref, wu_ref, wd_ref, o_ref):
    f32 = jnp.float32
    used = pl.program_id(1) * MOE_TM < total_ref[pl.program_id(0)]

    @pl.when(used)
    def _():
        x = x_ref[0, :, :D_MODEL].astype(jnp.bfloat16)
        a = jnp.dot(x, wg_ref[0], preferred_element_type=f32)
        u = jnp.dot(x, wu_ref[0], preferred_element_type=f32)
        he = (a * jax.nn.sigmoid(a) * u).astype(jnp.bfloat16)
        o_ref[0] = jnp.dot(he, wd_ref[0], preferred_element_type=f32) * x_ref[0, :, D_MODEL:D_MODEL + 1]

    @pl.when(jnp.logical_not(used))
    def _():
        o_ref[0] = jnp.zeros_like(o_ref[0])


def expert_ffn(total, xd, w_gate, w_up, w_down):
    e, slots, _ = xd.shape
    d = D_MODEL
    tm = MOE_TM
    wspec = pl.BlockSpec((1, d, D_EXPERT), lambda i, j, t: (i, 0, 0))
    return pl.pallas_call(
        _expert_kernel,
        grid_spec=pltpu.PrefetchScalarGridSpec(
            num_scalar_prefetch=1,
            grid=(e, slots // tm),
            in_specs=[pl.BlockSpec((1, tm, MOE_XW), lambda i, j, t: (i, j, 0)),
                      wspec, wspec,
                      pl.BlockSpec((1, D_EXPERT, d), lambda i, j, t: (i, 0, 0))],
            out_specs=pl.BlockSpec((1, tm, d), lambda i, j, t: (i, j, 0)),
        ),
        out_shape=jax.ShapeDtypeStruct((e, slots, d), jnp.float32),
        compiler_params=_cparams("arbitrary", "arbitrary"),
        name="expert_ffn",
    )(total, xd, w_gate, w_up, w_down)


def _combine_kernel(start_ref, big_ref, x_ref, code_ref, y_hbm, o_ref, buf_ref, sem):
    c = pl.program_id(0)
    f32 = jnp.float32
    bf = jnp.bfloat16
    code = code_ref[...]
    o_ref[...] = x_ref[...]

    def run(width):
        group = MOE_ROWS // width
        for first in range(0, N_EXPERTS, group):
            copies = [pltpu.make_async_copy(
                y_hbm.at[first + k, pl.ds(_slot_start(start_ref, c, first + k), width)],
                buf_ref.at[pl.ds(k * width, width)],
                sem.at[k]) for k in range(group)]
            for cp in copies:
                cp.start()
            pieces = []
            if width < LANES:
                lane = lax.broadcasted_iota(jnp.int32, (MOE_TT, LANES), 1)
                for k in range(0, group, 2):
                    e = first + k
                    target = jnp.where(lane < width, code[:, e:e + 1], code[:, e + 1:e + 2] + width)
                    pieces.append(jnp.where(target == lane, 1.0, 0.0).astype(bf))
            else:
                lane = lax.broadcasted_iota(jnp.int32, (MOE_TT, width), 1)
                for k in range(group):
                    e = first + k
                    pieces.append(jnp.where(code[:, e:e + 1] == lane, 1.0, 0.0).astype(bf))
            onehot = jnp.concatenate(pieces, axis=1)
            for cp in copies:
                cp.wait()
            y = buf_ref[...]
            y_hi = y.astype(bf)
            y_lo = (y - y_hi.astype(f32)).astype(bf)
            o_ref[...] += (jnp.dot(onehot, y_hi, preferred_element_type=f32)
                           + jnp.dot(onehot, y_lo, preferred_element_type=f32))

    @pl.when(big_ref[c] == 0)
    def _():
        run(MOE_W_SMALL)

    @pl.when(big_ref[c] != 0)
    def _():
        run(MOE_TT)


def moe_combine(starts, big, x, code_n, y):
    n, d = x.shape
    row = pl.BlockSpec((MOE_TT, d), lambda c, s, b: (c, 0))
    return pl.pallas_call(
        _combine_kernel,
        grid_spec=pltpu.PrefetchScalarGridSpec(
            num_scalar_prefetch=2,
            grid=(n // MOE_TT,),
            in_specs=[row, pl.BlockSpec((MOE_TT, LANES), lambda c, s, b: (c, 0)),
                      pl.BlockSpec(memory_space=pl.ANY)],
            out_specs=row,
            scratch_shapes=[pltpu.VMEM((MOE_ROWS, d), jnp.float32),
                            pltpu.SemaphoreType.DMA((MOE_ROWS // MOE_W_SMALL,))],
        ),
        out_shape=jax.ShapeDtypeStruct((n, d), jnp.float32),
        compiler_params=_cparams("arbitrary"),
        name="moe_combine",
    )(starts, big, x, code_n, y)


def ec_moe(x, norm_g, w_router, w_gate, w_up, w_down):
    n_tok, _ = x.shape
    cap = EC_FACTOR * n_tok // N_EXPERTS
    h, aff = moe_router(x, norm_g, w_router)
    thr, need = moe_threshold(aff, cap)
    code_t, code_n, starts, total = moe_assign(aff, thr, need)
    starts = starts[:, :, 0]
    total = total[:, 0]
    ends = jnp.concatenate([starts[1:], total[None]], axis=0)
    big = (jnp.max(ends - starts, axis=1) > MOE_W_SMALL).astype(jnp.int32)
    starts = starts.reshape(-1)
    n_tiles = n_tok // MOE_TT
    slots = -(-(cap + MOE_ALIGN * n_tiles + MOE_TT) // MOE_TM) * MOE_TM
    xd = moe_dispatch(starts, big, h, code_t, aff, slots)
    y = expert_ffn(total, xd, w_gate, w_up, w_down)
    return moe_combine(starts, big, x, code_n, y)


def split_in_proj(w_in):
    w = w_in.astype(jnp.bfloat16)
    sizes = (LRU_WIDTH, LRU_WIDTH, MLA_Q_RANK, MLA_KV_RANK, MLA_ROPE,
             DIL_HEADS * DIL_HEAD_DIM, DIL_HEADS * DIL_HEAD_DIM, DIL_HEADS * DIL_HEAD_DIM,
             RET_HEADS * RET_QK, RET_HEADS * RET_QK, RET_HEADS * RET_V, RET_HEADS * RET_V,
             N_BRANCHES * D_MODEL)
    parts, off = [], 0
    for s in sizes:
        parts.append(w[:, off:off + s])
        off += s
    xa, ga, cq, ckv, kr, dq, dk, dv, rq, rk, rv, rg, zg = parts
    gw = DIL_GROUP_WIDTH
    grp = lambda a, g: a[:, g * gw:(g + 1) * gw]
    zeros = jnp.zeros((w.shape[0], Z_PAD), w.dtype)
    main = jnp.concatenate([zg, xa, ga, cq, ckv, kr, zeros, grp(dq, 0), grp(dk, 0), grp(dv, 0),
                            rq, rk, rv, rg], axis=1)
    dil = [jnp.concatenate([grp(dq, g), grp(dk, g), grp(dv, g)], axis=1) for g in range(1, len(DIL_PAIRS))]
    return main, dil


def mixer(x, batch, seq, norm_g, w_in, conv_w, conv_b, lru_gate_w, lru_gate_b, lru_lambda, mla_q_norm,
          mla_kv_norm, w_uq, w_ukv, ret_decay, w_branch, w_out, mla_tables, ret_tables):
    n = batch * seq
    w_main, w_dil = split_in_proj(w_in)
    z = norm_matmul(x, norm_g, w_main, jnp.bfloat16, min(1024, n), 1024)
    wg, gb = lru_gate_dense(lru_gate_w, lru_gate_b)
    h_fwd = lru_direction(z, conv_w, conv_b, wg, gb, lru_lambda, batch, seq, False)
    ya = lru_direction(z, conv_w, conv_b, wg, gb, lru_lambda, batch, seq, True, h_fwd)
    q, k, v = mla_project(z, mla_q_norm, mla_kv_norm, mla_weights(w_uq, w_ukv), mla_tables, batch, seq)
    yb = mla_attention(q, k, v, batch, seq)
    dil = [band_group(z.reshape(batch, 1, seq, Z_COLS), 0, 1, batch, seq, (CB_DQ, CB_DK, CB_DV))]
    pairs = DIL_HEADS_PER_GROUP // 2
    for g in range(1, len(DIL_PAIRS)):
        d = DIL_PAIRS[g][1]
        zd = norm_matmul_dil(x, norm_g, w_dil[g - 1], d, batch, seq, min(1024, seq))
        dil.append(band_group(zd, g, d, batch, seq, (0, pairs, 2 * pairs)))
    yd = retention(z, ret_decay, ret_tables, batch, seq)
    return merge_project(x, ya, yb, [o for o, _ in dil], [l for _, l in dil], yd, z,
                         w_branch.astype(jnp.bfloat16), w_out.astype(jnp.bfloat16), seq)


def trunk(x, norm_mix, w_in, conv_w, conv_b, lru_gate_w, lru_gate_b, lru_lambda, mla_q_norm, mla_kv_norm,
          w_uq, w_ukv, ret_decay, w_branch, w_out, norm_ffn, w_router, w_gate, w_up, w_down, norm_final):
    batch, seq, d = x.shape
    x = x.reshape(batch * seq, d)
    mla_tables = mla_rope_tables(seq)
    ret_tables = ret_rope_tables(seq)
    bf = jnp.bfloat16
    for l in range(norm_mix.shape[0]):
        x = mixer(x, batch, seq, norm_mix[l], w_in[l], conv_w[l], conv_b[l], lru_gate_w[l], lru_gate_b[l],
                  lru_lambda[l], mla_q_norm[l], mla_kv_norm[l], w_uq[l], w_ukv[l], ret_decay[l],
                  w_branch[l], w_out[l], mla_tables, ret_tables)
        x = ec_moe(x, norm_ffn[l], w_router[l], w_gate[l].astype(bf), w_up[l].astype(bf), w_down[l].astype(bf))
    return rmsnorm_pallas(x, norm_final, jnp.float32, min(1024, batch * seq)).reshape(batch, seq, d)


def kernel(x_prompt, x_sample, norm_mix, w_in, conv_w, conv_b, lru_gate_w, lru_gate_b, lru_lambda,
           mla_q_norm, mla_kv_norm, w_uq, w_ukv, ret_decay, w_branch, w_out, norm_ffn, w_router,
           w_gate, w_up, w_down, norm_final):
    args = (norm_mix, w_in, conv_w, conv_b, lru_gate_w, lru_gate_b, lru_lambda, mla_q_norm, mla_kv_norm,
            w_uq, w_ukv, ret_decay, w_branch, w_out, norm_ffn, w_router, w_gate, w_up, w_down, norm_final)
    return trunk(x_prompt, *args), trunk(x_sample, *args)
```

```python
import functools
import math

import numpy as np
import jax
import jax.numpy as jnp
from jax import lax
from jax.experimental import pallas as pl
from jax.experimental.pallas import tpu as pltpu

D_MODEL = 1024
DEPTH = 4
EPS = 1e-6
NEG_INF = -1e30
ROPE_BASE = 10000.0
N_BRANCHES = 4
LRU_WIDTH = 512
LRU_BLOCKS = 8
LRU_BLOCK = LRU_WIDTH // LRU_BLOCKS
CONV_WIDTH = 4
CONV_LEFT = 2
LRU_C = 8.0
MLA_HEADS = 8
MLA_NOPE = 64
MLA_ROPE = 32
MLA_V = 64
MLA_Q_RANK = 256
MLA_KV_RANK = 128
DIL_PAIRS = ((128, 1), (512, 4), (2048, 16))
DIL_HEADS_PER_GROUP = 8
DIL_HEADS = DIL_HEADS_PER_GROUP * len(DIL_PAIRS)
DIL_HEAD_DIM = 64
RET_HEADS = 4
RET_QK = 128
RET_V = 128
N_EXPERTS = 16
EC_FACTOR = 2
D_EXPERT = 1024

LANES = 128
VMEM_LIMIT_BYTES = 56 * 1024 * 1024

Z_COLS = 9216
Z_PAD = 96
CB_ZG = 0
CB_XA, CB_GA = 32, 36
CB_MLA = 40
CB_DQ, CB_DK, CB_DV = 44, 48, 52
CB_RQ, CB_RK, CB_RV, CB_RG = 56, 60, 64, 68
DIL_GROUP_WIDTH = DIL_HEADS_PER_GROUP * DIL_HEAD_DIM


def _cparams(*sem):
    return pltpu.CompilerParams(dimension_semantics=sem, vmem_limit_bytes=VMEM_LIMIT_BYTES)


def _norm_matmul_kernel(x_ref, g_ref, w_ref, o_ref, h_ref):
    @pl.when(pl.program_id(1) == 0)
    def _():
        x = x_ref[...]
        y = x * lax.rsqrt(jnp.mean(x * x, axis=-1, keepdims=True) + EPS)
        h_ref[...] = (y * g_ref[...]).astype(h_ref.dtype)

    o_ref[...] = jnp.dot(h_ref[...], w_ref[...], preferred_element_type=jnp.float32).astype(o_ref.dtype)


def norm_matmul(x, g, w, out_dtype, tm, tn):
    n, d = x.shape
    c = w.shape[1]
    return pl.pallas_call(
        _norm_matmul_kernel,
        grid=(n // tm, c // tn),
        in_specs=[
            pl.BlockSpec((tm, d), lambda i, j: (i, 0)),
            pl.BlockSpec((1, d), lambda i, j: (0, 0)),
            pl.BlockSpec((d, tn), lambda i, j: (0, j)),
        ],
        out_specs=pl.BlockSpec((tm, tn), lambda i, j: (i, j)),
        out_shape=jax.ShapeDtypeStruct((n, c), out_dtype),
        scratch_shapes=[pltpu.VMEM((tm, d), jnp.bfloat16)],
        compiler_params=_cparams("arbitrary", "arbitrary"),
        name="norm_matmul",
    )(x, g.reshape(1, d), w)


def _norm_matmul_dil_kernel(x_ref, g_ref, w_ref, o_ref, hf_ref, hp_ref, *, dil):
    x = x_ref[...]
    y = x * lax.rsqrt(jnp.mean(x * x, axis=-1, keepdims=True) + EPS)
    y = y * g_ref[...]
    rows = x.shape[0] // dil
    for c in range(x.shape[1] // LANES):
        cols = slice(c * LANES, (c + 1) * LANES)
        hf_ref[c] = y[:, cols]
        for r in range(dil):
            hp_ref[r * rows:(r + 1) * rows, cols] = hf_ref[c, pl.ds(r, rows, stride=dil), :].astype(hp_ref.dtype)
    out = jnp.dot(hp_ref[...], w_ref[...], preferred_element_type=jnp.float32)
    for r in range(dil):
        o_ref[0, r] = out[r * rows:(r + 1) * rows, :].astype(o_ref.dtype)


def norm_matmul_dil(x, g, w, dil, batch, seq, tm):
    n, d = x.shape
    c = w.shape[1]
    n_t = seq // tm
    return pl.pallas_call(
        functools.partial(_norm_matmul_dil_kernel, dil=dil),
        grid=(n // tm,),
        in_specs=[
            pl.BlockSpec((tm, d), lambda i: (i, 0)),
            pl.BlockSpec((1, d), lambda i: (0, 0)),
            pl.BlockSpec((d, c), lambda i: (0, 0)),
        ],
        out_specs=pl.BlockSpec((1, dil, tm // dil, c), lambda i: (i // n_t, 0, i % n_t, 0)),
        out_shape=jax.ShapeDtypeStruct((batch, dil, seq // dil, c), jnp.bfloat16),
        scratch_shapes=[pltpu.VMEM((d // LANES, tm, LANES), jnp.float32), pltpu.VMEM((tm, d), jnp.bfloat16)],
        compiler_params=_cparams("arbitrary"),
        name=f"norm_matmul_dil{dil}",
    )(x, g.reshape(1, d), w)


def _rmsnorm_kernel(x_ref, g_ref, o_ref):
    x = x_ref[...]
    y = x * lax.rsqrt(jnp.mean(x * x, axis=-1, keepdims=True) + EPS)
    o_ref[...] = (y * g_ref[...]).astype(o_ref.dtype)


def rmsnorm_pallas(x, g, out_dtype, tm):
    n, d = x.shape
    return pl.pallas_call(
        _rmsnorm_kernel,
        grid=(n // tm,),
        in_specs=[pl.BlockSpec((tm, d), lambda i: (i, 0)), pl.BlockSpec((1, d), lambda i: (0, 0))],
        out_specs=pl.BlockSpec((tm, d), lambda i: (i, 0)),
        out_shape=jax.ShapeDtypeStruct((n, d), out_dtype),
        compiler_params=_cparams("arbitrary"),
        name="rmsnorm",
    )(x, g.reshape(1, d))


LRU_TC = 256
LRU_HALO = 16


def _softplus(x):
    return jnp.maximum(x, 0.0) + jnp.log(1.0 + jnp.exp(-jnp.abs(x)))


def _gelu_tanh(x):
    return 0.5 * x * (1.0 + jnp.tanh(math.sqrt(2.0 / math.pi) * (x + 0.044715 * (x * x * x))))


def _lru_scan_chunk(a, b, reverse):
    n = a.shape[0]
    row = lax.broadcasted_iota(jnp.int32, a.shape, 0)
    s = 1
    while s < n:
        if reverse:
            keep = row < (n - s)
            a_s = pltpu.roll(a, n - s, 0)
            b_s = pltpu.roll(b, n - s, 0)
        else:
            keep = row >= s
            a_s = pltpu.roll(a, s, 0)
            b_s = pltpu.roll(b, s, 0)
        b = jnp.where(keep, a * b_s + b, b)
        a = jnp.where(keep, a * a_s, a)
        s *= 2
    return a, b


def _lru_kernel(*refs, reverse, n_t):
    if reverse:
        (xp_ref, xc_ref, xn_ref, cw_ref, cb_ref, wg_ref, gb_ref, lam_ref, hf_ref, ga_ref,
         o_ref, carry_ref) = refs
    else:
        (xp_ref, xc_ref, xn_ref, cw_ref, cb_ref, wg_ref, gb_ref, lam_ref, o_ref, carry_ref) = refs
    step = pl.program_id(1)
    t = (n_t - 1 - step) if reverse else step

    @pl.when(step == 0)
    def _():
        carry_ref[...] = jnp.zeros_like(carry_ref)

    prev = jnp.where(t > 0, xp_ref[...].astype(jnp.float32), 0.0)
    nxt = jnp.where(t < n_t - 1, xn_ref[...].astype(jnp.float32), 0.0)
    win = jnp.concatenate([prev, xc_ref[...].astype(jnp.float32), nxt], axis=0)
    xc = cb_ref[...]
    for k in range(CONV_WIDTH):
        lo = LRU_HALO - CONV_LEFT + k
        xc = xc + cw_ref[k:k + 1, :] * win[lo:lo + LRU_TC, :]
    gl = jnp.dot(xc.astype(jnp.bfloat16), wg_ref[0], preferred_element_type=jnp.float32) + gb_ref[0]
    r = jax.nn.sigmoid(gl[:, :LRU_WIDTH])
    i = jax.nn.sigmoid(gl[:, LRU_WIDTH:])
    log_a = (-LRU_C) * r * _softplus(-lam_ref[0])
    a = jnp.exp(log_a)
    b = jnp.sqrt(1.0 - jnp.exp(2.0 * log_a)) * i * xc
    a_cum, b_cum = _lru_scan_chunk(a, b, reverse)
    h = b_cum + a_cum * carry_ref[0:1, :]
    last = 0 if reverse else LRU_TC - 1
    carry_ref[0:1, :] = h[last:last + 1, :]
    if reverse:
        o_ref[...] = (_gelu_tanh(ga_ref[...].astype(jnp.float32)) * (hf_ref[...] + h)).astype(o_ref.dtype)
    else:
        o_ref[...] = h


def lru_direction(z, conv_w, conv_b, wg, gb, lam, batch, seq, reverse, h_fwd=None):
    n_t = seq // LRU_TC
    per_halo = LRU_TC // LRU_HALO
    n_halo = batch * seq // LRU_HALO
    d = 1 if reverse else 0

    def tt(s):
        return (n_t - 1 - s) if reverse else s

    xa_col = CB_XA * LANES // LRU_WIDTH
    ga_col = CB_GA * LANES // LRU_WIDTH

    def cur(b, s):
        return (b * n_t + tt(s), 0)

    def prev(b, s):
        return (jnp.maximum((b * n_t + tt(s)) * per_halo - 1, 0), xa_col)

    def nxt(b, s):
        return (jnp.minimum((b * n_t + tt(s) + 1) * per_halo, n_halo - 1), xa_col)

    const2 = lambda b, s: (0, 0)
    in_specs = [
        pl.BlockSpec((LRU_HALO, LRU_WIDTH), prev),
        pl.BlockSpec((LRU_TC, LRU_WIDTH), lambda b, s: (b * n_t + tt(s), xa_col)),
        pl.BlockSpec((LRU_HALO, LRU_WIDTH), nxt),
        pl.BlockSpec((CONV_WIDTH, LRU_WIDTH), const2),
        pl.BlockSpec((1, LRU_WIDTH), const2),
        pl.BlockSpec((1, LRU_WIDTH, 2 * LRU_WIDTH), lambda b, s: (d, 0, 0)),
        pl.BlockSpec((1, 1, 2 * LRU_WIDTH), lambda b, s: (d, 0, 0)),
        pl.BlockSpec((1, 1, LRU_WIDTH), lambda b, s: (d, 0, 0)),
    ]
    args = [z, z, z, conv_w, conv_b.reshape(1, LRU_WIDTH), wg, gb, lam.reshape(2, 1, LRU_WIDTH)]
    if reverse:
        in_specs += [pl.BlockSpec((LRU_TC, LRU_WIDTH), cur),
                     pl.BlockSpec((LRU_TC, LRU_WIDTH), lambda b, s: (b * n_t + tt(s), ga_col))]
        args += [h_fwd, z]
        out_dtype = jnp.bfloat16
    else:
        out_dtype = jnp.float32
    return pl.pallas_call(
        functools.partial(_lru_kernel, reverse=reverse, n_t=n_t),
        grid=(batch, n_t),
        in_specs=in_specs,
        out_specs=pl.BlockSpec((LRU_TC, LRU_WIDTH), cur),
        out_shape=jax.ShapeDtypeStruct((batch * seq, LRU_WIDTH), out_dtype),
        scratch_shapes=[pltpu.VMEM((8, LRU_WIDTH), jnp.float32)],
        compiler_params=_cparams("arbitrary", "arbitrary"),
        name="lru_bwd" if reverse else "lru_fwd",
    )(*args)


def lru_gate_dense(gate_w, gate_b):
    eye = jnp.eye(LRU_BLOCKS, dtype=gate_w.dtype)
    dense = jnp.einsum('dgnij,nm->dgnimj', gate_w, eye).reshape(2, 2, LRU_WIDTH, LRU_WIDTH)
    wg = jnp.concatenate([dense[:, 0], dense[:, 1]], axis=-1).astype(jnp.bfloat16)
    gb = jnp.concatenate([gate_b[:, 0], gate_b[:, 1]], axis=-1).reshape(2, 1, 2 * LRU_WIDTH)
    return wg, gb


MLA_TM = 512
MLA_TQ = 512
MLA_TK = 512
MLA_DP = 128


def _mla_proj_kernel(z_ref, qn_ref, kn_ref, wqa_ref, wqb_ref, wk_ref, wv_ref, ea_ref, eb_ref,
                     cos_ref, sin_ref, q_ref, k_ref, v_ref):
    z = z_ref[...]
    cq = z[:, :MLA_Q_RANK].astype(jnp.float32)
    ckv = z[:, MLA_Q_RANK:MLA_Q_RANK + MLA_KV_RANK].astype(jnp.float32)
    kr = z[:, MLA_Q_RANK + MLA_KV_RANK:]
    cqn = (cq * lax.rsqrt(jnp.mean(cq * cq, axis=-1, keepdims=True) + EPS) * qn_ref[...]).astype(jnp.bfloat16)
    ckn = (ckv * lax.rsqrt(jnp.mean(ckv * ckv, axis=-1, keepdims=True) + EPS) * kn_ref[...]).astype(jnp.bfloat16)
    cos = cos_ref[...]
    sin = sin_ref[...]
    f32 = jnp.float32
    k_rope = (jnp.dot(kr, ea_ref[...], preferred_element_type=f32) * cos
              + jnp.dot(kr, eb_ref[...], preferred_element_type=f32) * sin)
    scale = (MLA_NOPE + MLA_ROPE) ** -0.5 * math.log2(math.e)
    row = lax.broadcasted_iota(jnp.int32, (MLA_DP, MLA_TM), 0)
    for h in range(MLA_HEADS):
        qa = jnp.dot(cqn, wqa_ref[h], preferred_element_type=f32)
        qb = jnp.dot(cqn, wqb_ref[h], preferred_element_type=f32)
        q_ref[0, h] = ((qa * cos + qb * sin) * scale).astype(q_ref.dtype)
        k_ref[0, h] = (jnp.dot(ckn, wk_ref[h], preferred_element_type=f32) + k_rope).astype(k_ref.dtype)
        vt = lax.dot_general(wv_ref[h], ckn, (((1,), (1,)), ((), ())), preferred_element_type=f32)
        vt = jnp.where(row == _mla_ones_row(h), 1.0, vt)
        for c in range(MLA_TM // MLA_TK):
            v_ref[0, h, c] = vt[:, c * MLA_TK:(c + 1) * MLA_TK].astype(v_ref.dtype)


def _mla_ones_row(head):
    return MLA_V if head % 2 == 0 else 0


def _rot_half_matrix(n):
    half = n // 2
    r = np.zeros((n, n), np.float32)
    for j in range(half):
        r[half + j, j] = -1.0
        r[j, half + j] = 1.0
    return r


def mla_weights(w_uq, w_ukv):
    rot = jnp.asarray(_rot_half_matrix(MLA_ROPE))
    wq = jnp.transpose(w_uq, (1, 0, 2))
    pad = lambda a, lo, hi: jnp.pad(a, ((0, 0), (0, 0), (lo, hi)))
    wqa = pad(wq, 0, MLA_DP - MLA_NOPE - MLA_ROPE)
    wqb = pad(jnp.einsum('hrd,de->hre', wq[..., MLA_NOPE:], rot), MLA_NOPE, MLA_DP - MLA_NOPE - MLA_ROPE)
    wkv = jnp.transpose(w_ukv, (1, 0, 2))
    wk = pad(wkv[..., :MLA_NOPE], 0, MLA_DP - MLA_NOPE)
    wv_even = pad(wkv[..., MLA_NOPE:], 0, MLA_V)
    wv_odd = pad(wkv[..., MLA_NOPE:], MLA_V, 0)
    wv = jnp.where((jnp.arange(MLA_HEADS) % 2 == 0)[:, None, None], wv_even, wv_odd)
    wv = jnp.transpose(wv, (0, 2, 1))
    ea = np.zeros((LANES, MLA_DP), np.float32)
    for j in range(MLA_ROPE):
        ea[j, MLA_NOPE + j] = 1.0
    eb = np.zeros((LANES, MLA_DP), np.float32)
    eb[:MLA_ROPE, MLA_NOPE:MLA_NOPE + MLA_ROPE] = _rot_half_matrix(MLA_ROPE)
    bf = jnp.bfloat16
    return (wqa.astype(bf), wqb.astype(bf), wk.astype(bf), wv.astype(bf),
            jnp.asarray(ea, bf), jnp.asarray(eb, bf))


def mla_rope_tables(seq):
    half = MLA_ROPE // 2
    inv = ROPE_BASE ** (-jnp.arange(half, dtype=jnp.float32) / half)
    ang = jnp.arange(seq, dtype=jnp.float32)[:, None] * inv[None, :]
    ones = jnp.ones((seq, MLA_NOPE), jnp.float32)
    zeros = jnp.zeros((seq, MLA_DP - MLA_NOPE - MLA_ROPE), jnp.float32)
    cos = jnp.concatenate([ones, jnp.cos(ang), jnp.cos(ang), zeros], axis=1)
    sin = jnp.concatenate([0.0 * ones, jnp.sin(ang), jnp.sin(ang), zeros], axis=1)
    return cos, sin


def mla_project(z, q_norm, kv_norm, weights, tables, batch, seq):
    wqa, wqb, wk, wv, ea, eb = weights
    cos, sin = tables
    n_t = seq // MLA_TM
    hshape = (batch, MLA_HEADS, seq, MLA_DP)
    c3 = lambda b, t: (0, 0, 0)
    c2 = lambda b, t: (0, 0)
    hspec = pl.BlockSpec((1, MLA_HEADS, MLA_TM, MLA_DP), lambda b, t: (b, 0, t, 0))
    return pl.pallas_call(
        _mla_proj_kernel,
        grid=(batch, n_t),
        in_specs=[
            pl.BlockSpec((MLA_TM, 4 * LANES), lambda b, t: (b * n_t + t, CB_MLA // 4)),
            pl.BlockSpec((1, MLA_Q_RANK), c2),
            pl.BlockSpec((1, MLA_KV_RANK), c2),
            pl.BlockSpec((MLA_HEADS, MLA_Q_RANK, MLA_DP), c3),
            pl.BlockSpec((MLA_HEADS, MLA_Q_RANK, MLA_DP), c3),
            pl.BlockSpec((MLA_HEADS, MLA_KV_RANK, MLA_DP), c3),
            pl.BlockSpec((MLA_HEADS, MLA_KV_RANK, MLA_DP), c3),
            pl.BlockSpec((LANES, MLA_DP), c2),
            pl.BlockSpec((LANES, MLA_DP), c2),
            pl.BlockSpec((MLA_TM, MLA_DP), lambda b, t: (t, 0)),
            pl.BlockSpec((MLA_TM, MLA_DP), lambda b, t: (t, 0)),
        ],
        out_specs=[hspec, hspec,
                   pl.BlockSpec((1, MLA_HEADS, MLA_TM // MLA_TK, MLA_DP, MLA_TK), lambda b, t: (b, 0, t, 0, 0))],
        out_shape=[jax.ShapeDtypeStruct(hshape, jnp.bfloat16)] * 2
        + [jax.ShapeDtypeStruct((batch, MLA_HEADS, seq // MLA_TK, MLA_DP, MLA_TK), jnp.bfloat16)],
        compiler_params=_cparams("arbitrary", "arbitrary"),
        name="mla_project",
    )(z, q_norm.reshape(1, -1), kv_norm.reshape(1, -1), wqa, wqb, wk, wv, ea, eb, cos, sin)


def _mla_attn_kernel(q_ref, k_ref, vt_ref, o_ref, acc_ref, *, n_kv):
    acc_ref[...] = jnp.zeros_like(acc_ref)
    f32 = jnp.float32
    nt = (((1,), (1,)), ((), ()))

    def body(j, carry):
        rows = pl.ds(pl.multiple_of(j * MLA_TK, MLA_TK), MLA_TK)
        new = []
        for hh in range(2):
            m_prev = carry[hh]
            s = lax.dot_general(k_ref[0, hh, rows, :], q_ref[0, hh], nt, preferred_element_type=f32)
            m_next = jnp.maximum(m_prev, jnp.max(s, axis=0, keepdims=True))
            p = jnp.exp2(s - m_next)
            alpha = jnp.exp2(m_prev - m_next)
            new.append(m_next)
            acc_ref[hh] = alpha * acc_ref[hh] + jnp.dot(vt_ref[0, hh, j], p.astype(jnp.bfloat16),
                                                         preferred_element_type=f32)
        return tuple(new)

    m0 = jnp.full((1, MLA_TQ), NEG_INF, f32)
    lax.fori_loop(0, n_kv, body, (m0, m0), unroll=min(4, n_kv))
    acc_a, acc_b = acc_ref[0], acc_ref[1]
    l_a = acc_a[_mla_ones_row(0):_mla_ones_row(0) + 1, :]
    l_b = acc_b[_mla_ones_row(1):_mla_ones_row(1) + 1, :]
    row = lax.broadcasted_iota(jnp.int32, acc_a.shape, 0)
    out_t = jnp.where(row < MLA_V, acc_a / l_a, acc_b / l_b)
    o_ref[...] = out_t.T.astype(o_ref.dtype)


def mla_attention(q, k, vt, batch, seq):
    n_q = seq // MLA_TQ
    n_kv = seq // MLA_TK
    qspec = pl.BlockSpec((1, 2, MLA_TQ, MLA_DP), lambda b, hp, i: (b, hp, i, 0))
    kspec = pl.BlockSpec((1, 2, seq, MLA_DP), lambda b, hp, i: (b, hp, 0, 0))
    vspec = pl.BlockSpec((1, 2, n_kv, MLA_DP, MLA_TK), lambda b, hp, i: (b, hp, 0, 0, 0))
    return pl.pallas_call(
        functools.partial(_mla_attn_kernel, n_kv=n_kv),
        grid=(batch, MLA_HEADS // 2, n_q),
        in_specs=[qspec, kspec, vspec],
        out_specs=pl.BlockSpec((MLA_TQ, LANES), lambda b, hp, i: (b * n_q + i, hp)),
        out_shape=jax.ShapeDtypeStruct((batch * seq, MLA_HEADS * MLA_V), jnp.bfloat16),
        scratch_shapes=[pltpu.VMEM((2, MLA_DP, MLA_TQ), jnp.float32)],
        compiler_params=_cparams("arbitrary", "arbitrary", "arbitrary"),
        name="mla_attention",
    )(q, k, vt)


BAND_Q = 128
BAND_W = 256
BAND_RADIUS = 64
BAND_ROWS_PER_STEP = 2048


def band_bias_table(group, dil):
    n = DIL_HEADS
    slopes = np.asarray([2.0 ** (-8.0 * (h + 1) / n) for h in range(n)], np.float32)
    slopes = slopes[group * DIL_HEADS_PER_GROUP:(group + 1) * DIL_HEADS_PER_GROUP]
    iq = np.arange(BAND_Q)[:, None]
    ik = np.arange(BAND_W)[None, :]
    tabs = []
    for d in range(3):
        dist = np.abs(d * BAND_RADIUS + iq - ik)
        bias = -slopes[:, None, None] * (dil * dist).astype(np.float32)[None]
        tabs.append(np.where((dist <= BAND_RADIUS)[None], bias, np.float32(NEG_INF)))
    return jnp.asarray(np.stack(tabs).astype(np.float32))


def _band_kernel(q_ref, k_ref, v_ref, bias_ref, o_ref, lse_ref, *, length):
    lane = lax.broadcasted_iota(jnp.int32, (1, LANES), 1)
    first = lane < DIL_HEAD_DIM
    scale = DIL_HEAD_DIM ** -0.5

    n_blocks = length // BAND_Q
    n_res = q_ref.shape[1]

    def body(idx, carry):
        rr = idx // n_blocks
        q0 = pl.multiple_of((idx % n_blocks) * BAND_Q, BAND_Q)
        start = pl.multiple_of(jnp.clip(q0 - BAND_RADIUS, 0, length - BAND_W), BAND_RADIUS)
        didx = (q0 - start) // BAND_RADIUS
        q = q_ref[0, rr, pl.ds(q0, BAND_Q), :]
        kw = k_ref[0, rr, pl.ds(start, BAND_W), :]
        vw = v_ref[0, rr, pl.ds(start, BAND_W), :]
        outs, lses = [], []
        for hh in range(2):
            sel = first if hh == 0 else jnp.logical_not(first)
            qh = jnp.where(sel, q, jnp.zeros_like(q))
            s = lax.dot_general(qh, kw, (((1,), (1,)), ((), ())), preferred_element_type=jnp.float32)
            s = s * scale + bias_ref[didx, hh]
            m = jnp.max(s, axis=1, keepdims=True)
            e = jnp.exp(s - m)
            den = jnp.sum(e, axis=1, keepdims=True)
            p = (e / den).astype(jnp.bfloat16)
            outs.append(jnp.dot(p, vw, preferred_element_type=jnp.float32))
            lses.append(m + jnp.log(den))
        o_ref[0, rr, pl.ds(q0, BAND_Q), :] = jnp.where(first, outs[0], outs[1]).astype(o_ref.dtype)
        lse_ref[0, rr, pl.ds(q0, BAND_Q), :] = jnp.where(first, lses[0], lses[1])
        return carry

    lax.fori_loop(0, n_res * n_blocks, body, 0, unroll=min(4, n_res * n_blocks))


def band_group(src, group, dil, batch, seq, col_blocks):
    length = seq // dil
    pairs = DIL_HEADS_PER_GROUP // 2
    n_res = max(1, min(dil, BAND_ROWS_PER_STEP // length))

    def zspec(cb):
        return pl.BlockSpec((1, n_res, length, LANES), lambda b, r, hp: (b, r, 0, cb + hp))

    ospec = pl.BlockSpec((1, n_res, length, LANES), lambda b, r, hp: (b, r, 0, hp))
    oshape = (batch, dil, length, DIL_GROUP_WIDTH)
    return pl.pallas_call(
        functools.partial(_band_kernel, length=length),
        grid=(batch, dil // n_res, pairs),
        in_specs=[zspec(col_blocks[0]), zspec(col_blocks[1]), zspec(col_blocks[2]),
                  pl.BlockSpec((3, 2, BAND_Q, BAND_W), lambda b, r, hp: (0, hp, 0, 0))],
        out_specs=[ospec, ospec],
        out_shape=[jax.ShapeDtypeStruct(oshape, jnp.bfloat16), jax.ShapeDtypeStruct(oshape, jnp.float32)],
        compiler_params=_cparams("arbitrary", "arbitrary", "arbitrary"),
        name=f"band_attention_g{group}",
    )(src, src, src, band_bias_table(group, dil))


RET_C = 256


def _ret_kernel(lg_ref, q_ref, k_ref, v_ref, g_ref, cos_ref, sin_ref, o_ref, qs_ref, ks_ref, o1_ref, *, n_c):
    h = pl.program_id(1)
    lgf = lg_ref[0, h]
    lgb = lg_ref[1, h]
    c = RET_C
    f32 = jnp.float32
    bf = jnp.bfloat16
    ii = lax.broadcasted_iota(jnp.int32, (c, c), 0)
    jj = lax.broadcasted_iota(jnp.int32, (c, c), 1)
    diff = (ii - jj).astype(f32)
    decay = jnp.where(diff >= 0.0, jnp.exp(lgf * jnp.maximum(diff, 0.0)), jnp.exp(lgb * jnp.maximum(-diff, 0.0)))
    idx = lax.broadcasted_iota(jnp.int32, (c, 1), 0).astype(f32)
    xi_f = jnp.exp(lgf * (idx + 1.0))
    zeta_f = jnp.exp(lgf * (c - 1.0 - idx))
    xi_b = jnp.exp(lgb * (c - idx))
    zeta_b = jnp.exp(lgb * idx)
    cd_f = jnp.exp(lgf * c)
    cd_b = jnp.exp(lgb * c)
    kscale = RET_QK ** -0.5

    def rope(x, rows):
        return x * cos_ref[rows, :] + pltpu.roll(x, RET_QK // 2, 1) * sin_ref[rows, :]

    def fwd(n, state):
        rows = pl.ds(pl.multiple_of(n * c, c), c)
        q = rope(q_ref[0, rows, :].astype(f32), rows)
        k = rope(k_ref[0, rows, :].astype(f32), rows) * kscale
        v = v_ref[0, rows, :]
        qs_ref[rows, :] = q
        ks_ref[rows, :] = k
        qb = q.astype(bf)
        s = lax.dot_general(qb, k.astype(bf), (((1,), (1,)), ((), ())), preferred_element_type=f32) * decay
        o = jnp.dot(s.astype(bf), v, preferred_element_type=f32)
        o = o + xi_f * jnp.dot(qb, state.astype(bf), preferred_element_type=f32)
        o1_ref[rows, :] = o
        kz = (k * zeta_f).T.astype(bf)
        return cd_f * state + jnp.dot(kz, v, preferred_element_type=f32)

    lax.fori_loop(0, n_c, fwd, jnp.zeros((RET_QK, RET_V), f32), unroll=2)

    def bwd(step, state):
        n = n_c - 1 - step
        rows = pl.ds(pl.multiple_of(n * c, c), c)
        q = qs_ref[rows, :]
        k = ks_ref[rows, :]
        v = v_ref[0, rows, :]
        of = o1_ref[rows, :] + xi_b * jnp.dot(q.astype(bf), state.astype(bf), preferred_element_type=f32)
        mu = jnp.mean(of, axis=-1, keepdims=True)
        var = jnp.mean(jnp.square(of - mu), axis=-1, keepdims=True)
        of = (of - mu) * lax.rsqrt(var + EPS)
        g = g_ref[0, rows, :].astype(f32)
        o_ref[0, rows, :] = (g * jax.nn.sigmoid(g) * of).astype(o_ref.dtype)
        kz = (k * zeta_b).T.astype(bf)
        return cd_b * state + jnp.dot(kz, v, preferred_element_type=f32)

    lax.fori_loop(0, n_c, bwd, jnp.zeros((RET_QK, RET_V), f32), unroll=2)


def ret_rope_tables(seq):
    half = RET_QK // 2
    inv = ROPE_BASE ** (-jnp.arange(half, dtype=jnp.float32) / half)
    ang = jnp.arange(seq, dtype=jnp.float32)[:, None] * inv[None, :]
    cos = jnp.concatenate([jnp.cos(ang), jnp.cos(ang)], axis=1)
    sin = jnp.concatenate([-jnp.sin(ang), jnp.sin(ang)], axis=1)
    return cos, sin


def retention(z, ret_decay, tables, batch, seq):
    zv = z.reshape(batch, seq, Z_COLS)
    log_gamma = jax.nn.log_sigmoid(ret_decay.astype(jnp.float32))
    cos, sin = tables

    def zspec(cb):
        return pl.BlockSpec((1, seq, LANES), lambda b, h: (b, 0, cb + h))

    tspec = pl.BlockSpec((seq, RET_QK), lambda b, h: (0, 0))
    out = pl.pallas_call(
        functools.partial(_ret_kernel, n_c=seq // RET_C),
        grid=(batch, RET_HEADS),
        in_specs=[pl.BlockSpec(memory_space=pltpu.SMEM),
                  zspec(CB_RQ), zspec(CB_RK), zspec(CB_RV), zspec(CB_RG), tspec, tspec],
        out_specs=pl.BlockSpec((1, seq, LANES), lambda b, h: (b, 0, h)),
        out_shape=jax.ShapeDtypeStruct((batch, seq, RET_HEADS * RET_V), jnp.bfloat16),
        scratch_shapes=[pltpu.VMEM((seq, RET_QK), jnp.float32)] * 3,
        compiler_params=_cparams("arbitrary", "arbitrary"),
        name="retention",
    )(log_gamma, zv, zv, zv, zv, cos, sin)
    return out.reshape(batch * seq, RET_HEADS * RET_V)


MERGE_TM = 512


def _merge_kernel(x_ref, ya_ref, yb_ref, o0_ref, o1_ref, o2_ref, l0_ref, l1_ref, l2_ref, yd_ref,
                  zg_ref, wb_ref, wo_ref, out_ref, tok_ref):
    f32 = jnp.float32

    def token_order(slot, src_ref):
        dil, rows, width = src_ref.shape[1:]
        tiles = width // LANES
        for r in range(dil):
            v = src_ref[0, r].astype(f32)
            for c in range(tiles):
                tok_ref[slot * tiles + c, pl.ds(r, rows, stride=dil), :] = v[:, c * LANES:(c + 1) * LANES]
        return jnp.concatenate([tok_ref[slot * tiles + c] for c in range(tiles)], axis=1)

    l0 = l0_ref[0, 0]
    l1, l2 = token_order(0, l1_ref), token_order(1, l2_ref)
    o1, o2 = token_order(2, o1_ref), token_order(3, o2_ref)
    m = jnp.maximum(jnp.maximum(l0, l1), l2)
    e0, e1, e2 = jnp.exp(l0 - m), jnp.exp(l1 - m), jnp.exp(l2 - m)
    inv = 1.0 / (e0 + e1 + e2)
    yc = ((e0 * inv) * o0_ref[0, 0].astype(f32) + (e1 * inv) * o1 + (e2 * inv) * o2).astype(jnp.bfloat16)
    merged = None
    for i, y in enumerate((ya_ref[...], yb_ref[...], yc, yd_ref[...])):
        gate = jax.nn.sigmoid(zg_ref[:, i * D_MODEL:(i + 1) * D_MODEL].astype(f32))
        term = gate * jnp.dot(y, wb_ref[i], preferred_element_type=f32)
        merged = term if merged is None else merged + term
    out_ref[...] = x_ref[...] + jnp.dot(merged.astype(jnp.bfloat16), wo_ref[...], preferred_element_type=f32)


def merge_project(x, ya, yb, dil_o, dil_lse, yd, z, w_branch, w_out, seq):
    n = x.shape[0]
    tm = MERGE_TM
    n_t = seq // tm
    width = 4 * LANES
    row = lambda i: (i, 0)
    bspec = pl.BlockSpec((tm, width), row)

    def dspec(dil):
        return pl.BlockSpec((1, dil, tm // dil, width), lambda i: (i // n_t, 0, i % n_t, 0))

    dspecs = [dspec(d) for _, d in DIL_PAIRS]
    return pl.pallas_call(
        _merge_kernel,
        grid=(n // tm,),
        in_specs=[pl.BlockSpec((tm, D_MODEL), row), bspec, bspec, *dspecs, *dspecs, bspec,
                  pl.BlockSpec((tm, N_BRANCHES * D_MODEL), lambda i: (i, CB_ZG * LANES // (N_BRANCHES * D_MODEL))),
                  pl.BlockSpec((N_BRANCHES, width, D_MODEL), lambda i: (0, 0, 0)),
                  pl.BlockSpec((D_MODEL, D_MODEL), lambda i: (0, 0))],
        out_specs=pl.BlockSpec((tm, D_MODEL), row),
        out_shape=jax.ShapeDtypeStruct((n, D_MODEL), jnp.float32),
        scratch_shapes=[pltpu.VMEM((4 * width // LANES, tm, LANES), jnp.float32)],
        compiler_params=_cparams("arbitrary"),
        name="merge_project",
    )(x, ya, yb, *dil_o, *dil_lse, yd, z, w_branch, w_out)


MOE_TM = 512
MOE_TT = 256
MOE_W_SMALL = 64
MOE_ROWS = 1024
MOE_ALIGN = 8
MOE_XW = D_MODEL + LANES


def _router_kernel(x_ref, g_ref, w_ref, h_ref, aff_ref):
    x = x_ref[...]
    y = x * lax.rsqrt(jnp.mean(x * x, axis=-1, keepdims=True) + EPS)
    h = (y * g_ref[...]).astype(jnp.bfloat16)
    h_ref[...] = h
    logits = jnp.dot(h, w_ref[...], preferred_element_type=jnp.float32)
    lane = lax.broadcasted_iota(jnp.int32, logits.shape, 1)
    logits = jnp.where(lane < N_EXPERTS, logits, NEG_INF)
    e = jnp.exp(logits - jnp.max(logits, axis=-1, keepdims=True))
    aff_t = (e / jnp.sum(e, axis=-1, keepdims=True)).T
    for c in range(x.shape[0] // MOE_TT):
        aff_ref[c] = aff_t[:N_EXPERTS, c * MOE_TT:(c + 1) * MOE_TT]


def moe_router(x, norm_g, w_router):
    n, d = x.shape
    tm = 2 * MOE_TT
    w = jnp.pad(w_router.astype(jnp.bfloat16), ((0, 0), (0, LANES - N_EXPERTS)))
    return pl.pallas_call(
        _router_kernel,
        grid=(n // tm,),
        in_specs=[pl.BlockSpec((tm, d), lambda i: (i, 0)), pl.BlockSpec((1, d), lambda i: (0, 0)),
                  pl.BlockSpec((d, LANES), lambda i: (0, 0))],
        out_specs=[pl.BlockSpec((tm, d), lambda i: (i, 0)),
                   pl.BlockSpec((tm // MOE_TT, N_EXPERTS, MOE_TT), lambda i: (i, 0, 0))],
        out_shape=[jax.ShapeDtypeStruct((n, d), jnp.bfloat16),
                   jax.ShapeDtypeStruct((n // MOE_TT, N_EXPERTS, MOE_TT), jnp.float32)],
        compiler_params=_cparams("arbitrary"),
        name="moe_router",
    )(x, norm_g.reshape(1, d), w)


def _affinity_bits(a):
    return lax.bitcast_convert_type(a, jnp.int32)


def _threshold_kernel(aff_ref, thr_ref, need_ref, *, cap, n_tiles):
    def count(pred, thr):
        def tile(c, acc):
            return acc + jnp.where(pred(_affinity_bits(aff_ref[c]), thr), 1.0, 0.0)
        acc = lax.fori_loop(0, n_tiles, tile, jnp.zeros((N_EXPERTS, MOE_TT), jnp.float32), unroll=8)
        return jnp.sum(acc, axis=1, keepdims=True)

    def bit(i, thr):
        cand = thr | jnp.left_shift(jnp.int32(1), 30 - i)
        return jnp.where(count(lambda b, t: b >= t, cand) >= cap, cand, thr)

    thr = lax.fori_loop(0, 31, bit, jnp.zeros((N_EXPERTS, 1), jnp.int32))
    thr_ref[...] = jnp.broadcast_to(thr, thr_ref.shape)
    need_ref[...] = jnp.broadcast_to(cap - count(lambda b, t: b > t, thr), need_ref.shape)


def moe_threshold(aff, cap):
    n_tiles = aff.shape[0]
    full = pl.BlockSpec((N_EXPERTS, LANES), lambda i: (0, 0))
    return pl.pallas_call(
        functools.partial(_threshold_kernel, cap=cap, n_tiles=n_tiles),
        grid=(1,),
        in_specs=[pl.BlockSpec(aff.shape, lambda i: (0, 0, 0))],
        out_specs=[full, full],
        out_shape=[jax.ShapeDtypeStruct((N_EXPERTS, LANES), jnp.int32),
                   jax.ShapeDtypeStruct((N_EXPERTS, LANES), jnp.float32)],
        compiler_params=_cparams("arbitrary"),
        name="moe_threshold",
    )(aff)


def _assign_kernel(aff_ref, thr_ref, need_ref, codet_ref, coden_ref, start_ref, total_ref, ties_ref, run_ref):
    @pl.when(pl.program_id(0) == 0)
    def _():
        ties_ref[...] = jnp.zeros_like(ties_ref)
        run_ref[...] = jnp.zeros_like(run_ref)

    f32 = jnp.float32
    bits = _affinity_bits(aff_ref[0])
    thr = thr_ref[:, :1]
    ii = lax.broadcasted_iota(jnp.int32, (MOE_TT, MOE_TT), 0)
    jj = lax.broadcasted_iota(jnp.int32, (MOE_TT, MOE_TT), 1)
    tri = jnp.where(ii <= jj, 1.0, 0.0).astype(jnp.bfloat16)
    eq = bits == thr
    eq_f = jnp.where(eq, 1.0, 0.0)
    tie_rank = jnp.dot(eq_f.astype(jnp.bfloat16), tri, preferred_element_type=f32) + ties_ref[:, :1]
    sel = jnp.logical_or(bits > thr, jnp.logical_and(eq, tie_rank <= need_ref[:, :1]))
    sel_f = jnp.where(sel, 1.0, 0.0)
    incl = jnp.dot(sel_f.astype(jnp.bfloat16), tri, preferred_element_type=f32)
    code = jnp.where(sel, incl - 1.0, -1.0)
    codet_ref[0] = code.astype(jnp.int32)
    padded = jnp.concatenate([code, jnp.zeros((LANES - N_EXPERTS, MOE_TT), f32)], axis=0)
    coden_ref[...] = padded.T.astype(jnp.int32)
    start_ref[0] = run_ref[...].astype(jnp.int32)
    taken = jnp.sum(sel_f, axis=1, keepdims=True)
    run_ref[...] = run_ref[...] + jnp.floor((taken + (MOE_ALIGN - 1)) * (1.0 / MOE_ALIGN)) * MOE_ALIGN
    total_ref[...] = run_ref[...].astype(jnp.int32)
    ties_ref[...] = ties_ref[...] + jnp.sum(eq_f, axis=1, keepdims=True)


def moe_assign(aff, thr, need):
    n_tiles = aff.shape[0]
    tile = pl.BlockSpec((1, N_EXPERTS, MOE_TT), lambda c: (c, 0, 0))
    full = pl.BlockSpec((N_EXPERTS, LANES), lambda c: (0, 0))
    return pl.pallas_call(
        _assign_kernel,
        grid=(n_tiles,),
        in_specs=[tile, full, full],
        out_specs=[tile, pl.BlockSpec((MOE_TT, LANES), lambda c: (c, 0)),
                   pl.BlockSpec((1, N_EXPERTS, LANES), lambda c: (c, 0, 0)), full],
        out_shape=[jax.ShapeDtypeStruct((n_tiles, N_EXPERTS, MOE_TT), jnp.int32),
                   jax.ShapeDtypeStruct((n_tiles * MOE_TT, LANES), jnp.int32),
                   jax.ShapeDtypeStruct((n_tiles, N_EXPERTS, LANES), jnp.int32),
                   jax.ShapeDtypeStruct((N_EXPERTS, LANES), jnp.int32)],
        scratch_shapes=[pltpu.VMEM((N_EXPERTS, LANES), jnp.float32)] * 2,
        compiler_params=_cparams("arbitrary"),
        name="moe_assign",
    )(aff, thr, need)


def _dispatch_kernel(start_ref, big_ref, h_ref, code_ref, aff_ref, init_hbm, x_hbm, buf_ref, sem, *, n_tiles):
    del init_hbm
    c = pl.program_id(0)
    cur = c % 2
    f32 = jnp.float32
    h = h_ref[...]
    code = code_ref[0]
    gate = aff_ref[0]

    def copies(tile, slot, first, width, group):
        return [pltpu.make_async_copy(
            buf_ref.at[slot, pl.ds(k * width, width)],
            x_hbm.at[first + k, pl.ds(_slot_start(start_ref, tile, first + k), width)],
            sem.at[slot, k]) for k in range(group)]

    def stage(first, width, group):
        slot = lax.broadcasted_iota(jnp.int32, (width, MOE_TT), 0)
        onehots = []
        for k in range(group):
            e = first + k
            hit = slot == code[e:e + 1, :]
            onehots.append(jnp.where(hit, 1.0, 0.0).astype(jnp.bfloat16))
            gwin = jnp.sum(jnp.where(hit, gate[e:e + 1, :], 0.0), axis=1, keepdims=True)
            buf_ref[cur, k * width:(k + 1) * width, D_MODEL:] = jnp.broadcast_to(gwin, (width, LANES))
        buf_ref[cur, :, :D_MODEL] = jnp.dot(jnp.concatenate(onehots, axis=0), h, preferred_element_type=f32)

    def wait_previous_small():
        prev = jnp.maximum(c - 1, 0)

        @pl.when(jnp.logical_and(c > 0, big_ref[prev] == 0))
        def _():
            for cp in copies(prev, 1 - cur, 0, MOE_W_SMALL, N_EXPERTS):
                cp.wait()

    @pl.when(big_ref[c] == 0)
    def _():
        stage(0, MOE_W_SMALL, N_EXPERTS)
        wait_previous_small()
        mine = copies(c, cur, 0, MOE_W_SMALL, N_EXPERTS)
        for cp in mine:
            cp.start()

        @pl.when(c == n_tiles - 1)
        def _():
            for cp in mine:
                cp.wait()

    @pl.when(big_ref[c] != 0)
    def _():
        wait_previous_small()
        group = MOE_ROWS // MOE_TT
        for first in range(0, N_EXPERTS, group):
            stage(first, MOE_TT, group)
            mine = copies(c, cur, first, MOE_TT, group)
            for cp in mine:
                cp.start()
            for cp in mine:
                cp.wait()


def _slot_start(start_ref, tile, expert):
    return pl.multiple_of(start_ref[tile * N_EXPERTS + expert], MOE_ALIGN)


def moe_dispatch(starts, big, h, code_t, aff, slots):
    n, d = h.shape
    n_tiles = n // MOE_TT
    tile = pl.BlockSpec((1, N_EXPERTS, MOE_TT), lambda c, s, b: (c, 0, 0))
    shape = (N_EXPERTS, slots, MOE_XW)
    return pl.pallas_call(
        functools.partial(_dispatch_kernel, n_tiles=n_tiles),
        grid_spec=pltpu.PrefetchScalarGridSpec(
            num_scalar_prefetch=2,
            grid=(n_tiles,),
            in_specs=[pl.BlockSpec((MOE_TT, d), lambda c, s, b: (c, 0)), tile, tile,
                      pl.BlockSpec(memory_space=pl.ANY)],
            out_specs=pl.BlockSpec(memory_space=pl.ANY),
            scratch_shapes=[pltpu.VMEM((2, MOE_ROWS, MOE_XW), jnp.float32),
                            pltpu.SemaphoreType.DMA((2, MOE_ROWS // MOE_W_SMALL))],
        ),
        out_shape=jax.ShapeDtypeStruct(shape, jnp.float32),
        input_output_aliases={5: 0},
        compiler_params=_cparams("arbitrary"),
        name="moe_dispatch",
    )(starts, big, h, code_t, aff, jnp.zeros(shape, jnp.float32))


def _expert_kernel(total_ref, x_ref, wg_ref, wu_ref, wd_ref, o_ref):
    f32 = jnp.float32
    used = pl.program_id(1) * MOE_TM < total_ref[pl.program_id(0)]

    @pl.when(used)
    def _():
        x = x_ref[0, :, :D_MODEL].astype(jnp.bfloat16)
        a = jnp.dot(x, wg_ref[0], preferred_element_type=f32)
        u = jnp.dot(x, wu_ref[0], preferred_element_type=f32)
        he = (a * jax.nn.sigmoid(a) * u).astype(jnp.bfloat16)
        o_ref[0] = jnp.dot(he, wd_ref[0], preferred_element_type=f32) * x_ref[0, :, D_MODEL:D_MODEL + 1]

    @pl.when(jnp.logical_not(used))
    def _():
        o_ref[0] = jnp.zeros_like(o_ref[0])


def expert_ffn(total, xd, w_gate, w_up, w_down):
    e, slots, _ = xd.shape
    d = D_MODEL
    tm = MOE_TM
    wspec = pl.BlockSpec((1, d, D_EXPERT), lambda i, j, t: (i, 0, 0))
    return pl.pallas_call(
        _expert_kernel,
        grid_spec=pltpu.PrefetchScalarGridSpec(
            num_scalar_prefetch=1,
            grid=(e, slots // tm),
            in_specs=[pl.BlockSpec((1, tm, MOE_XW), lambda i, j, t: (i, j, 0)),
                      wspec, wspec,
                      pl.BlockSpec((1, D_EXPERT, d), lambda i, j, t: (i, 0, 0))],
            out_specs=pl.BlockSpec((1, tm, d), lambda i, j, t: (i, j, 0)),
        ),
        out_shape=jax.ShapeDtypeStruct((e, slots, d), jnp.float32),
        compiler_params=_cparams("arbitrary", "arbitrary"),
        name="expert_ffn",
    )(total, xd, w_gate, w_up, w_down)


def _combine_kernel(start_ref, big_ref, x_ref, code_ref, y_hbm, o_ref, buf_ref, sem, *, n_tiles):
    c = pl.program_id(0)
    cur = c % 2
    f32 = jnp.float32
    bf = jnp.bfloat16
    code = code_ref[...]
    o_ref[...] = x_ref[...]

    def copies(tile, slot, first, width, group):
        return [pltpu.make_async_copy(
            y_hbm.at[first + k, pl.ds(_slot_start(start_ref, tile, first + k), width)],
            buf_ref.at[slot, pl.ds(k * width, width)],
            sem.at[slot, k]) for k in range(group)]

    def onehot(first, width, group):
        pieces = []
        if width < LANES:
            lane = lax.broadcasted_iota(jnp.int32, (MOE_TT, LANES), 1)
            for k in range(0, group, 2):
                e = first + k
                target = jnp.where(lane < width, code[:, e:e + 1], code[:, e + 1:e + 2] + width)
                pieces.append(jnp.where(target == lane, 1.0, 0.0).astype(bf))
        else:
            lane = lax.broadcasted_iota(jnp.int32, (MOE_TT, width), 1)
            for k in range(group):
                e = first + k
                pieces.append(jnp.where(code[:, e:e + 1] == lane, 1.0, 0.0).astype(bf))
        return jnp.concatenate(pieces, axis=1)

    def accumulate(oh):
        y = buf_ref[cur]
        y_hi = y.astype(bf)
        y_lo = (y - y_hi.astype(f32)).astype(bf)
        o_ref[...] += (jnp.dot(oh, y_hi, preferred_element_type=f32)
                       + jnp.dot(oh, y_lo, preferred_element_type=f32))

    def prefetch_next():
        nxt = jnp.minimum(c + 1, n_tiles - 1)

        @pl.when(jnp.logical_and(c + 1 < n_tiles, big_ref[nxt] == 0))
        def _():
            for cp in copies(nxt, 1 - cur, 0, MOE_W_SMALL, N_EXPERTS):
                cp.start()

    @pl.when(big_ref[c] == 0)
    def _():
        mine = copies(c, cur, 0, MOE_W_SMALL, N_EXPERTS)

        @pl.when(c == 0)
        def _():
            for cp in mine:
                cp.start()

        prefetch_next()
        oh = onehot(0, MOE_W_SMALL, N_EXPERTS)
        for cp in mine:
            cp.wait()
        accumulate(oh)

    @pl.when(big_ref[c] != 0)
    def _():
        prefetch_next()
        group = MOE_ROWS // MOE_TT
        for first in range(0, N_EXPERTS, group):
            mine = copies(c, cur, first, MOE_TT, group)
            for cp in mine:
                cp.start()
            oh = onehot(first, MOE_TT, group)
            for cp in mine:
                cp.wait()
            accumulate(oh)


def moe_combine(starts, big, x, code_n, y):
    n, d = x.shape
    row = pl.BlockSpec((MOE_TT, d), lambda c, s, b: (c, 0))
    return pl.pallas_call(
        functools.partial(_combine_kernel, n_tiles=n // MOE_TT),
        grid_spec=pltpu.PrefetchScalarGridSpec(
            num_scalar_prefetch=2,
            grid=(n // MOE_TT,),
            in_specs=[row, pl.BlockSpec((MOE_TT, LANES), lambda c, s, b: (c, 0)),
                      pl.BlockSpec(memory_space=pl.ANY)],
            out_specs=row,
            scratch_shapes=[pltpu.VMEM((2, MOE_ROWS, d), jnp.float32),
                            pltpu.SemaphoreType.DMA((2, MOE_ROWS // MOE_W_SMALL))],
        ),
        out_shape=jax.ShapeDtypeStruct((n, d), jnp.float32),
        compiler_params=_cparams("arbitrary"),
        name="moe_combine",
    )(starts, big, x, code_n, y)


def ec_moe(x, norm_g, w_router, w_gate, w_up, w_down):
    n_tok, _ = x.shape
    cap = EC_FACTOR * n_tok // N_EXPERTS
    h, aff = moe_router(x, norm_g, w_router)
    thr, need = moe_threshold(aff, cap)
    code_t, code_n, starts, total = moe_assign(aff, thr, need)
    starts = starts[:, :, 0]
    total = total[:, 0]
    ends = jnp.concatenate([starts[1:], total[None]], axis=0)
    big = (jnp.max(ends - starts, axis=1) > MOE_W_SMALL).astype(jnp.int32)
    starts = starts.reshape(-1)
    n_tiles = n_tok // MOE_TT
    slots = -(-(cap + MOE_ALIGN * n_tiles + MOE_TT) // MOE_TM) * MOE_TM
    xd = moe_dispatch(starts, big, h, code_t, aff, slots)
    y = expert_ffn(total, xd, w_gate, w_up, w_down)
    return moe_combine(starts, big, x, code_n, y)


def split_in_proj(w_in):
    w = w_in.astype(jnp.bfloat16)
    sizes = (LRU_WIDTH, LRU_WIDTH, MLA_Q_RANK, MLA_KV_RANK, MLA_ROPE,
             DIL_HEADS * DIL_HEAD_DIM, DIL_HEADS * DIL_HEAD_DIM, DIL_HEADS * DIL_HEAD_DIM,
             RET_HEADS * RET_QK, RET_HEADS * RET_QK, RET_HEADS * RET_V, RET_HEADS * RET_V,
             N_BRANCHES * D_MODEL)
    parts, off = [], 0
    for s in sizes:
        parts.append(w[:, off:off + s])
        off += s
    xa, ga, cq, ckv, kr, dq, dk, dv, rq, rk, rv, rg, zg = parts
    gw = DIL_GROUP_WIDTH
    grp = lambda a, g: a[:, g * gw:(g + 1) * gw]
    zeros = jnp.zeros((w.shape[0], Z_PAD), w.dtype)
    main = jnp.concatenate([zg, xa, ga, cq, ckv, kr, zeros, grp(dq, 0), grp(dk, 0), grp(dv, 0),
                            rq, rk, rv, rg], axis=1)
    dil = [jnp.concatenate([grp(dq, g), grp(dk, g), grp(dv, g)], axis=1) for g in range(1, len(DIL_PAIRS))]
    return main, dil


def mixer(x, batch, seq, norm_g, w_in, conv_w, conv_b, lru_gate_w, lru_gate_b, lru_lambda, mla_q_norm,
          mla_kv_norm, w_uq, w_ukv, ret_decay, w_branch, w_out, mla_tables, ret_tables):
    n = batch * seq
    w_main, w_dil = split_in_proj(w_in)
    z = norm_matmul(x, norm_g, w_main, jnp.bfloat16, min(1024, n), 1024)
    wg, gb = lru_gate_dense(lru_gate_w, lru_gate_b)
    h_fwd = lru_direction(z, conv_w, conv_b, wg, gb, lru_lambda, batch, seq, False)
    ya = lru_direction(z, conv_w, conv_b, wg, gb, lru_lambda, batch, seq, True, h_fwd)
    q, k, v = mla_project(z, mla_q_norm, mla_kv_norm, mla_weights(w_uq, w_ukv), mla_tables, batch, seq)
    yb = mla_attention(q, k, v, batch, seq)
    dil = [band_group(z.reshape(batch, 1, seq, Z_COLS), 0, 1, batch, seq, (CB_DQ, CB_DK, CB_DV))]
    pairs = DIL_HEADS_PER_GROUP // 2
    for g in range(1, len(DIL_PAIRS)):
        d = DIL_PAIRS[g][1]
        zd = norm_matmul_dil(x, norm_g, w_dil[g - 1], d, batch, seq, min(1024, seq))
        dil.append(band_group(zd, g, d, batch, seq, (0, pairs, 2 * pairs)))
    yd = retention(z, ret_decay, ret_tables, batch, seq)
    return merge_project(x, ya, yb, [o for o, _ in dil], [l for _, l in dil], yd, z,
                         w_branch.astype(jnp.bfloat16), w_out.astype(jnp.bfloat16), seq)


def trunk(x, norm_mix, w_in, conv_w, conv_b, lru_gate_w, lru_gate_b, lru_lambda, mla_q_norm, mla_kv_norm,
          w_uq, w_ukv, ret_decay, w_branch, w_out, norm_ffn, w_router, w_gate, w_up, w_down, norm_final):
    batch, seq, d = x.shape
    x = x.reshape(batch * seq, d)
    mla_tables = mla_rope_tables(seq)
    ret_tables = ret_rope_tables(seq)
    bf = jnp.bfloat16
    for l in range(norm_mix.shape[0]):
        x = mixer(x, batch, seq, norm_mix[l], w_in[l], conv_w[l], conv_b[l], lru_gate_w[l], lru_gate_b[l],
                  lru_lambda[l], mla_q_norm[l], mla_kv_norm[l], w_uq[l], w_ukv[l], ret_decay[l],
                  w_branch[l], w_out[l], mla_tables, ret_tables)
        x = ec_moe(x, norm_ffn[l], w_router[l], w_gate[l].astype(bf), w_up[l].astype(bf), w_down[l].astype(bf))
    return rmsnorm_pallas(x, norm_final, jnp.float32, min(1024, batch * seq)).reshape(batch, seq, d)


def kernel(x_prompt, x_sample, norm_mix, w_in, conv_w, conv_b, lru_gate_w, lru_gate_b, lru_lambda,
           mla_q_norm, mla_kv_norm, w_uq, w_ukv, ret_decay, w_branch, w_out, norm_ffn, w_router,
           w_gate, w_up, w_down, norm_final):
    args = (norm_mix, w_in, conv_w, conv_b, lru_gate_w, lru_gate_b, lru_lambda, mla_q_norm, mla_kv_norm,
            w_uq, w_ukv, ret_decay, w_branch, w_out, norm_ffn, w_router, w_gate, w_up, w_down, norm_final)
    return trunk(x_prompt, *args), trunk(x_sample, *args)
```

```python
import functools
import math

import numpy as np
import jax
import jax.numpy as jnp
from jax import lax
from jax.experimental import pallas as pl
from jax.experimental.pallas import tpu as pltpu

D_MODEL = 1024
DEPTH = 4
EPS = 1e-6
NEG_INF = -1e30
ROPE_BASE = 10000.0
N_BRANCHES = 4
LRU_WIDTH = 512
LRU_BLOCKS = 8
LRU_BLOCK = LRU_WIDTH // LRU_BLOCKS
CONV_WIDTH = 4
CONV_LEFT = 2
LRU_C = 8.0
MLA_HEADS = 8
MLA_NOPE = 64
MLA_ROPE = 32
MLA_V = 64
MLA_Q_RANK = 256
MLA_KV_RANK = 128
DIL_PAIRS = ((128, 1), (512, 4), (2048, 16))
DIL_HEADS_PER_GROUP = 8
DIL_HEADS = DIL_HEADS_PER_GROUP * len(DIL_PAIRS)
DIL_HEAD_DIM = 64
RET_HEADS = 4
RET_QK = 128
RET_V = 128
N_EXPERTS = 16
EC_FACTOR = 2
D_EXPERT = 1024

LANES = 128
VMEM_LIMIT_BYTES = 56 * 1024 * 1024

Z_COLS = 9216
Z_PAD = 96
CB_ZG = 0
CB_XA, CB_GA = 32, 36
CB_MLA = 40
CB_DQ, CB_DK, CB_DV = 44, 48, 52
CB_RQ, CB_RK, CB_RV, CB_RG = 56, 60, 64, 68
DIL_GROUP_WIDTH = DIL_HEADS_PER_GROUP * DIL_HEAD_DIM


def _cparams(*sem):
    return pltpu.CompilerParams(dimension_semantics=sem, vmem_limit_bytes=VMEM_LIMIT_BYTES)


def _norm_matmul_kernel(x_ref, g_ref, w_ref, o_ref, h_ref):
    @pl.when(pl.program_id(1) == 0)
    def _():
        x = x_ref[...]
        y = x * lax.rsqrt(jnp.mean(x * x, axis=-1, keepdims=True) + EPS)
        h_ref[...] = (y * g_ref[...]).astype(h_ref.dtype)

    o_ref[...] = jnp.dot(h_ref[...], w_ref[...], preferred_element_type=jnp.float32).astype(o_ref.dtype)


def norm_matmul(x, g, w, out_dtype, tm, tn):
    n, d = x.shape
    c = w.shape[1]
    return pl.pallas_call(
        _norm_matmul_kernel,
        grid=(n // tm, c // tn),
        in_specs=[
            pl.BlockSpec((tm, d), lambda i, j: (i, 0)),
            pl.BlockSpec((1, d), lambda i, j: (0, 0)),
            pl.BlockSpec((d, tn), lambda i, j: (0, j)),
        ],
        out_specs=pl.BlockSpec((tm, tn), lambda i, j: (i, j)),
        out_shape=jax.ShapeDtypeStruct((n, c), out_dtype),
        scratch_shapes=[pltpu.VMEM((tm, d), jnp.bfloat16)],
        compiler_params=_cparams("arbitrary", "arbitrary"),
        name="norm_matmul",
    )(x, g.reshape(1, d), w)


def _norm_matmul_dil_kernel(x_ref, g_ref, w_ref, o_ref, hf_ref, hp_ref, *, dil):
    x = x_ref[...]
    y = x * lax.rsqrt(jnp.mean(x * x, axis=-1, keepdims=True) + EPS)
    y = y * g_ref[...]
    rows = x.shape[0] // dil
    for c in range(x.shape[1] // LANES):
        cols = slice(c * LANES, (c + 1) * LANES)
        hf_ref[c] = y[:, cols]
        for r in range(dil):
            hp_ref[r * rows:(r + 1) * rows, cols] = hf_ref[c, pl.ds(r, rows, stride=dil), :].astype(hp_ref.dtype)
    out = jnp.dot(hp_ref[...], w_ref[...], preferred_element_type=jnp.float32)
    for r in range(dil):
        o_ref[0, r] = out[r * rows:(r + 1) * rows, :].astype(o_ref.dtype)


def norm_matmul_dil(x, g, w, dil, batch, seq, tm):
    n, d = x.shape
    c = w.shape[1]
    n_t = seq // tm
    return pl.pallas_call(
        functools.partial(_norm_matmul_dil_kernel, dil=dil),
        grid=(n // tm,),
        in_specs=[
            pl.BlockSpec((tm, d), lambda i: (i, 0)),
            pl.BlockSpec((1, d), lambda i: (0, 0)),
            pl.BlockSpec((d, c), lambda i: (0, 0)),
        ],
        out_specs=pl.BlockSpec((1, dil, tm // dil, c), lambda i: (i // n_t, 0, i % n_t, 0)),
        out_shape=jax.ShapeDtypeStruct((batch, dil, seq // dil, c), jnp.bfloat16),
        scratch_shapes=[pltpu.VMEM((d // LANES, tm, LANES), jnp.float32), pltpu.VMEM((tm, d), jnp.bfloat16)],
        compiler_params=_cparams("arbitrary"),
        name=f"norm_matmul_dil{dil}",
    )(x, g.reshape(1, d), w)


def _rmsnorm_kernel(x_ref, g_ref, o_ref):
    x = x_ref[...]
    y = x * lax.rsqrt(jnp.mean(x * x, axis=-1, keepdims=True) + EPS)
    o_ref[...] = (y * g_ref[...]).astype(o_ref.dtype)


def rmsnorm_pallas(x, g, out_dtype, tm):
    n, d = x.shape
    return pl.pallas_call(
        _rmsnorm_kernel,
        grid=(n // tm,),
        in_specs=[pl.BlockSpec((tm, d), lambda i: (i, 0)), pl.BlockSpec((1, d), lambda i: (0, 0))],
        out_specs=pl.BlockSpec((tm, d), lambda i: (i, 0)),
        out_shape=jax.ShapeDtypeStruct((n, d), out_dtype),
        compiler_params=_cparams("arbitrary"),
        name="rmsnorm",
    )(x, g.reshape(1, d))


LRU_TC = 256
LRU_HALO = 16


def _softplus(x):
    return jnp.maximum(x, 0.0) + jnp.log(1.0 + jnp.exp(-jnp.abs(x)))


def _gelu_tanh(x):
    return 0.5 * x * (1.0 + jnp.tanh(math.sqrt(2.0 / math.pi) * (x + 0.044715 * (x * x * x))))


def _lru_scan_chunk(a, b, reverse):
    n = a.shape[0]
    row = lax.broadcasted_iota(jnp.int32, a.shape, 0)
    s = 1
    while s < n:
        if reverse:
            keep = row < (n - s)
            a_s = pltpu.roll(a, n - s, 0)
            b_s = pltpu.roll(b, n - s, 0)
        else:
            keep = row >= s
            a_s = pltpu.roll(a, s, 0)
            b_s = pltpu.roll(b, s, 0)
        b = jnp.where(keep, a * b_s + b, b)
        a = jnp.where(keep, a * a_s, a)
        s *= 2
    return a, b


def _lru_kernel(*refs, reverse, n_t):
    if reverse:
        (xp_ref, xc_ref, xn_ref, cw_ref, cb_ref, wg_ref, gb_ref, lam_ref, hf_ref, ga_ref,
         o_ref, carry_ref) = refs
    else:
        (xp_ref, xc_ref, xn_ref, cw_ref, cb_ref, wg_ref, gb_ref, lam_ref, o_ref, carry_ref) = refs
    step = pl.program_id(1)
    t = (n_t - 1 - step) if reverse else step

    @pl.when(step == 0)
    def _():
        carry_ref[...] = jnp.zeros_like(carry_ref)

    prev = jnp.where(t > 0, xp_ref[...].astype(jnp.float32), 0.0)
    nxt = jnp.where(t < n_t - 1, xn_ref[...].astype(jnp.float32), 0.0)
    win = jnp.concatenate([prev, xc_ref[...].astype(jnp.float32), nxt], axis=0)
    xc = cb_ref[...]
    for k in range(CONV_WIDTH):
        lo = LRU_HALO - CONV_LEFT + k
        xc = xc + cw_ref[k:k + 1, :] * win[lo:lo + LRU_TC, :]
    gl = jnp.dot(xc.astype(jnp.bfloat16), wg_ref[0], preferred_element_type=jnp.float32) + gb_ref[0]
    r = jax.nn.sigmoid(gl[:, :LRU_WIDTH])
    i = jax.nn.sigmoid(gl[:, LRU_WIDTH:])
    log_a = (-LRU_C) * r * _softplus(-lam_ref[0])
    a = jnp.exp(log_a)
    b = jnp.sqrt(1.0 - jnp.exp(2.0 * log_a)) * i * xc
    a_cum, b_cum = _lru_scan_chunk(a, b, reverse)
    h = b_cum + a_cum * carry_ref[0:1, :]
    last = 0 if reverse else LRU_TC - 1
    carry_ref[0:1, :] = h[last:last + 1, :]
    if reverse:
        o_ref[...] = (_gelu_tanh(ga_ref[...].astype(jnp.float32)) * (hf_ref[...] + h)).astype(o_ref.dtype)
    else:
        o_ref[...] = h


def lru_direction(z, conv_w, conv_b, wg, gb, lam, batch, seq, reverse, h_fwd=None):
    n_t = seq // LRU_TC
    per_halo = LRU_TC // LRU_HALO
    n_halo = batch * seq // LRU_HALO
    d = 1 if reverse else 0

    def tt(s):
        return (n_t - 1 - s) if reverse else s

    xa_col = CB_XA * LANES // LRU_WIDTH
    ga_col = CB_GA * LANES // LRU_WIDTH

    def cur(b, s):
        return (b * n_t + tt(s), 0)

    def prev(b, s):
        return (jnp.maximum((b * n_t + tt(s)) * per_halo - 1, 0), xa_col)

    def nxt(b, s):
        return (jnp.minimum((b * n_t + tt(s) + 1) * per_halo, n_halo - 1), xa_col)

    const2 = lambda b, s: (0, 0)
    in_specs = [
        pl.BlockSpec((LRU_HALO, LRU_WIDTH), prev),
        pl.BlockSpec((LRU_TC, LRU_WIDTH), lambda b, s: (b * n_t + tt(s), xa_col)),
        pl.BlockSpec((LRU_HALO, LRU_WIDTH), nxt),
        pl.BlockSpec((CONV_WIDTH, LRU_WIDTH), const2),
        pl.BlockSpec((1, LRU_WIDTH), const2),
        pl.BlockSpec((1, LRU_WIDTH, 2 * LRU_WIDTH), lambda b, s: (d, 0, 0)),
        pl.BlockSpec((1, 1, 2 * LRU_WIDTH), lambda b, s: (d, 0, 0)),
        pl.BlockSpec((1, 1, LRU_WIDTH), lambda b, s: (d, 0, 0)),
    ]
    args = [z, z, z, conv_w, conv_b.reshape(1, LRU_WIDTH), wg, gb, lam.reshape(2, 1, LRU_WIDTH)]
    if reverse:
        in_specs += [pl.BlockSpec((LRU_TC, LRU_WIDTH), cur),
                     pl.BlockSpec((LRU_TC, LRU_WIDTH), lambda b, s: (b * n_t + tt(s), ga_col))]
        args += [h_fwd, z]
        out_dtype = jnp.bfloat16
    else:
        out_dtype = jnp.float32
    return pl.pallas_call(
        functools.partial(_lru_kernel, reverse=reverse, n_t=n_t),
        grid=(batch, n_t),
        in_specs=in_specs,
        out_specs=pl.BlockSpec((LRU_TC, LRU_WIDTH), cur),
        out_shape=jax.ShapeDtypeStruct((batch * seq, LRU_WIDTH), out_dtype),
        scratch_shapes=[pltpu.VMEM((8, LRU_WIDTH), jnp.float32)],
        compiler_params=_cparams("arbitrary", "arbitrary"),
        name="lru_bwd" if reverse else "lru_fwd",
    )(*args)


def lru_gate_dense(gate_w, gate_b):
    eye = jnp.eye(LRU_BLOCKS, dtype=gate_w.dtype)
    dense = jnp.einsum('dgnij,nm->dgnimj', gate_w, eye).reshape(2, 2, LRU_WIDTH, LRU_WIDTH)
    wg = jnp.concatenate([dense[:, 0], dense[:, 1]], axis=-1).astype(jnp.bfloat16)
    gb = jnp.concatenate([gate_b[:, 0], gate_b[:, 1]], axis=-1).reshape(2, 1, 2 * LRU_WIDTH)
    return wg, gb


MLA_TM = 512
MLA_TQ = 512
MLA_TK = 512
MLA_DP = 128


def _mla_proj_kernel(z_ref, qn_ref, kn_ref, wqa_ref, wqb_ref, wk_ref, wv_ref, ea_ref, eb_ref,
                     cos_ref, sin_ref, q_ref, k_ref, v_ref):
    z = z_ref[...]
    cq = z[:, :MLA_Q_RANK].astype(jnp.float32)
    ckv = z[:, MLA_Q_RANK:MLA_Q_RANK + MLA_KV_RANK].astype(jnp.float32)
    kr = z[:, MLA_Q_RANK + MLA_KV_RANK:]
    cqn = (cq * lax.rsqrt(jnp.mean(cq * cq, axis=-1, keepdims=True) + EPS) * qn_ref[...]).astype(jnp.bfloat16)
    ckn = (ckv * lax.rsqrt(jnp.mean(ckv * ckv, axis=-1, keepdims=True) + EPS) * kn_ref[...]).astype(jnp.bfloat16)
    cos = cos_ref[...]
    sin = sin_ref[...]
    f32 = jnp.float32
    k_rope = (jnp.dot(kr, ea_ref[...], preferred_element_type=f32) * cos
              + jnp.dot(kr, eb_ref[...], preferred_element_type=f32) * sin)
    scale = (MLA_NOPE + MLA_ROPE) ** -0.5 * math.log2(math.e)
    row = lax.broadcasted_iota(jnp.int32, (MLA_DP, MLA_TM), 0)
    for h in range(MLA_HEADS):
        qa = jnp.dot(cqn, wqa_ref[h], preferred_element_type=f32)
        qb = jnp.dot(cqn, wqb_ref[h], preferred_element_type=f32)
        q_ref[0, h] = ((qa * cos + qb * sin) * scale).astype(q_ref.dtype)
        k_ref[0, h] = (jnp.dot(ckn, wk_ref[h], preferred_element_type=f32) + k_rope).astype(k_ref.dtype)
        vt = lax.dot_general(wv_ref[h], ckn, (((1,), (1,)), ((), ())), preferred_element_type=f32)
        vt = jnp.where(row == _mla_ones_row(h), 1.0, vt)
        for c in range(MLA_TM // MLA_TK):
            v_ref[0, h, c] = vt[:, c * MLA_TK:(c + 1) * MLA_TK].astype(v_ref.dtype)


def _mla_ones_row(head):
    return MLA_V if head % 2 == 0 else 0


def _rot_half_matrix(n):
    half = n // 2
    r = np.zeros((n, n), np.float32)
    for j in range(half):
        r[half + j, j] = -1.0
        r[j, half + j] = 1.0
    return r


def mla_weights(w_uq, w_ukv):
    rot = jnp.asarray(_rot_half_matrix(MLA_ROPE))
    wq = jnp.transpose(w_uq, (1, 0, 2))
    pad = lambda a, lo, hi: jnp.pad(a, ((0, 0), (0, 0), (lo, hi)))
    wqa = pad(wq, 0, MLA_DP - MLA_NOPE - MLA_ROPE)
    wqb = pad(jnp.einsum('hrd,de->hre', wq[..., MLA_NOPE:], rot), MLA_NOPE, MLA_DP - MLA_NOPE - MLA_ROPE)
    wkv = jnp.transpose(w_ukv, (1, 0, 2))
    wk = pad(wkv[..., :MLA_NOPE], 0, MLA_DP - MLA_NOPE)
    wv_even = pad(wkv[..., MLA_NOPE:], 0, MLA_V)
    wv_odd = pad(wkv[..., MLA_NOPE:], MLA_V, 0)
    wv = jnp.where((jnp.arange(MLA_HEADS) % 2 == 0)[:, None, None], wv_even, wv_odd)
    wv = jnp.transpose(wv, (0, 2, 1))
    ea = np.zeros((LANES, MLA_DP), np.float32)
    for j in range(MLA_ROPE):
        ea[j, MLA_NOPE + j] = 1.0
    eb = np.zeros((LANES, MLA_DP), np.float32)
    eb[:MLA_ROPE, MLA_NOPE:MLA_NOPE + MLA_ROPE] = _rot_half_matrix(MLA_ROPE)
    bf = jnp.bfloat16
    return (wqa.astype(bf), wqb.astype(bf), wk.astype(bf), wv.astype(bf),
            jnp.asarray(ea, bf), jnp.asarray(eb, bf))


def mla_rope_tables(seq):
    half = MLA_ROPE // 2
    inv = ROPE_BASE ** (-jnp.arange(half, dtype=jnp.float32) / half)
    ang = jnp.arange(seq, dtype=jnp.float32)[:, None] * inv[None, :]
    ones = jnp.ones((seq, MLA_NOPE), jnp.float32)
    zeros = jnp.zeros((seq, MLA_DP - MLA_NOPE - MLA_ROPE), jnp.float32)
    cos = jnp.concatenate([ones, jnp.cos(ang), jnp.cos(ang), zeros], axis=1)
    sin = jnp.concatenate([0.0 * ones, jnp.sin(ang), jnp.sin(ang), zeros], axis=1)
    return cos, sin


def mla_project(z, q_norm, kv_norm, weights, tables, batch, seq):
    wqa, wqb, wk, wv, ea, eb = weights
    cos, sin = tables
    n_t = seq // MLA_TM
    hshape = (batch, MLA_HEADS, seq, MLA_DP)
    c3 = lambda b, t: (0, 0, 0)
    c2 = lambda b, t: (0, 0)
    hspec = pl.BlockSpec((1, MLA_HEADS, MLA_TM, MLA_DP), lambda b, t: (b, 0, t, 0))
    return pl.pallas_call(
        _mla_proj_kernel,
        grid=(batch, n_t),
        in_specs=[
            pl.BlockSpec((MLA_TM, 4 * LANES), lambda b, t: (b * n_t + t, CB_MLA // 4)),
            pl.BlockSpec((1, MLA_Q_RANK), c2),
            pl.BlockSpec((1, MLA_KV_RANK), c2),
            pl.BlockSpec((MLA_HEADS, MLA_Q_RANK, MLA_DP), c3),
            pl.BlockSpec((MLA_HEADS, MLA_Q_RANK, MLA_DP), c3),
            pl.BlockSpec((MLA_HEADS, MLA_KV_RANK, MLA_DP), c3),
            pl.BlockSpec((MLA_HEADS, MLA_KV_RANK, MLA_DP), c3),
            pl.BlockSpec((LANES, MLA_DP), c2),
            pl.BlockSpec((LANES, MLA_DP), c2),
            pl.BlockSpec((MLA_TM, MLA_DP), lambda b, t: (t, 0)),
            pl.BlockSpec((MLA_TM, MLA_DP), lambda b, t: (t, 0)),
        ],
        out_specs=[hspec, hspec,
                   pl.BlockSpec((1, MLA_HEADS, MLA_TM // MLA_TK, MLA_DP, MLA_TK), lambda b, t: (b, 0, t, 0, 0))],
        out_shape=[jax.ShapeDtypeStruct(hshape, jnp.bfloat16)] * 2
        + [jax.ShapeDtypeStruct((batch, MLA_HEADS, seq // MLA_TK, MLA_DP, MLA_TK), jnp.bfloat16)],
        compiler_params=_cparams("arbitrary", "arbitrary"),
        name="mla_project",
    )(z, q_norm.reshape(1, -1), kv_norm.reshape(1, -1), wqa, wqb, wk, wv, ea, eb, cos, sin)


def _mla_attn_kernel(q_ref, k_ref, vt_ref, o_ref, acc_ref, *, n_kv):
    acc_ref[...] = jnp.zeros_like(acc_ref)
    f32 = jnp.float32
    nt = (((1,), (1,)), ((), ()))

    def body(j, carry):
        rows = pl.ds(pl.multiple_of(j * MLA_TK, MLA_TK), MLA_TK)
        new = []
        for hh in range(2):
            m_prev = carry[hh]
            s = lax.dot_general(k_ref[0, hh, rows, :], q_ref[0, hh], nt, preferred_element_type=f32)
            m_next = jnp.maximum(m_prev, jnp.max(s, axis=0, keepdims=True))
            p = jnp.exp2(s - m_next)
            alpha = jnp.exp2(m_prev - m_next)
            new.append(m_next)
            acc_ref[hh] = alpha * acc_ref[hh] + jnp.dot(vt_ref[0, hh, j], p.astype(jnp.bfloat16),
                                                         preferred_element_type=f32)
        return tuple(new)

    m0 = jnp.full((1, MLA_TQ), NEG_INF, f32)
    lax.fori_loop(0, n_kv, body, (m0, m0), unroll=min(4, n_kv))
    acc_a, acc_b = acc_ref[0], acc_ref[1]
    l_a = acc_a[_mla_ones_row(0):_mla_ones_row(0) + 1, :]
    l_b = acc_b[_mla_ones_row(1):_mla_ones_row(1) + 1, :]
    row = lax.broadcasted_iota(jnp.int32, acc_a.shape, 0)
    out_t = jnp.where(row < MLA_V, acc_a / l_a, acc_b / l_b)
    o_ref[...] = out_t.T.astype(o_ref.dtype)


def mla_attention(q, k, vt, batch, seq):
    n_q = seq // MLA_TQ
    n_kv = seq // MLA_TK
    qspec = pl.BlockSpec((1, 2, MLA_TQ, MLA_DP), lambda b, hp, i: (b, hp, i, 0))
    kspec = pl.BlockSpec((1, 2, seq, MLA_DP), lambda b, hp, i: (b, hp, 0, 0))
    vspec = pl.BlockSpec((1, 2, n_kv, MLA_DP, MLA_TK), lambda b, hp, i: (b, hp, 0, 0, 0))
    return pl.pallas_call(
        functools.partial(_mla_attn_kernel, n_kv=n_kv),
        grid=(batch, MLA_HEADS // 2, n_q),
        in_specs=[qspec, kspec, vspec],
        out_specs=pl.BlockSpec((MLA_TQ, LANES), lambda b, hp, i: (b * n_q + i, hp)),
        out_shape=jax.ShapeDtypeStruct((batch * seq, MLA_HEADS * MLA_V), jnp.bfloat16),
        scratch_shapes=[pltpu.VMEM((2, MLA_DP, MLA_TQ), jnp.float32)],
        compiler_params=_cparams("arbitrary", "arbitrary", "arbitrary"),
        name="mla_attention",
    )(q, k, vt)


BAND_Q = 128
BAND_W = 256
BAND_RADIUS = 64
BAND_ROWS_PER_STEP = 2048


def band_bias_table(group, dil):
    n = DIL_HEADS
    slopes = np.asarray([2.0 ** (-8.0 * (h + 1) / n) for h in range(n)], np.float32)
    slopes = slopes[group * DIL_HEADS_PER_GROUP:(group + 1) * DIL_HEADS_PER_GROUP]
    iq = np.arange(BAND_Q)[:, None]
    ik = np.arange(BAND_W)[None, :]
    tabs = []
    for d in range(3):
        dist = np.abs(d * BAND_RADIUS + iq - ik)
        bias = -slopes[:, None, None] * (dil * dist).astype(np.float32)[None]
        tabs.append(np.where((dist <= BAND_RADIUS)[None], bias, np.float32(NEG_INF)))
    return jnp.asarray(np.stack(tabs).astype(np.float32))


def _band_kernel(q_ref, k_ref, v_ref, bias_ref, o_ref, lse_ref, *, length):
    lane = lax.broadcasted_iota(jnp.int32, (1, LANES), 1)
    first = lane < DIL_HEAD_DIM
    scale = DIL_HEAD_DIM ** -0.5

    n_blocks = length // BAND_Q
    n_res = q_ref.shape[1]

    def body(idx, carry):
        rr = idx // n_blocks
        q0 = pl.multiple_of((idx % n_blocks) * BAND_Q, BAND_Q)
        start = pl.multiple_of(jnp.clip(q0 - BAND_RADIUS, 0, length - BAND_W), BAND_RADIUS)
        didx = (q0 - start) // BAND_RADIUS
        q = q_ref[0, rr, pl.ds(q0, BAND_Q), :]
        kw = k_ref[0, rr, pl.ds(start, BAND_W), :]
        vw = v_ref[0, rr, pl.ds(start, BAND_W), :]
        outs, lses = [], []
        for hh in range(2):
            sel = first if hh == 0 else jnp.logical_not(first)
            qh = jnp.where(sel, q, jnp.zeros_like(q))
            s = lax.dot_general(qh, kw, (((1,), (1,)), ((), ())), preferred_element_type=jnp.float32)
            s = s * scale + bias_ref[didx, hh]
            m = jnp.max(s, axis=1, keepdims=True)
            e = jnp.exp(s - m)
            den = jnp.sum(e, axis=1, keepdims=True)
            p = (e / den).astype(jnp.bfloat16)
            outs.append(jnp.dot(p, vw, preferred_element_type=jnp.float32))
            lses.append(m + jnp.log(den))
        o_ref[0, rr, pl.ds(q0, BAND_Q), :] = jnp.where(first, outs[0], outs[1]).astype(o_ref.dtype)
        lse_ref[0, rr, pl.ds(q0, BAND_Q), :] = jnp.where(first, lses[0], lses[1])
        return carry

    lax.fori_loop(0, n_res * n_blocks, body, 0, unroll=min(16, n_res * n_blocks))


def band_group(src, group, dil, batch, seq, col_blocks):
    length = seq // dil
    pairs = DIL_HEADS_PER_GROUP // 2
    n_res = max(1, min(dil, BAND_ROWS_PER_STEP // length))

    def zspec(cb):
        return pl.BlockSpec((1, n_res, length, LANES), lambda b, r, hp: (b, r, 0, cb + hp))

    ospec = pl.BlockSpec((1, n_res, length, LANES), lambda b, r, hp: (b, r, 0, hp))
    oshape = (batch, dil, length, DIL_GROUP_WIDTH)
    return pl.pallas_call(
        functools.partial(_band_kernel, length=length),
        grid=(batch, dil // n_res, pairs),
        in_specs=[zspec(col_blocks[0]), zspec(col_blocks[1]), zspec(col_blocks[2]),
                  pl.BlockSpec((3, 2, BAND_Q, BAND_W), lambda b, r, hp: (0, hp, 0, 0))],
        out_specs=[ospec, ospec],
        out_shape=[jax.ShapeDtypeStruct(oshape, jnp.bfloat16), jax.ShapeDtypeStruct(oshape, jnp.float32)],
        compiler_params=_cparams("arbitrary", "arbitrary", "arbitrary"),
        name=f"band_attention_g{group}",
    )(src, src, src, band_bias_table(group, dil))


RET_C = 256


def _ret_kernel(lg_ref, q_ref, k_ref, v_ref, g_ref, cos_ref, sin_ref, o_ref, qs_ref, ks_ref, o1_ref, *, n_c):
    h = pl.program_id(1)
    lgf = lg_ref[0, h]
    lgb = lg_ref[1, h]
    c = RET_C
    f32 = jnp.float32
    bf = jnp.bfloat16
    ii = lax.broadcasted_iota(jnp.int32, (c, c), 0)
    jj = lax.broadcasted_iota(jnp.int32, (c, c), 1)
    diff = (ii - jj).astype(f32)
    decay = jnp.where(diff >= 0.0, jnp.exp(lgf * jnp.maximum(diff, 0.0)), jnp.exp(lgb * jnp.maximum(-diff, 0.0)))
    idx = lax.broadcasted_iota(jnp.int32, (c, 1), 0).astype(f32)
    xi_f = jnp.exp(lgf * (idx + 1.0))
    zeta_f = jnp.exp(lgf * (c - 1.0 - idx))
    xi_b = jnp.exp(lgb * (c - idx))
    zeta_b = jnp.exp(lgb * idx)
    cd_f = jnp.exp(lgf * c)
    cd_b = jnp.exp(lgb * c)
    kscale = RET_QK ** -0.5

    def rope(x, rows):
        return x * cos_ref[rows, :] + pltpu.roll(x, RET_QK // 2, 1) * sin_ref[rows, :]

    def fwd(n, state):
        rows = pl.ds(pl.multiple_of(n * c, c), c)
        q = rope(q_ref[0, rows, :].astype(f32), rows)
        k = rope(k_ref[0, rows, :].astype(f32), rows) * kscale
        v = v_ref[0, rows, :]
        qs_ref[rows, :] = q
        ks_ref[rows, :] = k
        qb = q.astype(bf)
        s = lax.dot_general(qb, k.astype(bf), (((1,), (1,)), ((), ())), preferred_element_type=f32) * decay
        o = jnp.dot(s.astype(bf), v, preferred_element_type=f32)
        o = o + xi_f * jnp.dot(qb, state.astype(bf), preferred_element_type=f32)
        o1_ref[rows, :] = o
        kz = (k * zeta_f).T.astype(bf)
        return cd_f * state + jnp.dot(kz, v, preferred_element_type=f32)

    lax.fori_loop(0, n_c, fwd, jnp.zeros((RET_QK, RET_V), f32), unroll=2)

    def bwd(step, state):
        n = n_c - 1 - step
        rows = pl.ds(pl.multiple_of(n * c, c), c)
        q = qs_ref[rows, :]
        k = ks_ref[rows, :]
        v = v_ref[0, rows, :]
        of = o1_ref[rows, :] + xi_b * jnp.dot(q.astype(bf), state.astype(bf), preferred_element_type=f32)
        mu = jnp.mean(of, axis=-1, keepdims=True)
        var = jnp.mean(jnp.square(of - mu), axis=-1, keepdims=True)
        of = (of - mu) * lax.rsqrt(var + EPS)
        g = g_ref[0, rows, :].astype(f32)
        o_ref[0, rows, :] = (g * jax.nn.sigmoid(g) * of).astype(o_ref.dtype)
        kz = (k * zeta_b).T.astype(bf)
        return cd_b * state + jnp.dot(kz, v, preferred_element_type=f32)

    lax.fori_loop(0, n_c, bwd, jnp.zeros((RET_QK, RET_V), f32), unroll=2)


def ret_rope_tables(seq):
    half = RET_QK // 2
    inv = ROPE_BASE ** (-jnp.arange(half, dtype=jnp.float32) / half)
    ang = jnp.arange(seq, dtype=jnp.float32)[:, None] * inv[None, :]
    cos = jnp.concatenate([jnp.cos(ang), jnp.cos(ang)], axis=1)
    sin = jnp.concatenate([-jnp.sin(ang), jnp.sin(ang)], axis=1)
    return cos, sin


def retention(z, ret_decay, tables, batch, seq):
    zv = z.reshape(batch, seq, Z_COLS)
    log_gamma = jax.nn.log_sigmoid(ret_decay.astype(jnp.float32))
    cos, sin = tables

    def zspec(cb):
        return pl.BlockSpec((1, seq, LANES), lambda b, h: (b, 0, cb + h))

    tspec = pl.BlockSpec((seq, RET_QK), lambda b, h: (0, 0))
    out = pl.pallas_call(
        functools.partial(_ret_kernel, n_c=seq // RET_C),
        grid=(batch, RET_HEADS),
        in_specs=[pl.BlockSpec(memory_space=pltpu.SMEM),
                  zspec(CB_RQ), zspec(CB_RK), zspec(CB_RV), zspec(CB_RG), tspec, tspec],
        out_specs=pl.BlockSpec((1, seq, LANES), lambda b, h: (b, 0, h)),
        out_shape=jax.ShapeDtypeStruct((batch, seq, RET_HEADS * RET_V), jnp.bfloat16),
        scratch_shapes=[pltpu.VMEM((seq, RET_QK), jnp.float32)] * 3,
        compiler_params=_cparams("arbitrary", "arbitrary"),
        name="retention",
    )(log_gamma, zv, zv, zv, zv, cos, sin)
    return out.reshape(batch * seq, RET_HEADS * RET_V)


MERGE_TM = 512


def _merge_kernel(x_ref, ya_ref, yb_ref, o0_ref, o1_ref, o2_ref, l0_ref, l1_ref, l2_ref, yd_ref,
                  zg_ref, wb_ref, wo_ref, out_ref, tok_ref):
    f32 = jnp.float32

    def token_order(slot, src_ref):
        dil, rows, width = src_ref.shape[1:]
        tiles = width // LANES
        for r in range(dil):
            v = src_ref[0, r].astype(f32)
            for c in range(tiles):
                tok_ref[slot * tiles + c, pl.ds(r, rows, stride=dil), :] = v[:, c * LANES:(c + 1) * LANES]
        return jnp.concatenate([tok_ref[slot * tiles + c] for c in range(tiles)], axis=1)

    l0 = l0_ref[0, 0]
    l1, l2 = token_order(0, l1_ref), token_order(1, l2_ref)
    o1, o2 = token_order(2, o1_ref), token_order(3, o2_ref)
    m = jnp.maximum(jnp.maximum(l0, l1), l2)
    e0, e1, e2 = jnp.exp(l0 - m), jnp.exp(l1 - m), jnp.exp(l2 - m)
    inv = 1.0 / (e0 + e1 + e2)
    yc = ((e0 * inv) * o0_ref[0, 0].astype(f32) + (e1 * inv) * o1 + (e2 * inv) * o2).astype(jnp.bfloat16)
    merged = None
    for i, y in enumerate((ya_ref[...], yb_ref[...], yc, yd_ref[...])):
        gate = jax.nn.sigmoid(zg_ref[:, i * D_MODEL:(i + 1) * D_MODEL].astype(f32))
        term = gate * jnp.dot(y, wb_ref[i], preferred_element_type=f32)
        merged = term if merged is None else merged + term
    out_ref[...] = x_ref[...] + jnp.dot(merged.astype(jnp.bfloat16), wo_ref[...], preferred_element_type=f32)


def merge_project(x, ya, yb, dil_o, dil_lse, yd, z, w_branch, w_out, seq):
    n = x.shape[0]
    tm = MERGE_TM
    n_t = seq // tm
    width = 4 * LANES
    row = lambda i: (i, 0)
    bspec = pl.BlockSpec((tm, width), row)

    def dspec(dil):
        return pl.BlockSpec((1, dil, tm // dil, width), lambda i: (i // n_t, 0, i % n_t, 0))

    dspecs = [dspec(d) for _, d in DIL_PAIRS]
    return pl.pallas_call(
        _merge_kernel,
        grid=(n // tm,),
        in_specs=[pl.BlockSpec((tm, D_MODEL), row), bspec, bspec, *dspecs, *dspecs, bspec,
                  pl.BlockSpec((tm, N_BRANCHES * D_MODEL), lambda i: (i, CB_ZG * LANES // (N_BRANCHES * D_MODEL))),
                  pl.BlockSpec((N_BRANCHES, width, D_MODEL), lambda i: (0, 0, 0)),
                  pl.BlockSpec((D_MODEL, D_MODEL), lambda i: (0, 0))],
        out_specs=pl.BlockSpec((tm, D_MODEL), row),
        out_shape=jax.ShapeDtypeStruct((n, D_MODEL), jnp.float32),
        scratch_shapes=[pltpu.VMEM((4 * width // LANES, tm, LANES), jnp.float32)],
        compiler_params=_cparams("arbitrary"),
        name="merge_project",
    )(x, ya, yb, *dil_o, *dil_lse, yd, z, w_branch, w_out)


MOE_TM = 512
MOE_TT = 256
MOE_W_SMALL = 64
MOE_ROWS = 1024
MOE_ALIGN = 8
MOE_XW = D_MODEL + LANES


def _router_kernel(x_ref, g_ref, w_ref, h_ref, aff_ref):
    x = x_ref[...]
    y = x * lax.rsqrt(jnp.mean(x * x, axis=-1, keepdims=True) + EPS)
    h = (y * g_ref[...]).astype(jnp.bfloat16)
    h_ref[...] = h
    logits = jnp.dot(h, w_ref[...], preferred_element_type=jnp.float32)
    lane = lax.broadcasted_iota(jnp.int32, logits.shape, 1)
    logits = jnp.where(lane < N_EXPERTS, logits, NEG_INF)
    e = jnp.exp(logits - jnp.max(logits, axis=-1, keepdims=True))
    aff_t = (e / jnp.sum(e, axis=-1, keepdims=True)).T
    for c in range(x.shape[0] // MOE_TT):
        aff_ref[c] = aff_t[:N_EXPERTS, c * MOE_TT:(c + 1) * MOE_TT]


def moe_router(x, norm_g, w_router):
    n, d = x.shape
    tm = 2 * MOE_TT
    w = jnp.pad(w_router.astype(jnp.bfloat16), ((0, 0), (0, LANES - N_EXPERTS)))
    return pl.pallas_call(
        _router_kernel,
        grid=(n // tm,),
        in_specs=[pl.BlockSpec((tm, d), lambda i: (i, 0)), pl.BlockSpec((1, d), lambda i: (0, 0)),
                  pl.BlockSpec((d, LANES), lambda i: (0, 0))],
        out_specs=[pl.BlockSpec((tm, d), lambda i: (i, 0)),
                   pl.BlockSpec((tm // MOE_TT, N_EXPERTS, MOE_TT), lambda i: (i, 0, 0))],
        out_shape=[jax.ShapeDtypeStruct((n, d), jnp.bfloat16),
                   jax.ShapeDtypeStruct((n // MOE_TT, N_EXPERTS, MOE_TT), jnp.float32)],
        compiler_params=_cparams("arbitrary"),
        name="moe_router",
    )(x, norm_g.reshape(1, d), w)


def _affinity_bits(a):
    return lax.bitcast_convert_type(a, jnp.int32)


def _threshold_kernel(aff_ref, thr_ref, need_ref, *, cap, n_tiles):
    def count(pred, thr):
        def tile(c, acc):
            return acc + jnp.where(pred(_affinity_bits(aff_ref[c]), thr), 1.0, 0.0)
        acc = lax.fori_loop(0, n_tiles, tile, jnp.zeros((N_EXPERTS, MOE_TT), jnp.float32), unroll=8)
        return jnp.sum(acc, axis=1, keepdims=True)

    def bit(i, thr):
        cand = thr | jnp.left_shift(jnp.int32(1), 30 - i)
        return jnp.where(count(lambda b, t: b >= t, cand) >= cap, cand, thr)

    thr = lax.fori_loop(0, 31, bit, jnp.zeros((N_EXPERTS, 1), jnp.int32))
    thr_ref[...] = jnp.broadcast_to(thr, thr_ref.shape)
    need_ref[...] = jnp.broadcast_to(cap - count(lambda b, t: b > t, thr), need_ref.shape)


def moe_threshold(aff, cap):
    n_tiles = aff.shape[0]
    full = pl.BlockSpec((N_EXPERTS, LANES), lambda i: (0, 0))
    return pl.pallas_call(
        functools.partial(_threshold_kernel, cap=cap, n_tiles=n_tiles),
        grid=(1,),
        in_specs=[pl.BlockSpec(aff.shape, lambda i: (0, 0, 0))],
        out_specs=[full, full],
        out_shape=[jax.ShapeDtypeStruct((N_EXPERTS, LANES), jnp.int32),
                   jax.ShapeDtypeStruct((N_EXPERTS, LANES), jnp.float32)],
        compiler_params=_cparams("arbitrary"),
        name="moe_threshold",
    )(aff)


def _assign_kernel(aff_ref, thr_ref, need_ref, codet_ref, coden_ref, start_ref, total_ref, ties_ref, run_ref):
    @pl.when(pl.program_id(0) == 0)
    def _():
        ties_ref[...] = jnp.zeros_like(ties_ref)
        run_ref[...] = jnp.zeros_like(run_ref)

    f32 = jnp.float32
    bits = _affinity_bits(aff_ref[0])
    thr = thr_ref[:, :1]
    ii = lax.broadcasted_iota(jnp.int32, (MOE_TT, MOE_TT), 0)
    jj = lax.broadcasted_iota(jnp.int32, (MOE_TT, MOE_TT), 1)
    tri = jnp.where(ii <= jj, 1.0, 0.0).astype(jnp.bfloat16)
    eq = bits == thr
    eq_f = jnp.where(eq, 1.0, 0.0)
    tie_rank = jnp.dot(eq_f.astype(jnp.bfloat16), tri, preferred_element_type=f32) + ties_ref[:, :1]
    sel = jnp.logical_or(bits > thr, jnp.logical_and(eq, tie_rank <= need_ref[:, :1]))
    sel_f = jnp.where(sel, 1.0, 0.0)
    incl = jnp.dot(sel_f.astype(jnp.bfloat16), tri, preferred_element_type=f32)
    code = jnp.where(sel, incl - 1.0, -1.0)
    codet_ref[0] = code.astype(jnp.int32)
    padded = jnp.concatenate([code, jnp.zeros((LANES - N_EXPERTS, MOE_TT), f32)], axis=0)
    coden_ref[...] = padded.T.astype(jnp.int32)
    start_ref[0] = run_ref[...].astype(jnp.int32)
    taken = jnp.sum(sel_f, axis=1, keepdims=True)
    run_ref[...] = run_ref[...] + jnp.floor((taken + (MOE_ALIGN - 1)) * (1.0 / MOE_ALIGN)) * MOE_ALIGN
    total_ref[...] = run_ref[...].astype(jnp.int32)
    ties_ref[...] = ties_ref[...] + jnp.sum(eq_f, axis=1, keepdims=True)


def moe_assign(aff, thr, need):
    n_tiles = aff.shape[0]
    tile = pl.BlockSpec((1, N_EXPERTS, MOE_TT), lambda c: (c, 0, 0))
    full = pl.BlockSpec((N_EXPERTS, LANES), lambda c: (0, 0))
    return pl.pallas_call(
        _assign_kernel,
        grid=(n_tiles,),
        in_specs=[tile, full, full],
        out_specs=[tile, pl.BlockSpec((MOE_TT, LANES), lambda c: (c, 0)),
                   pl.BlockSpec((1, N_EXPERTS, LANES), lambda c: (c, 0, 0)), full],
        out_shape=[jax.ShapeDtypeStruct((n_tiles, N_EXPERTS, MOE_TT), jnp.int32),
                   jax.ShapeDtypeStruct((n_tiles * MOE_TT, LANES), jnp.int32),
                   jax.ShapeDtypeStruct((n_tiles, N_EXPERTS, LANES), jnp.int32),
                   jax.ShapeDtypeStruct((N_EXPERTS, LANES), jnp.int32)],
        scratch_shapes=[pltpu.VMEM((N_EXPERTS, LANES), jnp.float32)] * 2,
        compiler_params=_cparams("arbitrary"),
        name="moe_assign",
    )(aff, thr, need)


def _dispatch_kernel(start_ref, big_ref, h_ref, code_ref, aff_ref, init_hbm, x_hbm, buf_ref, sem, *, n_tiles):
    del init_hbm
    c = pl.program_id(0)
    cur = c % 2
    f32 = jnp.float32
    h = h_ref[...]
    code = code_ref[0]
    gate = aff_ref[0]

    def copies(tile, slot, first, width, group):
        return [pltpu.make_async_copy(
            buf_ref.at[slot, pl.ds(k * width, width)],
            x_hbm.at[first + k, pl.ds(_slot_start(start_ref, tile, first + k), width)],
            sem.at[slot, k]) for k in range(group)]

    def stage(first, width, group):
        slot = lax.broadcasted_iota(jnp.int32, (width, MOE_TT), 0)
        onehots = []
        for k in range(group):
            e = first + k
            hit = slot == code[e:e + 1, :]
            onehots.append(jnp.where(hit, 1.0, 0.0).astype(jnp.bfloat16))
            gwin = jnp.sum(jnp.where(hit, gate[e:e + 1, :], 0.0), axis=1, keepdims=True)
            buf_ref[cur, k * width:(k + 1) * width, D_MODEL:] = jnp.broadcast_to(gwin, (width, LANES))
        buf_ref[cur, :, :D_MODEL] = jnp.dot(jnp.concatenate(onehots, axis=0), h, preferred_element_type=f32)

    def wait_previous_small():
        prev = jnp.maximum(c - 1, 0)

        @pl.when(jnp.logical_and(c > 0, big_ref[prev] == 0))
        def _():
            for cp in copies(prev, 1 - cur, 0, MOE_W_SMALL, N_EXPERTS):
                cp.wait()

    @pl.when(big_ref[c] == 0)
    def _():
        stage(0, MOE_W_SMALL, N_EXPERTS)
        wait_previous_small()
        mine = copies(c, cur, 0, MOE_W_SMALL, N_EXPERTS)
        for cp in mine:
            cp.start()

        @pl.when(c == n_tiles - 1)
        def _():
            for cp in mine:
                cp.wait()

    @pl.when(big_ref[c] != 0)
    def _():
        wait_previous_small()
        group = MOE_ROWS // MOE_TT
        for first in range(0, N_EXPERTS, group):
            stage(first, MOE_TT, group)
            mine = copies(c, cur, first, MOE_TT, group)
            for cp in mine:
                cp.start()
            for cp in mine:
                cp.wait()


def _slot_start(start_ref, tile, expert):
    return pl.multiple_of(start_ref[tile * N_EXPERTS + expert], MOE_ALIGN)


def moe_dispatch(starts, big, h, code_t, aff, slots):
    n, d = h.shape
    n_tiles = n // MOE_TT
    tile = pl.BlockSpec((1, N_EXPERTS, MOE_TT), lambda c, s, b: (c, 0, 0))
    shape = (N_EXPERTS, slots, MOE_XW)
    return pl.pallas_call(
        functools.partial(_dispatch_kernel, n_tiles=n_tiles),
        grid_spec=pltpu.PrefetchScalarGridSpec(
            num_scalar_prefetch=2,
            grid=(n_tiles,),
            in_specs=[pl.BlockSpec((MOE_TT, d), lambda c, s, b: (c, 0)), tile, tile,
                      pl.BlockSpec(memory_space=pl.ANY)],
            out_specs=pl.BlockSpec(memory_space=pl.ANY),
            scratch_shapes=[pltpu.VMEM((2, MOE_ROWS, MOE_XW), jnp.float32),
                            pltpu.SemaphoreType.DMA((2, MOE_ROWS // MOE_W_SMALL))],
        ),
        out_shape=jax.ShapeDtypeStruct(shape, jnp.float32),
        input_output_aliases={5: 0},
        compiler_params=_cparams("arbitrary"),
        name="moe_dispatch",
    )(starts, big, h, code_t, aff, jnp.zeros(shape, jnp.float32))


def _expert_kernel(total_ref, x_ref, wg_ref, wu_ref, wd_ref, o_ref):
    f32 = jnp.float32
    used = pl.program_id(1) * MOE_TM < total_ref[pl.program_id(0)]

    @pl.when(used)
    def _():
        x = x_ref[0, :, :D_MODEL].astype(jnp.bfloat16)
        a = jnp.dot(x, wg_ref[0], preferred_element_type=f32)
        u = jnp.dot(x, wu_ref[0], preferred_element_type=f32)
        he = (a * jax.nn.sigmoid(a) * u).astype(jnp.bfloat16)
        o_ref[0] = jnp.dot(he, wd_ref[0], preferred_element_type=f32) * x_ref[0, :, D_MODEL:D_MODEL + 1]

    @pl.when(jnp.logical_not(used))
    def _():
        o_ref[0] = jnp.zeros_like(o_ref[0])


def expert_ffn(total, xd, w_gate, w_up, w_down):
    e, slots, _ = xd.shape
    d = D_MODEL
    tm = MOE_TM
    wspec = pl.BlockSpec((1, d, D_EXPERT), lambda i, j, t: (i, 0, 0))
    return pl.pallas_call(
        _expert_kernel,
        grid_spec=pltpu.PrefetchScalarGridSpec(
            num_scalar_prefetch=1,
            grid=(e, slots // tm),
            in_specs=[pl.BlockSpec((1, tm, MOE_XW), lambda i, j, t: (i, j, 0)),
                      wspec, wspec,
                      pl.BlockSpec((1, D_EXPERT, d), lambda i, j, t: (i, 0, 0))],
            out_specs=pl.BlockSpec((1, tm, d), lambda i, j, t: (i, j, 0)),
        ),
        out_shape=jax.ShapeDtypeStruct((e, slots, d), jnp.float32),
        compiler_params=_cparams("arbitrary", "arbitrary"),
        name="expert_ffn",
    )(total, xd, w_gate, w_up, w_down)


def _combine_kernel(start_ref, big_ref, x_ref, code_ref, y_hbm, o_ref, buf_ref, sem, *, n_tiles):
    c = pl.program_id(0)
    cur = c % 2
    f32 = jnp.float32
    bf = jnp.bfloat16
    code = code_ref[...]
    o_ref[...] = x_ref[...]

    def copies(tile, slot, first, width, group):
        return [pltpu.make_async_copy(
            y_hbm.at[first + k, pl.ds(_slot_start(start_ref, tile, first + k), width)],
            buf_ref.at[slot, pl.ds(k * width, width)],
            sem.at[slot, k]) for k in range(group)]

    def onehot(first, width, group):
        pieces = []
        if width < LANES:
            lane = lax.broadcasted_iota(jnp.int32, (MOE_TT, LANES), 1)
            for k in range(0, group, 2):
                e = first + k
                target = jnp.where(lane < width, code[:, e:e + 1], code[:, e + 1:e + 2] + width)
                pieces.append(jnp.where(target == lane, 1.0, 0.0).astype(bf))
        else:
            lane = lax.broadcasted_iota(jnp.int32, (MOE_TT, width), 1)
            for k in range(group):
                e = first + k
                pieces.append(jnp.where(code[:, e:e + 1] == lane, 1.0, 0.0).astype(bf))
        return jnp.concatenate(pieces, axis=1)

    def accumulate(oh):
        y = buf_ref[cur]
        y_hi = y.astype(bf)
        y_lo = (y - y_hi.astype(f32)).astype(bf)
        o_ref[...] += (jnp.dot(oh, y_hi, preferred_element_type=f32)
                       + jnp.dot(oh, y_lo, preferred_element_type=f32))

    def prefetch_next():
        nxt = jnp.minimum(c + 1, n_tiles - 1)

        @pl.when(jnp.logical_and(c + 1 < n_tiles, big_ref[nxt] == 0))
        def _():
            for cp in copies(nxt, 1 - cur, 0, MOE_W_SMALL, N_EXPERTS):
                cp.start()

    @pl.when(big_ref[c] == 0)
    def _():
        mine = copies(c, cur, 0, MOE_W_SMALL, N_EXPERTS)

        @pl.when(c == 0)
        def _():
            for cp in mine:
                cp.start()

        prefetch_next()
        oh = onehot(0, MOE_W_SMALL, N_EXPERTS)
        for cp in mine:
            cp.wait()
        accumulate(oh)

    @pl.when(big_ref[c] != 0)
    def _():
        prefetch_next()
        group = MOE_ROWS // MOE_TT
        for first in range(0, N_EXPERTS, group):
            mine = copies(c, cur, first, MOE_TT, group)
            for cp in mine:
                cp.start()
            oh = onehot(first, MOE_TT, group)
            for cp in mine:
                cp.wait()
            accumulate(oh)


def moe_combine(starts, big, x, code_n, y):
    n, d = x.shape
    row = pl.BlockSpec((MOE_TT, d), lambda c, s, b: (c, 0))
    return pl.pallas_call(
        functools.partial(_combine_kernel, n_tiles=n // MOE_TT),
        grid_spec=pltpu.PrefetchScalarGridSpec(
            num_scalar_prefetch=2,
            grid=(n // MOE_TT,),
            in_specs=[row, pl.BlockSpec((MOE_TT, LANES), lambda c, s, b: (c, 0)),
                      pl.BlockSpec(memory_space=pl.ANY)],
            out_specs=row,
            scratch_shapes=[pltpu.VMEM((2, MOE_ROWS, d), jnp.float32),
                            pltpu.SemaphoreType.DMA((2, MOE_ROWS // MOE_W_SMALL))],
        ),
        out_shape=jax.ShapeDtypeStruct((n, d), jnp.float32),
        compiler_params=_cparams("arbitrary"),
        name="moe_combine",
    )(starts, big, x, code_n, y)


def ec_moe(x, norm_g, w_router, w_gate, w_up, w_down):
    n_tok, _ = x.shape
    cap = EC_FACTOR * n_tok // N_EXPERTS
    h, aff = moe_router(x, norm_g, w_router)
    thr, need = moe_threshold(aff, cap)
    code_t, code_n, starts, total = moe_assign(aff, thr, need)
    starts = starts[:, :, 0]
    total = total[:, 0]
    ends = jnp.concatenate([starts[1:], total[None]], axis=0)
    big = (jnp.max(ends - starts, axis=1) > MOE_W_SMALL).astype(jnp.int32)
    starts = starts.reshape(-1)
    n_tiles = n_tok // MOE_TT
    slots = -(-(cap + MOE_ALIGN * n_tiles + MOE_TT) // MOE_TM) * MOE_TM
    xd = moe_dispatch(starts, big, h, code_t, aff, slots)
    y = expert_ffn(total, xd, w_gate, w_up, w_down)
    return moe_combine(starts, big, x, code_n, y)


def split_in_proj(w_in):
    w = w_in.astype(jnp.bfloat16)
    sizes = (LRU_WIDTH, LRU_WIDTH, MLA_Q_RANK, MLA_KV_RANK, MLA_ROPE,
             DIL_HEADS * DIL_HEAD_DIM, DIL_HEADS * DIL_HEAD_DIM, DIL_HEADS * DIL_HEAD_DIM,
             RET_HEADS * RET_QK, RET_HEADS * RET_QK, RET_HEADS * RET_V, RET_HEADS * RET_V,
             N_BRANCHES * D_MODEL)
    parts, off = [], 0
    for s in sizes:
        parts.append(w[:, off:off + s])
        off += s
    xa, ga, cq, ckv, kr, dq, dk, dv, rq, rk, rv, rg, zg = parts
    gw = DIL_GROUP_WIDTH
    grp = lambda a, g: a[:, g * gw:(g + 1) * gw]
    zeros = jnp.zeros((w.shape[0], Z_PAD), w.dtype)
    main = jnp.concatenate([zg, xa, ga, cq, ckv, kr, zeros, grp(dq, 0), grp(dk, 0), grp(dv, 0),
                            rq, rk, rv, rg], axis=1)
    dil = [jnp.concatenate([grp(dq, g), grp(dk, g), grp(dv, g)], axis=1) for g in range(1, len(DIL_PAIRS))]
    return main, dil


def mixer(x, batch, seq, norm_g, w_in, conv_w, conv_b, lru_gate_w, lru_gate_b, lru_lambda, mla_q_norm,
          mla_kv_norm, w_uq, w_ukv, ret_decay, w_branch, w_out, mla_tables, ret_tables):
    n = batch * seq
    w_main, w_dil = split_in_proj(w_in)
    z = norm_matmul(x, norm_g, w_main, jnp.bfloat16, min(1024, n), 1024)
    wg, gb = lru_gate_dense(lru_gate_w, lru_gate_b)
    h_fwd = lru_direction(z, conv_w, conv_b, wg, gb, lru_lambda, batch, seq, False)
    ya = lru_direction(z, conv_w, conv_b, wg, gb, lru_lambda, batch, seq, True, h_fwd)
    q, k, v = mla_project(z, mla_q_norm, mla_kv_norm, mla_weights(w_uq, w_ukv), mla_tables, batch, seq)
    yb = mla_attention(q, k, v, batch, seq)
    dil = [band_group(z.reshape(batch, 1, seq, Z_COLS), 0, 1, batch, seq, (CB_DQ, CB_DK, CB_DV))]
    pairs = DIL_HEADS_PER_GROUP // 2
    for g in range(1, len(DIL_PAIRS)):
        d = DIL_PAIRS[g][1]
        zd = norm_matmul_dil(x, norm_g, w_dil[g - 1], d, batch, seq, min(1024, seq))
        dil.append(band_group(zd, g, d, batch, seq, (0, pairs, 2 * pairs)))
    yd = retention(z, ret_decay, ret_tables, batch, seq)
    return merge_project(x, ya, yb, [o for o, _ in dil], [l for _, l in dil], yd, z,
                         w_branch.astype(jnp.bfloat16), w_out.astype(jnp.bfloat16), seq)


def trunk(x, norm_mix, w_in, conv_w, conv_b, lru_gate_w, lru_gate_b, lru_lambda, mla_q_norm, mla_kv_norm,
          w_uq, w_ukv, ret_decay, w_branch, w_out, norm_ffn, w_router, w_gate, w_up, w_down, norm_final):
    batch, seq, d = x.shape
    x = x.reshape(batch * seq, d)
    mla_tables = mla_rope_tables(seq)
    ret_tables = ret_rope_tables(seq)
    bf = jnp.bfloat16
    for l in range(norm_mix.shape[0]):
        x = mixer(x, batch, seq, norm_mix[l], w_in[l], conv_w[l], conv_b[l], lru_gate_w[l], lru_gate_b[l],
                  lru_lambda[l], mla_q_norm[l], mla_kv_norm[l], w_uq[l], w_ukv[l], ret_decay[l],
                  w_branch[l], w_out[l], mla_tables, ret_tables)
        x = ec_moe(x, norm_ffn[l], w_router[l], w_gate[l].astype(bf), w_up[l].astype(bf), w_down[l].astype(bf))
    return rmsnorm_pallas(x, norm_final, jnp.float32, min(1024, batch * seq)).reshape(batch, seq, d)


def kernel(x_prompt, x_sample, norm_mix, w_in, conv_w, conv_b, lru_gate_w, lru_gate_b, lru_lambda,
           mla_q_norm, mla_kv_norm, w_uq, w_ukv, ret_decay, w_branch, w_out, norm_ffn, w_router,
           w_gate, w_up, w_down, norm_final):
    args = (norm_mix, w_in, conv_w, conv_b, lru_gate_w, lru_gate_b, lru_lambda, mla_q_norm, mla_kv_norm,
            w_uq, w_ukv, ret_decay, w_branch, w_out, norm_ffn, w_router, w_gate, w_up, w_down, norm_final)
    return trunk(x_prompt, *args), trunk(x_sample, *args)
```

```python
import functools
import math

import numpy as np
import jax
import jax.numpy as jnp
from jax import lax
from jax.experimental import pallas as pl
from jax.experimental.pallas import tpu as pltpu

D_MODEL = 1024
DEPTH = 4
EPS = 1e-6
NEG_INF = -1e30
ROPE_BASE = 10000.0
N_BRANCHES = 4
LRU_WIDTH = 512
LRU_BLOCKS = 8
LRU_BLOCK = LRU_WIDTH // LRU_BLOCKS
CONV_WIDTH = 4
CONV_LEFT = 2
LRU_C = 8.0
MLA_HEADS = 8
MLA_NOPE = 64
MLA_ROPE = 32
MLA_V = 64
MLA_Q_RANK = 256
MLA_KV_RANK = 128
DIL_PAIRS = ((128, 1), (512, 4), (2048, 16))
DIL_HEADS_PER_GROUP = 8
DIL_HEADS = DIL_HEADS_PER_GROUP * len(DIL_PAIRS)
DIL_HEAD_DIM = 64
RET_HEADS = 4
RET_QK = 128
RET_V = 128
N_EXPERTS = 16
EC_FACTOR = 2
D_EXPERT = 1024

LANES = 128
VMEM_LIMIT_BYTES = 56 * 1024 * 1024

Z_COLS = 9216
Z_PAD = 96
CB_ZG = 0
CB_XA, CB_GA = 32, 36
CB_MLA = 40
CB_DQ, CB_DK, CB_DV = 44, 48, 52
CB_RQ, CB_RK, CB_RV, CB_RG = 56, 60, 64, 68
DIL_GROUP_WIDTH = DIL_HEADS_PER_GROUP * DIL_HEAD_DIM


def _cparams(*sem):
    return pltpu.CompilerParams(dimension_semantics=sem, vmem_limit_bytes=VMEM_LIMIT_BYTES)


def _norm_matmul_kernel(x_ref, g_ref, w_ref, o_ref, h_ref):
    @pl.when(pl.program_id(1) == 0)
    def _():
        x = x_ref[...]
        y = x * lax.rsqrt(jnp.mean(x * x, axis=-1, keepdims=True) + EPS)
        h_ref[...] = (y * g_ref[...]).astype(h_ref.dtype)

    o_ref[...] = jnp.dot(h_ref[...], w_ref[...], preferred_element_type=jnp.float32).astype(o_ref.dtype)


def norm_matmul(x, g, w, out_dtype, tm, tn):
    n, d = x.shape
    c = w.shape[1]
    return pl.pallas_call(
        _norm_matmul_kernel,
        grid=(n // tm, c // tn),
        in_specs=[
            pl.BlockSpec((tm, d), lambda i, j: (i, 0)),
            pl.BlockSpec((1, d), lambda i, j: (0, 0)),
            pl.BlockSpec((d, tn), lambda i, j: (0, j)),
        ],
        out_specs=pl.BlockSpec((tm, tn), lambda i, j: (i, j)),
        out_shape=jax.ShapeDtypeStruct((n, c), out_dtype),
        scratch_shapes=[pltpu.VMEM((tm, d), jnp.bfloat16)],
        compiler_params=_cparams("arbitrary", "arbitrary"),
        name="norm_matmul",
    )(x, g.reshape(1, d), w)


def _norm_matmul_dil_kernel(x_ref, g_ref, w_ref, o_ref, hf_ref, hp_ref, *, dil):
    x = x_ref[...]
    y = x * lax.rsqrt(jnp.mean(x * x, axis=-1, keepdims=True) + EPS)
    y = y * g_ref[...]
    rows = x.shape[0] // dil
    for c in range(x.shape[1] // LANES):
        cols = slice(c * LANES, (c + 1) * LANES)
        hf_ref[c] = y[:, cols]
        for r in range(dil):
            hp_ref[r * rows:(r + 1) * rows, cols] = hf_ref[c, pl.ds(r, rows, stride=dil), :].astype(hp_ref.dtype)
    out = jnp.dot(hp_ref[...], w_ref[...], preferred_element_type=jnp.float32)
    for r in range(dil):
        o_ref[0, r] = out[r * rows:(r + 1) * rows, :].astype(o_ref.dtype)


def norm_matmul_dil(x, g, w, dil, batch, seq, tm):
    n, d = x.shape
    c = w.shape[1]
    n_t = seq // tm
    return pl.pallas_call(
        functools.partial(_norm_matmul_dil_kernel, dil=dil),
        grid=(n // tm,),
        in_specs=[
            pl.BlockSpec((tm, d), lambda i: (i, 0)),
            pl.BlockSpec((1, d), lambda i: (0, 0)),
            pl.BlockSpec((d, c), lambda i: (0, 0)),
        ],
        out_specs=pl.BlockSpec((1, dil, tm // dil, c), lambda i: (i // n_t, 0, i % n_t, 0)),
        out_shape=jax.ShapeDtypeStruct((batch, dil, seq // dil, c), jnp.bfloat16),
        scratch_shapes=[pltpu.VMEM((d // LANES, tm, LANES), jnp.float32), pltpu.VMEM((tm, d), jnp.bfloat16)],
        compiler_params=_cparams("arbitrary"),
        name=f"norm_matmul_dil{dil}",
    )(x, g.reshape(1, d), w)


def _rmsnorm_kernel(x_ref, g_ref, o_ref):
    x = x_ref[...]
    y = x * lax.rsqrt(jnp.mean(x * x, axis=-1, keepdims=True) + EPS)
    o_ref[...] = (y * g_ref[...]).astype(o_ref.dtype)


def rmsnorm_pallas(x, g, out_dtype, tm):
    n, d = x.shape
    return pl.pallas_call(
        _rmsnorm_kernel,
        grid=(n // tm,),
        in_specs=[pl.BlockSpec((tm, d), lambda i: (i, 0)), pl.BlockSpec((1, d), lambda i: (0, 0))],
        out_specs=pl.BlockSpec((tm, d), lambda i: (i, 0)),
        out_shape=jax.ShapeDtypeStruct((n, d), out_dtype),
        compiler_params=_cparams("arbitrary"),
        name="rmsnorm",
    )(x, g.reshape(1, d))


LRU_TC = 256
LRU_HALO = 16


def _softplus(x):
    return jnp.maximum(x, 0.0) + jnp.log(1.0 + jnp.exp(-jnp.abs(x)))


def _gelu_tanh(x):
    return 0.5 * x * (1.0 + jnp.tanh(math.sqrt(2.0 / math.pi) * (x + 0.044715 * (x * x * x))))


def _lru_scan_chunk(a, b, reverse):
    n = a.shape[0]
    row = lax.broadcasted_iota(jnp.int32, a.shape, 0)
    s = 1
    while s < n:
        if reverse:
            keep = row < (n - s)
            a_s = pltpu.roll(a, n - s, 0)
            b_s = pltpu.roll(b, n - s, 0)
        else:
            keep = row >= s
            a_s = pltpu.roll(a, s, 0)
            b_s = pltpu.roll(b, s, 0)
        b = jnp.where(keep, a * b_s + b, b)
        a = jnp.where(keep, a * a_s, a)
        s *= 2
    return a, b


def _lru_kernel(*refs, reverse, n_t):
    if reverse:
        (xp_ref, xc_ref, xn_ref, cw_ref, cb_ref, wg_ref, gb_ref, lam_ref, hf_ref, ga_ref,
         o_ref, carry_ref) = refs
    else:
        (xp_ref, xc_ref, xn_ref, cw_ref, cb_ref, wg_ref, gb_ref, lam_ref, o_ref, carry_ref) = refs
    step = pl.program_id(1)
    t = (n_t - 1 - step) if reverse else step

    @pl.when(step == 0)
    def _():
        carry_ref[...] = jnp.zeros_like(carry_ref)

    prev = jnp.where(t > 0, xp_ref[...].astype(jnp.float32), 0.0)
    nxt = jnp.where(t < n_t - 1, xn_ref[...].astype(jnp.float32), 0.0)
    win = jnp.concatenate([prev, xc_ref[...].astype(jnp.float32), nxt], axis=0)
    xc = cb_ref[...]
    for k in range(CONV_WIDTH):
        lo = LRU_HALO - CONV_LEFT + k
        xc = xc + cw_ref[k:k + 1, :] * win[lo:lo + LRU_TC, :]
    gl = jnp.dot(xc.astype(jnp.bfloat16), wg_ref[0], preferred_element_type=jnp.float32) + gb_ref[0]
    r = jax.nn.sigmoid(gl[:, :LRU_WIDTH])
    i = jax.nn.sigmoid(gl[:, LRU_WIDTH:])
    log_a = (-LRU_C) * r * _softplus(-lam_ref[0])
    a = jnp.exp(log_a)
    b = jnp.sqrt(1.0 - jnp.exp(2.0 * log_a)) * i * xc
    a_cum, b_cum = _lru_scan_chunk(a, b, reverse)
    h = b_cum + a_cum * carry_ref[0:1, :]
    last = 0 if reverse else LRU_TC - 1
    carry_ref[0:1, :] = h[last:last + 1, :]
    if reverse:
        o_ref[...] = (_gelu_tanh(ga_ref[...].astype(jnp.float32)) * (hf_ref[...] + h)).astype(o_ref.dtype)
    else:
        o_ref[...] = h


def lru_direction(z, conv_w, conv_b, wg, gb, lam, batch, seq, reverse, h_fwd=None):
    n_t = seq // LRU_TC
    per_halo = LRU_TC // LRU_HALO
    n_halo = batch * seq // LRU_HALO
    d = 1 if reverse else 0

    def tt(s):
        return (n_t - 1 - s) if reverse else s

    xa_col = CB_XA * LANES // LRU_WIDTH
    ga_col = CB_GA * LANES // LRU_WIDTH

    def cur(b, s):
        return (b * n_t + tt(s), 0)

    def prev(b, s):
        return (jnp.maximum((b * n_t + tt(s)) * per_halo - 1, 0), xa_col)

    def nxt(b, s):
        return (jnp.minimum((b * n_t + tt(s) + 1) * per_halo, n_halo - 1), xa_col)

    const2 = lambda b, s: (0, 0)
    in_specs = [
        pl.BlockSpec((LRU_HALO, LRU_WIDTH), prev),
        pl.BlockSpec((LRU_TC, LRU_WIDTH), lambda b, s: (b * n_t + tt(s), xa_col)),
        pl.BlockSpec((LRU_HALO, LRU_WIDTH), nxt),
        pl.BlockSpec((CONV_WIDTH, LRU_WIDTH), const2),
        pl.BlockSpec((1, LRU_WIDTH), const2),
        pl.BlockSpec((1, LRU_WIDTH, 2 * LRU_WIDTH), lambda b, s: (d, 0, 0)),
        pl.BlockSpec((1, 1, 2 * LRU_WIDTH), lambda b, s: (d, 0, 0)),
        pl.BlockSpec((1, 1, LRU_WIDTH), lambda b, s: (d, 0, 0)),
    ]
    args = [z, z, z, conv_w, conv_b.reshape(1, LRU_WIDTH), wg, gb, lam.reshape(2, 1, LRU_WIDTH)]
    if reverse:
        in_specs += [pl.BlockSpec((LRU_TC, LRU_WIDTH), cur),
                     pl.BlockSpec((LRU_TC, LRU_WIDTH), lambda b, s: (b * n_t + tt(s), ga_col))]
        args += [h_fwd, z]
        out_dtype = jnp.bfloat16
    else:
        out_dtype = jnp.float32
    return pl.pallas_call(
        functools.partial(_lru_kernel, reverse=reverse, n_t=n_t),
        grid=(batch, n_t),
        in_specs=in_specs,
        out_specs=pl.BlockSpec((LRU_TC, LRU_WIDTH), cur),
        out_shape=jax.ShapeDtypeStruct((batch * seq, LRU_WIDTH), out_dtype),
        scratch_shapes=[pltpu.VMEM((8, LRU_WIDTH), jnp.float32)],
        compiler_params=_cparams("arbitrary", "arbitrary"),
        name="lru_bwd" if reverse else "lru_fwd",
    )(*args)


def lru_gate_dense(gate_w, gate_b):
    eye = jnp.eye(LRU_BLOCKS, dtype=gate_w.dtype)
    dense = jnp.einsum('dgnij,nm->dgnimj', gate_w, eye).reshape(2, 2, LRU_WIDTH, LRU_WIDTH)
    wg = jnp.concatenate([dense[:, 0], dense[:, 1]], axis=-1).astype(jnp.bfloat16)
    gb = jnp.concatenate([gate_b[:, 0], gate_b[:, 1]], axis=-1).reshape(2, 1, 2 * LRU_WIDTH)
    return wg, gb


MLA_TM = 512
MLA_TQ = 2048
MLA_TK = 256
MLA_DP = 128


def _mla_proj_kernel(z_ref, qn_ref, kn_ref, wqa_ref, wqb_ref, wk_ref, wv_ref, ea_ref, eb_ref,
                     cos_ref, sin_ref, q_ref, k_ref, v_ref):
    z = z_ref[...]
    cq = z[:, :MLA_Q_RANK].astype(jnp.float32)
    ckv = z[:, MLA_Q_RANK:MLA_Q_RANK + MLA_KV_RANK].astype(jnp.float32)
    kr = z[:, MLA_Q_RANK + MLA_KV_RANK:]
    cqn = (cq * lax.rsqrt(jnp.mean(cq * cq, axis=-1, keepdims=True) + EPS) * qn_ref[...]).astype(jnp.bfloat16)
    ckn = (ckv * lax.rsqrt(jnp.mean(ckv * ckv, axis=-1, keepdims=True) + EPS) * kn_ref[...]).astype(jnp.bfloat16)
    cos = cos_ref[...]
    sin = sin_ref[...]
    f32 = jnp.float32
    k_rope = (jnp.dot(kr, ea_ref[...], preferred_element_type=f32) * cos
              + jnp.dot(kr, eb_ref[...], preferred_element_type=f32) * sin)
    scale = (MLA_NOPE + MLA_ROPE) ** -0.5 * math.log2(math.e)
    row = lax.broadcasted_iota(jnp.int32, (MLA_DP, MLA_TM), 0)
    for h in range(MLA_HEADS):
        qa = jnp.dot(cqn, wqa_ref[h], preferred_element_type=f32)
        qb = jnp.dot(cqn, wqb_ref[h], preferred_element_type=f32)
        q_ref[0, h] = ((qa * cos + qb * sin) * scale).astype(q_ref.dtype)
        k_ref[0, h] = (jnp.dot(ckn, wk_ref[h], preferred_element_type=f32) + k_rope).astype(k_ref.dtype)
        vt = lax.dot_general(wv_ref[h], ckn, (((1,), (1,)), ((), ())), preferred_element_type=f32)
        vt = jnp.where(row == _mla_ones_row(h), 1.0, vt)
        for c in range(MLA_TM // MLA_TK):
            v_ref[0, h, c] = vt[:, c * MLA_TK:(c + 1) * MLA_TK].astype(v_ref.dtype)


def _mla_ones_row(head):
    return MLA_V if head % 2 == 0 else 0


def _rot_half_matrix(n):
    half = n // 2
    r = np.zeros((n, n), np.float32)
    for j in range(half):
        r[half + j, j] = -1.0
        r[j, half + j] = 1.0
    return r


def mla_weights(w_uq, w_ukv):
    rot = jnp.asarray(_rot_half_matrix(MLA_ROPE))
    wq = jnp.transpose(w_uq, (1, 0, 2))
    pad = lambda a, lo, hi: jnp.pad(a, ((0, 0), (0, 0), (lo, hi)))
    wqa = pad(wq, 0, MLA_DP - MLA_NOPE - MLA_ROPE)
    wqb = pad(jnp.einsum('hrd,de->hre', wq[..., MLA_NOPE:], rot), MLA_NOPE, MLA_DP - MLA_NOPE - MLA_ROPE)
    wkv = jnp.transpose(w_ukv, (1, 0, 2))
    wk = pad(wkv[..., :MLA_NOPE], 0, MLA_DP - MLA_NOPE)
    wv_even = pad(wkv[..., MLA_NOPE:], 0, MLA_V)
    wv_odd = pad(wkv[..., MLA_NOPE:], MLA_V, 0)
    wv = jnp.where((jnp.arange(MLA_HEADS) % 2 == 0)[:, None, None], wv_even, wv_odd)
    wv = jnp.transpose(wv, (0, 2, 1))
    ea = np.zeros((LANES, MLA_DP), np.float32)
    for j in range(MLA_ROPE):
        ea[j, MLA_NOPE + j] = 1.0
    eb = np.zeros((LANES, MLA_DP), np.float32)
    eb[:MLA_ROPE, MLA_NOPE:MLA_NOPE + MLA_ROPE] = _rot_half_matrix(MLA_ROPE)
    bf = jnp.bfloat16
    return (wqa.astype(bf), wqb.astype(bf), wk.astype(bf), wv.astype(bf),
            jnp.asarray(ea, bf), jnp.asarray(eb, bf))


def mla_rope_tables(seq):
    half = MLA_ROPE // 2
    inv = ROPE_BASE ** (-jnp.arange(half, dtype=jnp.float32) / half)
    ang = jnp.arange(seq, dtype=jnp.float32)[:, None] * inv[None, :]
    ones = jnp.ones((seq, MLA_NOPE), jnp.float32)
    zeros = jnp.zeros((seq, MLA_DP - MLA_NOPE - MLA_ROPE), jnp.float32)
    cos = jnp.concatenate([ones, jnp.cos(ang), jnp.cos(ang), zeros], axis=1)
    sin = jnp.concatenate([0.0 * ones, jnp.sin(ang), jnp.sin(ang), zeros], axis=1)
    return cos, sin


def mla_project(z, q_norm, kv_norm, weights, tables, batch, seq):
    wqa, wqb, wk, wv, ea, eb = weights
    cos, sin = tables
    n_t = seq // MLA_TM
    hshape = (batch, MLA_HEADS, seq, MLA_DP)
    c3 = lambda b, t: (0, 0, 0)
    c2 = lambda b, t: (0, 0)
    hspec = pl.BlockSpec((1, MLA_HEADS, MLA_TM, MLA_DP), lambda b, t: (b, 0, t, 0))
    return pl.pallas_call(
        _mla_proj_kernel,
        grid=(batch, n_t),
        in_specs=[
            pl.BlockSpec((MLA_TM, 4 * LANES), lambda b, t: (b * n_t + t, CB_MLA // 4)),
            pl.BlockSpec((1, MLA_Q_RANK), c2),
            pl.BlockSpec((1, MLA_KV_RANK), c2),
            pl.BlockSpec((MLA_HEADS, MLA_Q_RANK, MLA_DP), c3),
            pl.BlockSpec((MLA_HEADS, MLA_Q_RANK, MLA_DP), c3),
            pl.BlockSpec((MLA_HEADS, MLA_KV_RANK, MLA_DP), c3),
            pl.BlockSpec((MLA_HEADS, MLA_KV_RANK, MLA_DP), c3),
            pl.BlockSpec((LANES, MLA_DP), c2),
            pl.BlockSpec((LANES, MLA_DP), c2),
            pl.BlockSpec((MLA_TM, MLA_DP), lambda b, t: (t, 0)),
            pl.BlockSpec((MLA_TM, MLA_DP), lambda b, t: (t, 0)),
        ],
        out_specs=[hspec, hspec,
                   pl.BlockSpec((1, MLA_HEADS, MLA_TM // MLA_TK, MLA_DP, MLA_TK), lambda b, t: (b, 0, t, 0, 0))],
        out_shape=[jax.ShapeDtypeStruct(hshape, jnp.bfloat16)] * 2
        + [jax.ShapeDtypeStruct((batch, MLA_HEADS, seq // MLA_TK, MLA_DP, MLA_TK), jnp.bfloat16)],
        compiler_params=_cparams("arbitrary", "arbitrary"),
        name="mla_project",
    )(z, q_norm.reshape(1, -1), kv_norm.reshape(1, -1), wqa, wqb, wk, wv, ea, eb, cos, sin)


def _mla_attn_kernel(q_ref, k_ref, vt_ref, o_ref, acc_ref, *, n_kv):
    acc_ref[...] = jnp.zeros_like(acc_ref)
    f32 = jnp.float32
    nt = (((1,), (1,)), ((), ()))

    def body(j, carry):
        rows = pl.ds(pl.multiple_of(j * MLA_TK, MLA_TK), MLA_TK)
        new = []
        for hh in range(2):
            m_prev = carry[hh]
            s = lax.dot_general(k_ref[0, hh, rows, :], q_ref[0, hh], nt, preferred_element_type=f32)
            m_next = jnp.maximum(m_prev, jnp.max(s, axis=0, keepdims=True))
            p = jnp.exp2(s - m_next)
            alpha = jnp.exp2(m_prev - m_next)
            new.append(m_next)
            acc_ref[hh] = alpha * acc_ref[hh] + jnp.dot(vt_ref[0, hh, j], p.astype(jnp.bfloat16),
                                                         preferred_element_type=f32)
        return tuple(new)

    m0 = jnp.full((1, MLA_TQ), NEG_INF, f32)
    lax.fori_loop(0, n_kv, body, (m0, m0), unroll=min(4, n_kv))
    acc_a, acc_b = acc_ref[0], acc_ref[1]
    l_a = acc_a[_mla_ones_row(0):_mla_ones_row(0) + 1, :]
    l_b = acc_b[_mla_ones_row(1):_mla_ones_row(1) + 1, :]
    row = lax.broadcasted_iota(jnp.int32, acc_a.shape, 0)
    out_t = jnp.where(row < MLA_V, acc_a / l_a, acc_b / l_b)
    o_ref[...] = out_t.T.astype(o_ref.dtype)


def mla_attention(q, k, vt, batch, seq):
    n_q = seq // MLA_TQ
    n_kv = seq // MLA_TK
    qspec = pl.BlockSpec((1, 2, MLA_TQ, MLA_DP), lambda b, hp, i: (b, hp, i, 0))
    kspec = pl.BlockSpec((1, 2, seq, MLA_DP), lambda b, hp, i: (b, hp, 0, 0))
    vspec = pl.BlockSpec((1, 2, n_kv, MLA_DP, MLA_TK), lambda b, hp, i: (b, hp, 0, 0, 0))
    return pl.pallas_call(
        functools.partial(_mla_attn_kernel, n_kv=n_kv),
        grid=(batch, MLA_HEADS // 2, n_q),
        in_specs=[qspec, kspec, vspec],
        out_specs=pl.BlockSpec((MLA_TQ, LANES), lambda b, hp, i: (b * n_q + i, hp)),
        out_shape=jax.ShapeDtypeStruct((batch * seq, MLA_HEADS * MLA_V), jnp.bfloat16),
        scratch_shapes=[pltpu.VMEM((2, MLA_DP, MLA_TQ), jnp.float32)],
        compiler_params=_cparams("arbitrary", "arbitrary", "arbitrary"),
        name="mla_attention",
    )(q, k, vt)


BAND_Q = 128
BAND_W = 256
BAND_RADIUS = 64
BAND_ROWS_PER_STEP = 2048


def band_bias_table(group, dil):
    n = DIL_HEADS
    slopes = np.asarray([2.0 ** (-8.0 * (h + 1) / n) for h in range(n)], np.float32)
    slopes = slopes[group * DIL_HEADS_PER_GROUP:(group + 1) * DIL_HEADS_PER_GROUP]
    iq = np.arange(BAND_Q)[:, None]
    ik = np.arange(BAND_W)[None, :]
    tabs = []
    for d in range(3):
        dist = np.abs(d * BAND_RADIUS + iq - ik)
        bias = -slopes[:, None, None] * (dil * dist).astype(np.float32)[None]
        tabs.append(np.where((dist <= BAND_RADIUS)[None], bias, np.float32(NEG_INF)))
    return jnp.asarray(np.stack(tabs).astype(np.float32))


def _band_kernel(q_ref, k_ref, v_ref, bias_ref, o_ref, lse_ref, *, length):
    lane = lax.broadcasted_iota(jnp.int32, (1, LANES), 1)
    first = lane < DIL_HEAD_DIM
    scale = DIL_HEAD_DIM ** -0.5

    n_blocks = length // BAND_Q
    n_res = q_ref.shape[1]

    def body(idx, carry):
        rr = idx // n_blocks
        q0 = pl.multiple_of((idx % n_blocks) * BAND_Q, BAND_Q)
        start = pl.multiple_of(jnp.clip(q0 - BAND_RADIUS, 0, length - BAND_W), BAND_RADIUS)
        didx = (q0 - start) // BAND_RADIUS
        q = q_ref[0, rr, pl.ds(q0, BAND_Q), :]
        kw = k_ref[0, rr, pl.ds(start, BAND_W), :]
        vw = v_ref[0, rr, pl.ds(start, BAND_W), :]
        outs, lses = [], []
        for hh in range(2):
            sel = first if hh == 0 else jnp.logical_not(first)
            qh = jnp.where(sel, q, jnp.zeros_like(q))
            s = lax.dot_general(qh, kw, (((1,), (1,)), ((), ())), preferred_element_type=jnp.float32)
            s = s * scale + bias_ref[didx, hh]
            m = jnp.max(s, axis=1, keepdims=True)
            e = jnp.exp(s - m)
            den = jnp.sum(e, axis=1, keepdims=True)
            p = (e / den).astype(jnp.bfloat16)
            outs.append(jnp.dot(p, vw, preferred_element_type=jnp.float32))
            lses.append(m + jnp.log(den))
        o_ref[0, rr, pl.ds(q0, BAND_Q), :] = jnp.where(first, outs[0], outs[1]).astype(o_ref.dtype)
        lse_ref[0, rr, pl.ds(q0, BAND_Q), :] = jnp.where(first, lses[0], lses[1])
        return carry

    lax.fori_loop(0, n_res * n_blocks, body, 0, unroll=min(16, n_res * n_blocks))


def band_group(src, group, dil, batch, seq, col_blocks):
    length = seq // dil
    pairs = DIL_HEADS_PER_GROUP // 2
    n_res = max(1, min(dil, BAND_ROWS_PER_STEP // length))

    def zspec(cb):
        return pl.BlockSpec((1, n_res, length, LANES), lambda b, r, hp: (b, r, 0, cb + hp))

    ospec = pl.BlockSpec((1, n_res, length, LANES), lambda b, r, hp: (b, r, 0, hp))
    oshape = (batch, dil, length, DIL_GROUP_WIDTH)
    return pl.pallas_call(
        functools.partial(_band_kernel, length=length),
        grid=(batch, dil // n_res, pairs),
        in_specs=[zspec(col_blocks[0]), zspec(col_blocks[1]), zspec(col_blocks[2]),
                  pl.BlockSpec((3, 2, BAND_Q, BAND_W), lambda b, r, hp: (0, hp, 0, 0))],
        out_specs=[ospec, ospec],
        out_shape=[jax.ShapeDtypeStruct(oshape, jnp.bfloat16), jax.ShapeDtypeStruct(oshape, jnp.float32)],
        compiler_params=_cparams("arbitrary", "arbitrary", "arbitrary"),
        name=f"band_attention_g{group}",
    )(src, src, src, band_bias_table(group, dil))


RET_C = 256


def _ret_kernel(lg_ref, q_ref, k_ref, v_ref, g_ref, cos_ref, sin_ref, o_ref, qs_ref, ks_ref, o1_ref, *, n_c):
    h = pl.program_id(1)
    lgf = lg_ref[0, h]
    lgb = lg_ref[1, h]
    c = RET_C
    f32 = jnp.float32
    bf = jnp.bfloat16
    ii = lax.broadcasted_iota(jnp.int32, (c, c), 0)
    jj = lax.broadcasted_iota(jnp.int32, (c, c), 1)
    diff = (ii - jj).astype(f32)
    decay = jnp.where(diff >= 0.0, jnp.exp(lgf * jnp.maximum(diff, 0.0)), jnp.exp(lgb * jnp.maximum(-diff, 0.0)))
    idx = lax.broadcasted_iota(jnp.int32, (c, 1), 0).astype(f32)
    xi_f = jnp.exp(lgf * (idx + 1.0))
    zeta_f = jnp.exp(lgf * (c - 1.0 - idx))
    xi_b = jnp.exp(lgb * (c - idx))
    zeta_b = jnp.exp(lgb * idx)
    cd_f = jnp.exp(lgf * c)
    cd_b = jnp.exp(lgb * c)
    kscale = RET_QK ** -0.5

    def rope(x, rows):
        return x * cos_ref[rows, :] + pltpu.roll(x, RET_QK // 2, 1) * sin_ref[rows, :]

    def fwd(n, state):
        rows = pl.ds(pl.multiple_of(n * c, c), c)
        q = rope(q_ref[0, rows, :].astype(f32), rows)
        k = rope(k_ref[0, rows, :].astype(f32), rows) * kscale
        v = v_ref[0, rows, :]
        qs_ref[rows, :] = q
        ks_ref[rows, :] = k
        qb = q.astype(bf)
        s = lax.dot_general(qb, k.astype(bf), (((1,), (1,)), ((), ())), preferred_element_type=f32) * decay
        o = jnp.dot(s.astype(bf), v, preferred_element_type=f32)
        o = o + xi_f * jnp.dot(qb, state.astype(bf), preferred_element_type=f32)
        o1_ref[rows, :] = o
        kz = (k * zeta_f).T.astype(bf)
        return cd_f * state + jnp.dot(kz, v, preferred_element_type=f32)

    lax.fori_loop(0, n_c, fwd, jnp.zeros((RET_QK, RET_V), f32), unroll=2)

    def bwd(step, state):
        n = n_c - 1 - step
        rows = pl.ds(pl.multiple_of(n * c, c), c)
        q = qs_ref[rows, :]
        k = ks_ref[rows, :]
        v = v_ref[0, rows, :]
        of = o1_ref[rows, :] + xi_b * jnp.dot(q.astype(bf), state.astype(bf), preferred_element_type=f32)
        mu = jnp.mean(of, axis=-1, keepdims=True)
        var = jnp.mean(jnp.square(of - mu), axis=-1, keepdims=True)
        of = (of - mu) * lax.rsqrt(var + EPS)
        g = g_ref[0, rows, :].astype(f32)
        o_ref[0, rows, :] = (g * jax.nn.sigmoid(g) * of).astype(o_ref.dtype)
        kz = (k * zeta_b).T.astype(bf)
        return cd_b * state + jnp.dot(kz, v, preferred_element_type=f32)

    lax.fori_loop(0, n_c, bwd, jnp.zeros((RET_QK, RET_V), f32), unroll=2)


def ret_rope_tables(seq):
    half = RET_QK // 2
    inv = ROPE_BASE ** (-jnp.arange(half, dtype=jnp.float32) / half)
    ang = jnp.arange(seq, dtype=jnp.float32)[:, None] * inv[None, :]
    cos = jnp.concatenate([jnp.cos(ang), jnp.cos(ang)], axis=1)
    sin = jnp.concatenate([-jnp.sin(ang), jnp.sin(ang)], axis=1)
    return cos, sin


def retention(z, ret_decay, tables, batch, seq):
    zv = z.reshape(batch, seq, Z_COLS)
    log_gamma = jax.nn.log_sigmoid(ret_decay.astype(jnp.float32))
    cos, sin = tables

    def zspec(cb):
        return pl.BlockSpec((1, seq, LANES), lambda b, h: (b, 0, cb + h))

    tspec = pl.BlockSpec((seq, RET_QK), lambda b, h: (0, 0))
    out = pl.pallas_call(
        functools.partial(_ret_kernel, n_c=seq // RET_C),
        grid=(batch, RET_HEADS),
        in_specs=[pl.BlockSpec(memory_space=pltpu.SMEM),
                  zspec(CB_RQ), zspec(CB_RK), zspec(CB_RV), zspec(CB_RG), tspec, tspec],
        out_specs=pl.BlockSpec((1, seq, LANES), lambda b, h: (b, 0, h)),
        out_shape=jax.ShapeDtypeStruct((batch, seq, RET_HEADS * RET_V), jnp.bfloat16),
        scratch_shapes=[pltpu.VMEM((seq, RET_QK), jnp.float32)] * 3,
        compiler_params=_cparams("arbitrary", "arbitrary"),
        name="retention",
    )(log_gamma, zv, zv, zv, zv, cos, sin)
    return out.reshape(batch * seq, RET_HEADS * RET_V)


MERGE_TM = 512


def _merge_kernel(x_ref, ya_ref, yb_ref, o0_ref, o1_ref, o2_ref, l0_ref, l1_ref, l2_ref, yd_ref,
                  zg_ref, wb_ref, wo_ref, out_ref, tok_ref):
    f32 = jnp.float32

    def token_order(slot, src_ref):
        dil, rows, width = src_ref.shape[1:]
        tiles = width // LANES
        for r in range(dil):
            v = src_ref[0, r].astype(f32)
            for c in range(tiles):
                tok_ref[slot * tiles + c, pl.ds(r, rows, stride=dil), :] = v[:, c * LANES:(c + 1) * LANES]
        return jnp.concatenate([tok_ref[slot * tiles + c] for c in range(tiles)], axis=1)

    l0 = l0_ref[0, 0]
    l1, l2 = token_order(0, l1_ref), token_order(1, l2_ref)
    o1, o2 = token_order(2, o1_ref), token_order(3, o2_ref)
    m = jnp.maximum(jnp.maximum(l0, l1), l2)
    e0, e1, e2 = jnp.exp(l0 - m), jnp.exp(l1 - m), jnp.exp(l2 - m)
    inv = 1.0 / (e0 + e1 + e2)
    yc = ((e0 * inv) * o0_ref[0, 0].astype(f32) + (e1 * inv) * o1 + (e2 * inv) * o2).astype(jnp.bfloat16)
    merged = None
    for i, y in enumerate((ya_ref[...], yb_ref[...], yc, yd_ref[...])):
        gate = jax.nn.sigmoid(zg_ref[:, i * D_MODEL:(i + 1) * D_MODEL].astype(f32))
        term = gate * jnp.dot(y, wb_ref[i], preferred_element_type=f32)
        merged = term if merged is None else merged + term
    out_ref[...] = x_ref[...] + jnp.dot(merged.astype(jnp.bfloat16), wo_ref[...], preferred_element_type=f32)


def merge_project(x, ya, yb, dil_o, dil_lse, yd, z, w_branch, w_out, seq):
    n = x.shape[0]
    tm = MERGE_TM
    n_t = seq // tm
    width = 4 * LANES
    row = lambda i: (i, 0)
    bspec = pl.BlockSpec((tm, width), row)

    def dspec(dil):
        return pl.BlockSpec((1, dil, tm // dil, width), lambda i: (i // n_t, 0, i % n_t, 0))

    dspecs = [dspec(d) for _, d in DIL_PAIRS]
    return pl.pallas_call(
        _merge_kernel,
        grid=(n // tm,),
        in_specs=[pl.BlockSpec((tm, D_MODEL), row), bspec, bspec, *dspecs, *dspecs, bspec,
                  pl.BlockSpec((tm, N_BRANCHES * D_MODEL), lambda i: (i, CB_ZG * LANES // (N_BRANCHES * D_MODEL))),
                  pl.BlockSpec((N_BRANCHES, width, D_MODEL), lambda i: (0, 0, 0)),
                  pl.BlockSpec((D_MODEL, D_MODEL), lambda i: (0, 0))],
        out_specs=pl.BlockSpec((tm, D_MODEL), row),
        out_shape=jax.ShapeDtypeStruct((n, D_MODEL), jnp.float32),
        scratch_shapes=[pltpu.VMEM((4 * width // LANES, tm, LANES), jnp.float32)],
        compiler_params=_cparams("arbitrary"),
        name="merge_project",
    )(x, ya, yb, *dil_o, *dil_lse, yd, z, w_branch, w_out)


MOE_TM = 512
MOE_TT = 256
MOE_W_SMALL = 64
MOE_ROWS = 1024
MOE_ALIGN = 8
MOE_XW = D_MODEL + LANES


def _router_kernel(x_ref, g_ref, w_ref, h_ref, aff_ref):
    x = x_ref[...]
    y = x * lax.rsqrt(jnp.mean(x * x, axis=-1, keepdims=True) + EPS)
    h = (y * g_ref[...]).astype(jnp.bfloat16)
    h_ref[...] = h
    logits = jnp.dot(h, w_ref[...], preferred_element_type=jnp.float32)
    lane = lax.broadcasted_iota(jnp.int32, logits.shape, 1)
    logits = jnp.where(lane < N_EXPERTS, logits, NEG_INF)
    e = jnp.exp(logits - jnp.max(logits, axis=-1, keepdims=True))
    aff_t = (e / jnp.sum(e, axis=-1, keepdims=True)).T
    for c in range(x.shape[0] // MOE_TT):
        aff_ref[c] = aff_t[:N_EXPERTS, c * MOE_TT:(c + 1) * MOE_TT]


def moe_router(x, norm_g, w_router):
    n, d = x.shape
    tm = 2 * MOE_TT
    w = jnp.pad(w_router.astype(jnp.bfloat16), ((0, 0), (0, LANES - N_EXPERTS)))
    return pl.pallas_call(
        _router_kernel,
        grid=(n // tm,),
        in_specs=[pl.BlockSpec((tm, d), lambda i: (i, 0)), pl.BlockSpec((1, d), lambda i: (0, 0)),
                  pl.BlockSpec((d, LANES), lambda i: (0, 0))],
        out_specs=[pl.BlockSpec((tm, d), lambda i: (i, 0)),
                   pl.BlockSpec((tm // MOE_TT, N_EXPERTS, MOE_TT), lambda i: (i, 0, 0))],
        out_shape=[jax.ShapeDtypeStruct((n, d), jnp.bfloat16),
                   jax.ShapeDtypeStruct((n // MOE_TT, N_EXPERTS, MOE_TT), jnp.float32)],
        compiler_params=_cparams("arbitrary"),
        name="moe_router",
    )(x, norm_g.reshape(1, d), w)


def _affinity_bits(a):
    return lax.bitcast_convert_type(a, jnp.int32)


def _threshold_kernel(aff_ref, thr_ref, need_ref, *, cap, n_tiles):
    def count(pred, thr):
        def tile(c, acc):
            return acc + jnp.where(pred(_affinity_bits(aff_ref[c]), thr), 1.0, 0.0)
        acc = lax.fori_loop(0, n_tiles, tile, jnp.zeros((N_EXPERTS, MOE_TT), jnp.float32), unroll=8)
        return jnp.sum(acc, axis=1, keepdims=True)

    def bit(i, thr):
        cand = thr | jnp.left_shift(jnp.int32(1), 30 - i)
        return jnp.where(count(lambda b, t: b >= t, cand) >= cap, cand, thr)

    thr = lax.fori_loop(0, 31, bit, jnp.zeros((N_EXPERTS, 1), jnp.int32))
    thr_ref[...] = jnp.broadcast_to(thr, thr_ref.shape)
    need_ref[...] = jnp.broadcast_to(cap - count(lambda b, t: b > t, thr), need_ref.shape)


def moe_threshold(aff, cap):
    n_tiles = aff.shape[0]
    full = pl.BlockSpec((N_EXPERTS, LANES), lambda i: (0, 0))
    return pl.pallas_call(
        functools.partial(_threshold_kernel, cap=cap, n_tiles=n_tiles),
        grid=(1,),
        in_specs=[pl.BlockSpec(aff.shape, lambda i: (0, 0, 0))],
        out_specs=[full, full],
        out_shape=[jax.ShapeDtypeStruct((N_EXPERTS, LANES), jnp.int32),
                   jax.ShapeDtypeStruct((N_EXPERTS, LANES), jnp.float32)],
        compiler_params=_cparams("arbitrary"),
        name="moe_threshold",
    )(aff)


def _assign_kernel(aff_ref, thr_ref, need_ref, codet_ref, coden_ref, start_ref, total_ref, ties_ref, run_ref):
    @pl.when(pl.program_id(0) == 0)
    def _():
        ties_ref[...] = jnp.zeros_like(ties_ref)
        run_ref[...] = jnp.zeros_like(run_ref)

    f32 = jnp.float32
    bits = _affinity_bits(aff_ref[0])
    thr = thr_ref[:, :1]
    ii = lax.broadcasted_iota(jnp.int32, (MOE_TT, MOE_TT), 0)
    jj = lax.broadcasted_iota(jnp.int32, (MOE_TT, MOE_TT), 1)
    tri = jnp.where(ii <= jj, 1.0, 0.0).astype(jnp.bfloat16)
    eq = bits == thr
    eq_f = jnp.where(eq, 1.0, 0.0)
    tie_rank = jnp.dot(eq_f.astype(jnp.bfloat16), tri, preferred_element_type=f32) + ties_ref[:, :1]
    sel = jnp.logical_or(bits > thr, jnp.logical_and(eq, tie_rank <= need_ref[:, :1]))
    sel_f = jnp.where(sel, 1.0, 0.0)
    incl = jnp.dot(sel_f.astype(jnp.bfloat16), tri, preferred_element_type=f32)
    code = jnp.where(sel, incl - 1.0, -1.0)
    codet_ref[0] = code.astype(jnp.int32)
    padded = jnp.concatenate([code, jnp.zeros((LANES - N_EXPERTS, MOE_TT), f32)], axis=0)
    coden_ref[...] = padded.T.astype(jnp.int32)
    start_ref[0] = run_ref[...].astype(jnp.int32)
    taken = jnp.sum(sel_f, axis=1, keepdims=True)
    run_ref[...] = run_ref[...] + jnp.floor((taken + (MOE_ALIGN - 1)) * (1.0 / MOE_ALIGN)) * MOE_ALIGN
    total_ref[...] = run_ref[...].astype(jnp.int32)
    ties_ref[...] = ties_ref[...] + jnp.sum(eq_f, axis=1, keepdims=True)


def moe_assign(aff, thr, need):
    n_tiles = aff.shape[0]
    tile = pl.BlockSpec((1, N_EXPERTS, MOE_TT), lambda c: (c, 0, 0))
    full = pl.BlockSpec((N_EXPERTS, LANES), lambda c: (0, 0))
    return pl.pallas_call(
        _assign_kernel,
        grid=(n_tiles,),
        in_specs=[tile, full, full],
        out_specs=[tile, pl.BlockSpec((MOE_TT, LANES), lambda c: (c, 0)),
                   pl.BlockSpec((1, N_EXPERTS, LANES), lambda c: (c, 0, 0)), full],
        out_shape=[jax.ShapeDtypeStruct((n_tiles, N_EXPERTS, MOE_TT), jnp.int32),
                   jax.ShapeDtypeStruct((n_tiles * MOE_TT, LANES), jnp.int32),
                   jax.ShapeDtypeStruct((n_tiles, N_EXPERTS, LANES), jnp.int32),
                   jax.ShapeDtypeStruct((N_EXPERTS, LANES), jnp.int32)],
        scratch_shapes=[pltpu.VMEM((N_EXPERTS, LANES), jnp.float32)] * 2,
        compiler_params=_cparams("arbitrary"),
        name="moe_assign",
    )(aff, thr, need)


def _dispatch_kernel(start_ref, big_ref, h_ref, code_ref, aff_ref, init_hbm, x_hbm, buf_ref, sem, *, n_tiles):
    del init_hbm
    c = pl.program_id(0)
    cur = c % 2
    f32 = jnp.float32
    h = h_ref[...]
    code = code_ref[0]
    gate = aff_ref[0]

    def copies(tile, slot, first, width, group):
        return [pltpu.make_async_copy(
            buf_ref.at[slot, pl.ds(k * width, width)],
            x_hbm.at[first + k, pl.ds(_slot_start(start_ref, tile, first + k), width)],
            sem.at[slot, k]) for k in range(group)]

    def stage(first, width, group):
        slot = lax.broadcasted_iota(jnp.int32, (width, MOE_TT), 0)
        onehots = []
        for k in range(group):
            e = first + k
            hit = slot == code[e:e + 1, :]
            onehots.append(jnp.where(hit, 1.0, 0.0).astype(jnp.bfloat16))
            gwin = jnp.sum(jnp.where(hit, gate[e:e + 1, :], 0.0), axis=1, keepdims=True)
            buf_ref[cur, k * width:(k + 1) * width, D_MODEL:] = jnp.broadcast_to(gwin, (width, LANES))
        buf_ref[cur, :, :D_MODEL] = jnp.dot(jnp.concatenate(onehots, axis=0), h, preferred_element_type=f32)

    def wait_previous_small():
        prev = jnp.maximum(c - 1, 0)

        @pl.when(jnp.logical_and(c > 0, big_ref[prev] == 0))
        def _():
            for cp in copies(prev, 1 - cur, 0, MOE_W_SMALL, N_EXPERTS):
                cp.wait()

    @pl.when(big_ref[c] == 0)
    def _():
        stage(0, MOE_W_SMALL, N_EXPERTS)
        wait_previous_small()
        mine = copies(c, cur, 0, MOE_W_SMALL, N_EXPERTS)
        for cp in mine:
            cp.start()

        @pl.when(c == n_tiles - 1)
        def _():
            for cp in mine:
                cp.wait()

    @pl.when(big_ref[c] != 0)
    def _():
        wait_previous_small()
        group = MOE_ROWS // MOE_TT
        for first in range(0, N_EXPERTS, group):
            stage(first, MOE_TT, group)
            mine = copies(c, cur, first, MOE_TT, group)
            for cp in mine:
                cp.start()
            for cp in mine:
                cp.wait()


def _slot_start(start_ref, tile, expert):
    return pl.multiple_of(start_ref[tile * N_EXPERTS + expert], MOE_ALIGN)


def moe_dispatch(starts, big, h, code_t, aff, slots):
    n, d = h.shape
    n_tiles = n // MOE_TT
    tile = pl.BlockSpec((1, N_EXPERTS, MOE_TT), lambda c, s, b: (c, 0, 0))
    shape = (N_EXPERTS, slots, MOE_XW)
    return pl.pallas_call(
        functools.partial(_dispatch_kernel, n_tiles=n_tiles),
        grid_spec=pltpu.PrefetchScalarGridSpec(
            num_scalar_prefetch=2,
            grid=(n_tiles,),
            in_specs=[pl.BlockSpec((MOE_TT, d), lambda c, s, b: (c, 0)), tile, tile,
                      pl.BlockSpec(memory_space=pl.ANY)],
            out_specs=pl.BlockSpec(memory_space=pl.ANY),
            scratch_shapes=[pltpu.VMEM((2, MOE_ROWS, MOE_XW), jnp.float32),
                            pltpu.SemaphoreType.DMA((2, MOE_ROWS // MOE_W_SMALL))],
        ),
        out_shape=jax.ShapeDtypeStruct(shape, jnp.float32),
        input_output_aliases={5: 0},
        compiler_params=_cparams("arbitrary"),
        name="moe_dispatch",
    )(starts, big, h, code_t, aff, jnp.zeros(shape, jnp.float32))


def _expert_kernel(total_ref, x_ref, wg_ref, wu_ref, wd_ref, o_ref):
    f32 = jnp.float32
    used = pl.program_id(1) * MOE_TM < total_ref[pl.program_id(0)]

    @pl.when(used)
    def _():
        x = x_ref[0, :, :D_MODEL].astype(jnp.bfloat16)
        a = jnp.dot(x, wg_ref[0], preferred_element_type=f32)
        u = jnp.dot(x, wu_ref[0], preferred_element_type=f32)
        he = (a * jax.nn.sigmoid(a) * u).astype(jnp.bfloat16)
        o_ref[0] = jnp.dot(he, wd_ref[0], preferred_element_type=f32) * x_ref[0, :, D_MODEL:D_MODEL + 1]

    @pl.when(jnp.logical_not(used))
    def _():
        o_ref[0] = jnp.zeros_like(o_ref[0])


def expert_ffn(total, xd, w_gate, w_up, w_down):
    e, slots, _ = xd.shape
    d = D_MODEL
    tm = MOE_TM
    wspec = pl.BlockSpec((1, d, D_EXPERT), lambda i, j, t: (i, 0, 0))
    return pl.pallas_call(
        _expert_kernel,
        grid_spec=pltpu.PrefetchScalarGridSpec(
            num_scalar_prefetch=1,
            grid=(e, slots // tm),
            in_specs=[pl.BlockSpec((1, tm, MOE_XW), lambda i, j, t: (i, j, 0)),
                      wspec, wspec,
                      pl.BlockSpec((1, D_EXPERT, d), lambda i, j, t: (i, 0, 0))],
            out_specs=pl.BlockSpec((1, tm, d), lambda i, j, t: (i, j, 0)),
        ),
        out_shape=jax.ShapeDtypeStruct((e, slots, d), jnp.float32),
        compiler_params=_cparams("arbitrary", "arbitrary"),
        name="expert_ffn",
    )(total, xd, w_gate, w_up, w_down)


def _combine_kernel(start_ref, big_ref, x_ref, code_ref, y_hbm, o_ref, buf_ref, sem, *, n_tiles):
    c = pl.program_id(0)
    cur = c % 2
    f32 = jnp.float32
    bf = jnp.bfloat16
    code = code_ref[...]
    o_ref[...] = x_ref[...]

    def copies(tile, slot, first, width, group):
        return [pltpu.make_async_copy(
            y_hbm.at[first + k, pl.ds(_slot_start(start_ref, tile, first + k), width)],
            buf_ref.at[slot, pl.ds(k * width, width)],
            sem.at[slot, k]) for k in range(group)]

    def onehot(first, width, group):
        pieces = []
        if width < LANES:
            lane = lax.broadcasted_iota(jnp.int32, (MOE_TT, LANES), 1)
            for k in range(0, group, 2):
                e = first + k
                target = jnp.where(lane < width, code[:, e:e + 1], code[:, e + 1:e + 2] + width)
                pieces.append(jnp.where(target == lane, 1.0, 0.0).astype(bf))
        else:
            lane = lax.broadcasted_iota(jnp.int32, (MOE_TT, width), 1)
            for k in range(group):
                e = first + k
                pieces.append(jnp.where(code[:, e:e + 1] == lane, 1.0, 0.0).astype(bf))
        return jnp.concatenate(pieces, axis=1)

    def accumulate(oh):
        y = buf_ref[cur]
        y_hi = y.astype(bf)
        y_lo = (y - y_hi.astype(f32)).astype(bf)
        o_ref[...] += (jnp.dot(oh, y_hi, preferred_element_type=f32)
                       + jnp.dot(oh, y_lo, preferred_element_type=f32))

    def prefetch_next():
        nxt = jnp.minimum(c + 1, n_tiles - 1)

        @pl.when(jnp.logical_and(c + 1 < n_tiles, big_ref[nxt] == 0))
        def _():
            for cp in copies(nxt, 1 - cur, 0, MOE_W_SMALL, N_EXPERTS):
                cp.start()

    @pl.when(big_ref[c] == 0)
    def _():
        mine = copies(c, cur, 0, MOE_W_SMALL, N_EXPERTS)

        @pl.when(c == 0)
        def _():
            for cp in mine:
                cp.start()

        prefetch_next()
        oh = onehot(0, MOE_W_SMALL, N_EXPERTS)
        for cp in mine:
            cp.wait()
        accumulate(oh)

    @pl.when(big_ref[c] != 0)
    def _():
        prefetch_next()
        group = MOE_ROWS // MOE_TT
        for first in range(0, N_EXPERTS, group):
            mine = copies(c, cur, first, MOE_TT, group)
            for cp in mine:
                cp.start()
            oh = onehot(first, MOE_TT, group)
            for cp in mine:
                cp.wait()
            accumulate(oh)


def moe_combine(starts, big, x, code_n, y):
    n, d = x.shape
    row = pl.BlockSpec((MOE_TT, d), lambda c, s, b: (c, 0))
    return pl.pallas_call(
        functools.partial(_combine_kernel, n_tiles=n // MOE_TT),
        grid_spec=pltpu.PrefetchScalarGridSpec(
            num_scalar_prefetch=2,
            grid=(n // MOE_TT,),
            in_specs=[row, pl.BlockSpec((MOE_TT, LANES), lambda c, s, b: (c, 0)),
                      pl.BlockSpec(memory_space=pl.ANY)],
            out_specs=row,
            scratch_shapes=[pltpu.VMEM((2, MOE_ROWS, d), jnp.float32),
                            pltpu.SemaphoreType.DMA((2, MOE_ROWS // MOE_W_SMALL))],
        ),
        out_shape=jax.ShapeDtypeStruct((n, d), jnp.float32),
        compiler_params=_cparams("arbitrary"),
        name="moe_combine",
    )(starts, big, x, code_n, y)


def ec_moe(x, norm_g, w_router, w_gate, w_up, w_down):
    n_tok, _ = x.shape
    cap = EC_FACTOR * n_tok // N_EXPERTS
    h, aff = moe_router(x, norm_g, w_router)
    thr, need = moe_threshold(aff, cap)
    code_t, code_n, starts, total = moe_assign(aff, thr, need)
    starts = starts[:, :, 0]
    total = total[:, 0]
    ends = jnp.concatenate([starts[1:], total[None]], axis=0)
    big = (jnp.max(ends - starts, axis=1) > MOE_W_SMALL).astype(jnp.int32)
    starts = starts.reshape(-1)
    n_tiles = n_tok // MOE_TT
    slots = -(-(cap + MOE_ALIGN * n_tiles + MOE_TT) // MOE_TM) * MOE_TM
    xd = moe_dispatch(starts, big, h, code_t, aff, slots)
    y = expert_ffn(total, xd, w_gate, w_up, w_down)
    return moe_combine(starts, big, x, code_n, y)


def split_in_proj(w_in):
    w = w_in.astype(jnp.bfloat16)
    sizes = (LRU_WIDTH, LRU_WIDTH, MLA_Q_RANK, MLA_KV_RANK, MLA_ROPE,
             DIL_HEADS * DIL_HEAD_DIM, DIL_HEADS * DIL_HEAD_DIM, DIL_HEADS * DIL_HEAD_DIM,
             RET_HEADS * RET_QK, RET_HEADS * RET_QK, RET_HEADS * RET_V, RET_HEADS * RET_V,
             N_BRANCHES * D_MODEL)
    parts, off = [], 0
    for s in sizes:
        parts.append(w[:, off:off + s])
        off += s
    xa, ga, cq, ckv, kr, dq, dk, dv, rq, rk, rv, rg, zg = parts
    gw = DIL_GROUP_WIDTH
    grp = lambda a, g: a[:, g * gw:(g + 1) * gw]
    zeros = jnp.zeros((w.shape[0], Z_PAD), w.dtype)
    main = jnp.concatenate([zg, xa, ga, cq, ckv, kr, zeros, grp(dq, 0), grp(dk, 0), grp(dv, 0),
                            rq, rk, rv, rg], axis=1)
    dil = [jnp.concatenate([grp(dq, g), grp(dk, g), grp(dv, g)], axis=1) for g in range(1, len(DIL_PAIRS))]
    return main, dil


def mixer(x, batch, seq, norm_g, w_in, conv_w, conv_b, lru_gate_w, lru_gate_b, lru_lambda, mla_q_norm,
          mla_kv_norm, w_uq, w_ukv, ret_decay, w_branch, w_out, mla_tables, ret_tables):
    n = batch * seq
    w_main, w_dil = split_in_proj(w_in)
    z = norm_matmul(x, norm_g, w_main, jnp.bfloat16, min(1024, n), 1024)
    wg, gb = lru_gate_dense(lru_gate_w, lru_gate_b)
    h_fwd = lru_direction(z, conv_w, conv_b, wg, gb, lru_lambda, batch, seq, False)
    ya = lru_direction(z, conv_w, conv_b, wg, gb, lru_lambda, batch, seq, True, h_fwd)
    q, k, v = mla_project(z, mla_q_norm, mla_kv_norm, mla_weights(w_uq, w_ukv), mla_tables, batch, seq)
    yb = mla_attention(q, k, v, batch, seq)
    dil = [band_group(z.reshape(batch, 1, seq, Z_COLS), 0, 1, batch, seq, (CB_DQ, CB_DK, CB_DV))]
    pairs = DIL_HEADS_PER_GROUP // 2
    for g in range(1, len(DIL_PAIRS)):
        d = DIL_PAIRS[g][1]
        zd = norm_matmul_dil(x, norm_g, w_dil[g - 1], d, batch, seq, min(1024, seq))
        dil.append(band_group(zd, g, d, batch, seq, (0, pairs, 2 * pairs)))
    yd = retention(z, ret_decay, ret_tables, batch, seq)
    return merge_project(x, ya, yb, [o for o, _ in dil], [l for _, l in dil], yd, z,
                         w_branch.astype(jnp.bfloat16), w_out.astype(jnp.bfloat16), seq)


def trunk(x, norm_mix, w_in, conv_w, conv_b, lru_gate_w, lru_gate_b, lru_lambda, mla_q_norm, mla_kv_norm,
          w_uq, w_ukv, ret_decay, w_branch, w_out, norm_ffn, w_router, w_gate, w_up, w_down, norm_final):
    batch, seq, d = x.shape
    x = x.reshape(batch * seq, d)
    mla_tables = mla_rope_tables(seq)
    ret_tables = ret_rope_tables(seq)
    bf = jnp.bfloat16
    for l in range(norm_mix.shape[0]):
        x = mixer(x, batch, seq, norm_mix[l], w_in[l], conv_w[l], conv_b[l], lru_gate_w[l], lru_gate_b[l],
                  lru_lambda[l], mla_q_norm[l], mla_kv_norm[l], w_uq[l], w_ukv[l], ret_decay[l],
                  w_branch[l], w_out[l], mla_tables, ret_tables)
        x = ec_moe(x, norm_ffn[l], w_router[l], w_gate[l].astype(bf), w_up[l].astype(bf), w_down[l].astype(bf))
    return rmsnorm_pallas(x, norm_final, jnp.float32, min(1024, batch * seq)).reshape(batch, seq, d)


def kernel(x_prompt, x_sample, norm_mix, w_in, conv_w, conv_b, lru_gate_w, lru_gate_b, lru_lambda,
           mla_q_norm, mla_kv_norm, w_uq, w_ukv, ret_decay, w_branch, w_out, norm_ffn, w_router,
           w_gate, w_up, w_down, norm_final):
    args = (norm_mix, w_in, conv_w, conv_b, lru_gate_w, lru_gate_b, lru_lambda, mla_q_norm, mla_kv_norm,
            w_uq, w_ukv, ret_decay, w_branch, w_out, norm_ffn, w_router, w_gate, w_up, w_down, norm_final)
    return trunk(x_prompt, *args), trunk(x_sample, *args)
```

```python
import functools
import math

import numpy as np
import jax
import jax.numpy as jnp
from jax import lax
from jax.experimental import pallas as pl
from jax.experimental.pallas import tpu as pltpu

D_MODEL = 1024
DEPTH = 4
EPS = 1e-6
NEG_INF = -1e30
ROPE_BASE = 10000.0
N_BRANCHES = 4
LRU_WIDTH = 512
LRU_BLOCKS = 8
LRU_BLOCK = LRU_WIDTH // LRU_BLOCKS
CONV_WIDTH = 4
CONV_LEFT = 2
LRU_C = 8.0
MLA_HEADS = 8
MLA_NOPE = 64
MLA_ROPE = 32
MLA_V = 64
MLA_Q_RANK = 256
MLA_KV_RANK = 128
DIL_PAIRS = ((128, 1), (512, 4), (2048, 16))
DIL_HEADS_PER_GROUP = 8
DIL_HEADS = DIL_HEADS_PER_GROUP * len(DIL_PAIRS)
DIL_HEAD_DIM = 64
RET_HEADS = 4
RET_QK = 128
RET_V = 128
N_EXPERTS = 16
EC_FACTOR = 2
D_EXPERT = 1024

LANES = 128
VMEM_LIMIT_BYTES = 56 * 1024 * 1024

Z_COLS = 9216
Z_PAD = 96
CB_ZG = 0
CB_XA, CB_GA = 32, 36
CB_MLA = 40
CB_DQ, CB_DK, CB_DV = 44, 48, 52
CB_RQ, CB_RK, CB_RV, CB_RG = 56, 60, 64, 68
DIL_GROUP_WIDTH = DIL_HEADS_PER_GROUP * DIL_HEAD_DIM


def _cparams(*sem):
    return pltpu.CompilerParams(dimension_semantics=sem, vmem_limit_bytes=VMEM_LIMIT_BYTES)


def _norm_matmul_kernel(x_ref, g_ref, w_ref, o_ref, h_ref):
    @pl.when(pl.program_id(1) == 0)
    def _():
        x = x_ref[...]
        y = x * lax.rsqrt(jnp.mean(x * x, axis=-1, keepdims=True) + EPS)
        h_ref[...] = (y * g_ref[...]).astype(h_ref.dtype)

    o_ref[...] = jnp.dot(h_ref[...], w_ref[...], preferred_element_type=jnp.float32).astype(o_ref.dtype)


def norm_matmul(x, g, w, out_dtype, tm, tn):
    n, d = x.shape
    c = w.shape[1]
    return pl.pallas_call(
        _norm_matmul_kernel,
        grid=(n // tm, c // tn),
        in_specs=[
            pl.BlockSpec((tm, d), lambda i, j: (i, 0)),
            pl.BlockSpec((1, d), lambda i, j: (0, 0)),
            pl.BlockSpec((d, tn), lambda i, j: (0, j)),
        ],
        out_specs=pl.BlockSpec((tm, tn), lambda i, j: (i, j)),
        out_shape=jax.ShapeDtypeStruct((n, c), out_dtype),
        scratch_shapes=[pltpu.VMEM((tm, d), jnp.bfloat16)],
        compiler_params=_cparams("arbitrary", "arbitrary"),
        name="norm_matmul",
    )(x, g.reshape(1, d), w)


def _norm_matmul_dil_kernel(x_ref, g_ref, w_ref, o_ref, hf_ref, hp_ref, *, dil):
    x = x_ref[...]
    y = x * lax.rsqrt(jnp.mean(x * x, axis=-1, keepdims=True) + EPS)
    y = y * g_ref[...]
    rows = x.shape[0] // dil
    for c in range(x.shape[1] // LANES):
        cols = slice(c * LANES, (c + 1) * LANES)
        hf_ref[c] = y[:, cols]
        for r in range(dil):
            hp_ref[r * rows:(r + 1) * rows, cols] = hf_ref[c, pl.ds(r, rows, stride=dil), :].astype(hp_ref.dtype)
    out = jnp.dot(hp_ref[...], w_ref[...], preferred_element_type=jnp.float32)
    for r in range(dil):
        o_ref[0, r] = out[r * rows:(r + 1) * rows, :].astype(o_ref.dtype)


def norm_matmul_dil(x, g, w, dil, batch, seq, tm):
    n, d = x.shape
    c = w.shape[1]
    n_t = seq // tm
    return pl.pallas_call(
        functools.partial(_norm_matmul_dil_kernel, dil=dil),
        grid=(n // tm,),
        in_specs=[
            pl.BlockSpec((tm, d), lambda i: (i, 0)),
            pl.BlockSpec((1, d), lambda i: (0, 0)),
            pl.BlockSpec((d, c), lambda i: (0, 0)),
        ],
        out_specs=pl.BlockSpec((1, dil, tm // dil, c), lambda i: (i // n_t, 0, i % n_t, 0)),
        out_shape=jax.ShapeDtypeStruct((batch, dil, seq // dil, c), jnp.bfloat16),
        scratch_shapes=[pltpu.VMEM((d // LANES, tm, LANES), jnp.float32), pltpu.VMEM((tm, d), jnp.bfloat16)],
        compiler_params=_cparams("arbitrary"),
        name=f"norm_matmul_dil{dil}",
    )(x, g.reshape(1, d), w)


def _rmsnorm_kernel(x_ref, g_ref, o_ref):
    x = x_ref[...]
    y = x * lax.rsqrt(jnp.mean(x * x, axis=-1, keepdims=True) + EPS)
    o_ref[...] = (y * g_ref[...]).astype(o_ref.dtype)


def rmsnorm_pallas(x, g, out_dtype, tm):
    n, d = x.shape
    return pl.pallas_call(
        _rmsnorm_kernel,
        grid=(n // tm,),
        in_specs=[pl.BlockSpec((tm, d), lambda i: (i, 0)), pl.BlockSpec((1, d), lambda i: (0, 0))],
        out_specs=pl.BlockSpec((tm, d), lambda i: (i, 0)),
        out_shape=jax.ShapeDtypeStruct((n, d), out_dtype),
        compiler_params=_cparams("arbitrary"),
        name="rmsnorm",
    )(x, g.reshape(1, d))


LRU_TC = 256
LRU_HALO = 16


def _softplus(x):
    return jnp.maximum(x, 0.0) + jnp.log(1.0 + jnp.exp(-jnp.abs(x)))


def _gelu_tanh(x):
    return 0.5 * x * (1.0 + jnp.tanh(math.sqrt(2.0 / math.pi) * (x + 0.044715 * (x * x * x))))


def _lru_scan_chunk(a, b, reverse):
    n = a.shape[0]
    row = lax.broadcasted_iota(jnp.int32, a.shape, 0)
    s = 1
    while s < n:
        if reverse:
            keep = row < (n - s)
            a_s = pltpu.roll(a, n - s, 0)
            b_s = pltpu.roll(b, n - s, 0)
        else:
            keep = row >= s
            a_s = pltpu.roll(a, s, 0)
            b_s = pltpu.roll(b, s, 0)
        b = jnp.where(keep, a * b_s + b, b)
        a = jnp.where(keep, a * a_s, a)
        s *= 2
    return a, b


def _lru_kernel(*refs, reverse, n_t):
    if reverse:
        (xp_ref, xc_ref, xn_ref, cw_ref, cb_ref, wg_ref, gb_ref, lam_ref, hf_ref, ga_ref,
         o_ref, carry_ref) = refs
    else:
        (xp_ref, xc_ref, xn_ref, cw_ref, cb_ref, wg_ref, gb_ref, lam_ref, o_ref, carry_ref) = refs
    step = pl.program_id(1)
    t = (n_t - 1 - step) if reverse else step

    @pl.when(step == 0)
    def _():
        carry_ref[...] = jnp.zeros_like(carry_ref)

    prev = jnp.where(t > 0, xp_ref[...].astype(jnp.float32), 0.0)
    nxt = jnp.where(t < n_t - 1, xn_ref[...].astype(jnp.float32), 0.0)
    win = jnp.concatenate([prev, xc_ref[...].astype(jnp.float32), nxt], axis=0)
    xc = cb_ref[...]
    for k in range(CONV_WIDTH):
        lo = LRU_HALO - CONV_LEFT + k
        xc = xc + cw_ref[k:k + 1, :] * win[lo:lo + LRU_TC, :]
    gl = jnp.dot(xc.astype(jnp.bfloat16), wg_ref[0], preferred_element_type=jnp.float32) + gb_ref[0]
    r = jax.nn.sigmoid(gl[:, :LRU_WIDTH])
    i = jax.nn.sigmoid(gl[:, LRU_WIDTH:])
    log_a = (-LRU_C) * r * _softplus(-lam_ref[0])
    a = jnp.exp(log_a)
    b = jnp.sqrt(1.0 - jnp.exp(2.0 * log_a)) * i * xc
    a_cum, b_cum = _lru_scan_chunk(a, b, reverse)
    h = b_cum + a_cum * carry_ref[0:1, :]
    last = 0 if reverse else LRU_TC - 1
    carry_ref[0:1, :] = h[last:last + 1, :]
    if reverse:
        o_ref[...] = (_gelu_tanh(ga_ref[...].astype(jnp.float32)) * (hf_ref[...] + h)).astype(o_ref.dtype)
    else:
        o_ref[...] = h


def lru_direction(z, conv_w, conv_b, wg, gb, lam, batch, seq, reverse, h_fwd=None):
    n_t = seq // LRU_TC
    per_halo = LRU_TC // LRU_HALO
    n_halo = batch * seq // LRU_HALO
    d = 1 if reverse else 0

    def tt(s):
        return (n_t - 1 - s) if reverse else s

    xa_col = CB_XA * LANES // LRU_WIDTH
    ga_col = CB_GA * LANES // LRU_WIDTH

    def cur(b, s):
        return (b * n_t + tt(s), 0)

    def prev(b, s):
        return (jnp.maximum((b * n_t + tt(s)) * per_halo - 1, 0), xa_col)

    def nxt(b, s):
        return (jnp.minimum((b * n_t + tt(s) + 1) * per_halo, n_halo - 1), xa_col)

    const2 = lambda b, s: (0, 0)
    in_specs = [
        pl.BlockSpec((LRU_HALO, LRU_WIDTH), prev),
        pl.BlockSpec((LRU_TC, LRU_WIDTH), lambda b, s: (b * n_t + tt(s), xa_col)),
        pl.BlockSpec((LRU_HALO, LRU_WIDTH), nxt),
        pl.BlockSpec((CONV_WIDTH, LRU_WIDTH), const2),
        pl.BlockSpec((1, LRU_WIDTH), const2),
        pl.BlockSpec((1, LRU_WIDTH, 2 * LRU_WIDTH), lambda b, s: (d, 0, 0)),
        pl.BlockSpec((1, 1, 2 * LRU_WIDTH), lambda b, s: (d, 0, 0)),
        pl.BlockSpec((1, 1, LRU_WIDTH), lambda b, s: (d, 0, 0)),
    ]
    args = [z, z, z, conv_w, conv_b.reshape(1, LRU_WIDTH), wg, gb, lam.reshape(2, 1, LRU_WIDTH)]
    if reverse:
        in_specs += [pl.BlockSpec((LRU_TC, LRU_WIDTH), cur),
                     pl.BlockSpec((LRU_TC, LRU_WIDTH), lambda b, s: (b * n_t + tt(s), ga_col))]
        args += [h_fwd, z]
        out_dtype = jnp.bfloat16
    else:
        out_dtype = jnp.float32
    return pl.pallas_call(
        functools.partial(_lru_kernel, reverse=reverse, n_t=n_t),
        grid=(batch, n_t),
        in_specs=in_specs,
        out_specs=pl.BlockSpec((LRU_TC, LRU_WIDTH), cur),
        out_shape=jax.ShapeDtypeStruct((batch * seq, LRU_WIDTH), out_dtype),
        scratch_shapes=[pltpu.VMEM((8, LRU_WIDTH), jnp.float32)],
        compiler_params=_cparams("arbitrary", "arbitrary"),
        name="lru_bwd" if reverse else "lru_fwd",
    )(*args)


def lru_gate_dense(gate_w, gate_b):
    eye = jnp.eye(LRU_BLOCKS, dtype=gate_w.dtype)
    dense = jnp.einsum('dgnij,nm->dgnimj', gate_w, eye).reshape(2, 2, LRU_WIDTH, LRU_WIDTH)
    wg = jnp.concatenate([dense[:, 0], dense[:, 1]], axis=-1).astype(jnp.bfloat16)
    gb = jnp.concatenate([gate_b[:, 0], gate_b[:, 1]], axis=-1).reshape(2, 1, 2 * LRU_WIDTH)
    return wg, gb


MLA_TM = 512
MLA_TQ = 2048
MLA_TK = 256
MLA_DP = 128


def _mla_proj_kernel(z_ref, qn_ref, kn_ref, wqa_ref, wqb_ref, wk_ref, wv_ref, ea_ref, eb_ref,
                     cos_ref, sin_ref, q_ref, k_ref, v_ref):
    z = z_ref[...]
    cq = z[:, :MLA_Q_RANK].astype(jnp.float32)
    ckv = z[:, MLA_Q_RANK:MLA_Q_RANK + MLA_KV_RANK].astype(jnp.float32)
    kr = z[:, MLA_Q_RANK + MLA_KV_RANK:]
    cqn = (cq * lax.rsqrt(jnp.mean(cq * cq, axis=-1, keepdims=True) + EPS) * qn_ref[...]).astype(jnp.bfloat16)
    ckn = (ckv * lax.rsqrt(jnp.mean(ckv * ckv, axis=-1, keepdims=True) + EPS) * kn_ref[...]).astype(jnp.bfloat16)
    cos = cos_ref[...]
    sin = sin_ref[...]
    f32 = jnp.float32
    k_rope = (jnp.dot(kr, ea_ref[...], preferred_element_type=f32) * cos
              + jnp.dot(kr, eb_ref[...], preferred_element_type=f32) * sin)
    scale = (MLA_NOPE + MLA_ROPE) ** -0.5 * math.log2(math.e)
    row = lax.broadcasted_iota(jnp.int32, (MLA_DP, MLA_TM), 0)
    for h in range(MLA_HEADS):
        qa = jnp.dot(cqn, wqa_ref[h], preferred_element_type=f32)
        qb = jnp.dot(cqn, wqb_ref[h], preferred_element_type=f32)
        q_ref[0, h] = ((qa * cos + qb * sin) * scale).astype(q_ref.dtype)
        k_ref[0, h] = (jnp.dot(ckn, wk_ref[h], preferred_element_type=f32) + k_rope).astype(k_ref.dtype)
        vt = lax.dot_general(wv_ref[h], ckn, (((1,), (1,)), ((), ())), preferred_element_type=f32)
        vt = jnp.where(row == _mla_ones_row(h), 1.0, vt)
        for c in range(MLA_TM // MLA_TK):
            v_ref[0, h, c] = vt[:, c * MLA_TK:(c + 1) * MLA_TK].astype(v_ref.dtype)


def _mla_ones_row(head):
    return MLA_V if head % 2 == 0 else 0


def _rot_half_matrix(n):
    half = n // 2
    r = np.zeros((n, n), np.float32)
    for j in range(half):
        r[half + j, j] = -1.0
        r[j, half + j] = 1.0
    return r


def mla_weights(w_uq, w_ukv):
    rot = jnp.asarray(_rot_half_matrix(MLA_ROPE))
    wq = jnp.transpose(w_uq, (1, 0, 2))
    pad = lambda a, lo, hi: jnp.pad(a, ((0, 0), (0, 0), (lo, hi)))
    wqa = pad(wq, 0, MLA_DP - MLA_NOPE - MLA_ROPE)
    wqb = pad(jnp.einsum('hrd,de->hre', wq[..., MLA_NOPE:], rot), MLA_NOPE, MLA_DP - MLA_NOPE - MLA_ROPE)
    wkv = jnp.transpose(w_ukv, (1, 0, 2))
    wk = pad(wkv[..., :MLA_NOPE], 0, MLA_DP - MLA_NOPE)
    wv_even = pad(wkv[..., MLA_NOPE:], 0, MLA_V)
    wv_odd = pad(wkv[..., MLA_NOPE:], MLA_V, 0)
    wv = jnp.where((jnp.arange(MLA_HEADS) % 2 == 0)[:, None, None], wv_even, wv_odd)
    wv = jnp.transpose(wv, (0, 2, 1))
    ea = np.zeros((LANES, MLA_DP), np.float32)
    for j in range(MLA_ROPE):
        ea[j, MLA_NOPE + j] = 1.0
    eb = np.zeros((LANES, MLA_DP), np.float32)
    eb[:MLA_ROPE, MLA_NOPE:MLA_NOPE + MLA_ROPE] = _rot_half_matrix(MLA_ROPE)
    bf = jnp.bfloat16
    return (wqa.astype(bf), wqb.astype(bf), wk.astype(bf), wv.astype(bf),
            jnp.asarray(ea, bf), jnp.asarray(eb, bf))


def mla_rope_tables(seq):
    half = MLA_ROPE // 2
    inv = ROPE_BASE ** (-jnp.arange(half, dtype=jnp.float32) / half)
    ang = jnp.arange(seq, dtype=jnp.float32)[:, None] * inv[None, :]
    ones = jnp.ones((seq, MLA_NOPE), jnp.float32)
    zeros = jnp.zeros((seq, MLA_DP - MLA_NOPE - MLA_ROPE), jnp.float32)
    cos = jnp.concatenate([ones, jnp.cos(ang), jnp.cos(ang), zeros], axis=1)
    sin = jnp.concatenate([0.0 * ones, jnp.sin(ang), jnp.sin(ang), zeros], axis=1)
    return cos, sin


def mla_project(z, q_norm, kv_norm, weights, tables, batch, seq):
    wqa, wqb, wk, wv, ea, eb = weights
    cos, sin = tables
    n_t = seq // MLA_TM
    hshape = (batch, MLA_HEADS, seq, MLA_DP)
    c3 = lambda b, t: (0, 0, 0)
    c2 = lambda b, t: (0, 0)
    hspec = pl.BlockSpec((1, MLA_HEADS, MLA_TM, MLA_DP), lambda b, t: (b, 0, t, 0))
    return pl.pallas_call(
        _mla_proj_kernel,
        grid=(batch, n_t),
        in_specs=[
            pl.BlockSpec((MLA_TM, 4 * LANES), lambda b, t: (b * n_t + t, CB_MLA // 4)),
            pl.BlockSpec((1, MLA_Q_RANK), c2),
            pl.BlockSpec((1, MLA_KV_RANK), c2),
            pl.BlockSpec((MLA_HEADS, MLA_Q_RANK, MLA_DP), c3),
            pl.BlockSpec((MLA_HEADS, MLA_Q_RANK, MLA_DP), c3),
            pl.BlockSpec((MLA_HEADS, MLA_KV_RANK, MLA_DP), c3),
            pl.BlockSpec((MLA_HEADS, MLA_KV_RANK, MLA_DP), c3),
            pl.BlockSpec((LANES, MLA_DP), c2),
            pl.BlockSpec((LANES, MLA_DP), c2),
            pl.BlockSpec((MLA_TM, MLA_DP), lambda b, t: (t, 0)),
            pl.BlockSpec((MLA_TM, MLA_DP), lambda b, t: (t, 0)),
        ],
        out_specs=[hspec, hspec,
                   pl.BlockSpec((1, MLA_HEADS, MLA_TM // MLA_TK, MLA_DP, MLA_TK), lambda b, t: (b, 0, t, 0, 0))],
        out_shape=[jax.ShapeDtypeStruct(hshape, jnp.bfloat16)] * 2
        + [jax.ShapeDtypeStruct((batch, MLA_HEADS, seq // MLA_TK, MLA_DP, MLA_TK), jnp.bfloat16)],
        compiler_params=_cparams("arbitrary", "arbitrary"),
        name="mla_project",
    )(z, q_norm.reshape(1, -1), kv_norm.reshape(1, -1), wqa, wqb, wk, wv, ea, eb, cos, sin)


def _mla_attn_kernel(q_ref, k_ref, vt_ref, o_ref, acc_ref, *, n_kv):
    acc_ref[...] = jnp.zeros_like(acc_ref)
    f32 = jnp.float32
    nt = (((1,), (1,)), ((), ()))

    def body(j, carry):
        rows = pl.ds(pl.multiple_of(j * MLA_TK, MLA_TK), MLA_TK)
        new = []
        for hh in range(2):
            m_prev = carry[hh]
            s = lax.dot_general(k_ref[0, hh, rows, :], q_ref[0, hh], nt, preferred_element_type=f32)
            m_next = jnp.maximum(m_prev, jnp.max(s, axis=0, keepdims=True))
            p = jnp.exp2(s - m_next)
            alpha = jnp.exp2(m_prev - m_next)
            new.append(m_next)
            acc_ref[hh] = alpha * acc_ref[hh] + jnp.dot(vt_ref[0, hh, j], p.astype(jnp.bfloat16),
                                                         preferred_element_type=f32)
        return tuple(new)

    m0 = jnp.full((1, MLA_TQ), NEG_INF, f32)
    lax.fori_loop(0, n_kv, body, (m0, m0), unroll=min(4, n_kv))
    acc_a, acc_b = acc_ref[0], acc_ref[1]
    l_a = acc_a[_mla_ones_row(0):_mla_ones_row(0) + 1, :]
    l_b = acc_b[_mla_ones_row(1):_mla_ones_row(1) + 1, :]
    row = lax.broadcasted_iota(jnp.int32, acc_a.shape, 0)
    out_t = jnp.where(row < MLA_V, acc_a / l_a, acc_b / l_b)
    o_ref[...] = out_t.T.astype(o_ref.dtype)


def mla_attention(q, k, vt, batch, seq):
    n_q = seq // MLA_TQ
    n_kv = seq // MLA_TK
    qspec = pl.BlockSpec((1, 2, MLA_TQ, MLA_DP), lambda b, hp, i: (b, hp, i, 0))
    kspec = pl.BlockSpec((1, 2, seq, MLA_DP), lambda b, hp, i: (b, hp, 0, 0))
    vspec = pl.BlockSpec((1, 2, n_kv, MLA_DP, MLA_TK), lambda b, hp, i: (b, hp, 0, 0, 0))
    return pl.pallas_call(
        functools.partial(_mla_attn_kernel, n_kv=n_kv),
        grid=(batch, MLA_HEADS // 2, n_q),
        in_specs=[qspec, kspec, vspec],
        out_specs=pl.BlockSpec((MLA_TQ, LANES), lambda b, hp, i: (b * n_q + i, hp)),
        out_shape=jax.ShapeDtypeStruct((batch * seq, MLA_HEADS * MLA_V), jnp.bfloat16),
        scratch_shapes=[pltpu.VMEM((2, MLA_DP, MLA_TQ), jnp.float32)],
        compiler_params=_cparams("arbitrary", "arbitrary", "arbitrary"),
        name="mla_attention",
    )(q, k, vt)


BAND_Q = 128
BAND_W = 256
BAND_RADIUS = 64
BAND_ROWS_PER_STEP = 2048


def band_bias_table(group, dil):
    n = DIL_HEADS
    slopes = np.asarray([2.0 ** (-8.0 * (h + 1) / n) for h in range(n)], np.float32)
    slopes = slopes[group * DIL_HEADS_PER_GROUP:(group + 1) * DIL_HEADS_PER_GROUP]
    iq = np.arange(BAND_Q)[:, None]
    ik = np.arange(BAND_W)[None, :]
    tabs = []
    for d in range(3):
        dist = np.abs(d * BAND_RADIUS + iq - ik)
        bias = -slopes[:, None, None] * (dil * dist).astype(np.float32)[None]
        tabs.append(np.where((dist <= BAND_RADIUS)[None], bias, np.float32(NEG_INF)))
    return jnp.asarray(np.stack(tabs).astype(np.float32))


def _band_kernel(q_ref, k_ref, v_ref, bias_ref, o_ref, lse_ref, *, length):
    lane = lax.broadcasted_iota(jnp.int32, (1, LANES), 1)
    first = lane < DIL_HEAD_DIM
    scale = DIL_HEAD_DIM ** -0.5

    n_blocks = length // BAND_Q
    n_res = q_ref.shape[1]

    def body(idx, carry):
        rr = idx // n_blocks
        q0 = pl.multiple_of((idx % n_blocks) * BAND_Q, BAND_Q)
        start = pl.multiple_of(jnp.clip(q0 - BAND_RADIUS, 0, length - BAND_W), BAND_RADIUS)
        didx = (q0 - start) // BAND_RADIUS
        q = q_ref[0, rr, pl.ds(q0, BAND_Q), :]
        kw = k_ref[0, rr, pl.ds(start, BAND_W), :]
        vw = v_ref[0, rr, pl.ds(start, BAND_W), :]
        outs, lses = [], []
        for hh in range(2):
            sel = first if hh == 0 else jnp.logical_not(first)
            qh = jnp.where(sel, q, jnp.zeros_like(q))
            s = lax.dot_general(qh, kw, (((1,), (1,)), ((), ())), preferred_element_type=jnp.float32)
            s = s * scale + bias_ref[didx, hh]
            m = jnp.max(s, axis=1, keepdims=True)
            e = jnp.exp(s - m)
            den = jnp.sum(e, axis=1, keepdims=True)
            p = (e / den).astype(jnp.bfloat16)
            outs.append(jnp.dot(p, vw, preferred_element_type=jnp.float32))
            lses.append(m + jnp.log(den))
        o_ref[0, rr, pl.ds(q0, BAND_Q), :] = jnp.where(first, outs[0], outs[1]).astype(o_ref.dtype)
        lse_ref[0, rr, pl.ds(q0, BAND_Q), :] = jnp.where(first, lses[0], lses[1])
        return carry

    lax.fori_loop(0, n_res * n_blocks, body, 0, unroll=min(16, n_res * n_blocks))


def band_group(src, group, dil, batch, seq, col_blocks):
    length = seq // dil
    pairs = DIL_HEADS_PER_GROUP // 2
    n_res = max(1, min(dil, BAND_ROWS_PER_STEP // length))

    def zspec(cb):
        return pl.BlockSpec((1, n_res, length, LANES), lambda b, r, hp: (b, r, 0, cb + hp))

    ospec = pl.BlockSpec((1, n_res, length, LANES), lambda b, r, hp: (b, r, 0, hp))
    oshape = (batch, dil, length, DIL_GROUP_WIDTH)
    return pl.pallas_call(
        functools.partial(_band_kernel, length=length),
        grid=(batch, dil // n_res, pairs),
        in_specs=[zspec(col_blocks[0]), zspec(col_blocks[1]), zspec(col_blocks[2]),
                  pl.BlockSpec((3, 2, BAND_Q, BAND_W), lambda b, r, hp: (0, hp, 0, 0))],
        out_specs=[ospec, ospec],
        out_shape=[jax.ShapeDtypeStruct(oshape, jnp.bfloat16), jax.ShapeDtypeStruct(oshape, jnp.float32)],
        compiler_params=_cparams("arbitrary", "arbitrary", "arbitrary"),
        name=f"band_attention_g{group}",
    )(src, src, src, band_bias_table(group, dil))


RET_C = 256


def _ret_kernel(lg_ref, q_ref, k_ref, v_ref, g_ref, cos_ref, sin_ref, o_ref, qs_ref, ks_ref, o1_ref, *, n_c):
    h = pl.program_id(1)
    lgf = lg_ref[0, h]
    lgb = lg_ref[1, h]
    c = RET_C
    f32 = jnp.float32
    bf = jnp.bfloat16
    ii = lax.broadcasted_iota(jnp.int32, (c, c), 0)
    jj = lax.broadcasted_iota(jnp.int32, (c, c), 1)
    diff = (ii - jj).astype(f32)
    decay = jnp.where(diff >= 0.0, jnp.exp(lgf * jnp.maximum(diff, 0.0)), jnp.exp(lgb * jnp.maximum(-diff, 0.0)))
    idx = lax.broadcasted_iota(jnp.int32, (c, 1), 0).astype(f32)
    xi_f = jnp.exp(lgf * (idx + 1.0))
    zeta_f = jnp.exp(lgf * (c - 1.0 - idx))
    xi_b = jnp.exp(lgb * (c - idx))
    zeta_b = jnp.exp(lgb * idx)
    cd_f = jnp.exp(lgf * c)
    cd_b = jnp.exp(lgb * c)
    kscale = RET_QK ** -0.5

    def rope(x, rows):
        return x * cos_ref[rows, :] + pltpu.roll(x, RET_QK // 2, 1) * sin_ref[rows, :]

    def fwd(n, state):
        rows = pl.ds(pl.multiple_of(n * c, c), c)
        q = rope(q_ref[0, rows, :].astype(f32), rows)
        k = rope(k_ref[0, rows, :].astype(f32), rows) * kscale
        v = v_ref[0, rows, :]
        qs_ref[rows, :] = q
        ks_ref[rows, :] = k
        qb = q.astype(bf)
        s = lax.dot_general(qb, k.astype(bf), (((1,), (1,)), ((), ())), preferred_element_type=f32) * decay
        o = jnp.dot(s.astype(bf), v, preferred_element_type=f32)
        o = o + xi_f * jnp.dot(qb, state.astype(bf), preferred_element_type=f32)
        o1_ref[rows, :] = o
        kz = (k * zeta_f).T.astype(bf)
        return cd_f * state + jnp.dot(kz, v, preferred_element_type=f32)

    lax.fori_loop(0, n_c, fwd, jnp.zeros((RET_QK, RET_V), f32), unroll=min(8, n_c))

    def bwd(step, state):
        n = n_c - 1 - step
        rows = pl.ds(pl.multiple_of(n * c, c), c)
        q = qs_ref[rows, :]
        k = ks_ref[rows, :]
        v = v_ref[0, rows, :]
        of = o1_ref[rows, :] + xi_b * jnp.dot(q.astype(bf), state.astype(bf), preferred_element_type=f32)
        mu = jnp.mean(of, axis=-1, keepdims=True)
        var = jnp.mean(jnp.square(of - mu), axis=-1, keepdims=True)
        of = (of - mu) * lax.rsqrt(var + EPS)
        g = g_ref[0, rows, :].astype(f32)
        o_ref[0, rows, :] = (g * jax.nn.sigmoid(g) * of).astype(o_ref.dtype)
        kz = (k * zeta_b).T.astype(bf)
        return cd_b * state + jnp.dot(kz, v, preferred_element_type=f32)

    lax.fori_loop(0, n_c, bwd, jnp.zeros((RET_QK, RET_V), f32), unroll=min(8, n_c))


def ret_rope_tables(seq):
    half = RET_QK // 2
    inv = ROPE_BASE ** (-jnp.arange(half, dtype=jnp.float32) / half)
    ang = jnp.arange(seq, dtype=jnp.float32)[:, None] * inv[None, :]
    cos = jnp.concatenate([jnp.cos(ang), jnp.cos(ang)], axis=1)
    sin = jnp.concatenate([-jnp.sin(ang), jnp.sin(ang)], axis=1)
    return cos, sin


def retention(z, ret_decay, tables, batch, seq):
    zv = z.reshape(batch, seq, Z_COLS)
    log_gamma = jax.nn.log_sigmoid(ret_decay.astype(jnp.float32))
    cos, sin = tables

    def zspec(cb):
        return pl.BlockSpec((1, seq, LANES), lambda b, h: (b, 0, cb + h))

    tspec = pl.BlockSpec((seq, RET_QK), lambda b, h: (0, 0))
    out = pl.pallas_call(
        functools.partial(_ret_kernel, n_c=seq // RET_C),
        grid=(batch, RET_HEADS),
        in_specs=[pl.BlockSpec(memory_space=pltpu.SMEM),
                  zspec(CB_RQ), zspec(CB_RK), zspec(CB_RV), zspec(CB_RG), tspec, tspec],
        out_specs=pl.BlockSpec((1, seq, LANES), lambda b, h: (b, 0, h)),
        out_shape=jax.ShapeDtypeStruct((batch, seq, RET_HEADS * RET_V), jnp.bfloat16),
        scratch_shapes=[pltpu.VMEM((seq, RET_QK), jnp.float32)] * 3,
        compiler_params=_cparams("arbitrary", "arbitrary"),
        name="retention",
    )(log_gamma, zv, zv, zv, zv, cos, sin)
    return out.reshape(batch * seq, RET_HEADS * RET_V)


MERGE_TM = 512


def _merge_kernel(x_ref, ya_ref, yb_ref, o0_ref, o1_ref, o2_ref, l0_ref, l1_ref, l2_ref, yd_ref,
                  zg_ref, wb_ref, wo_ref, out_ref, tok_ref):
    f32 = jnp.float32

    def token_order(slot, src_ref):
        dil, rows, width = src_ref.shape[1:]
        tiles = width // LANES
        for r in range(dil):
            v = src_ref[0, r].astype(f32)
            for c in range(tiles):
                tok_ref[slot * tiles + c, pl.ds(r, rows, stride=dil), :] = v[:, c * LANES:(c + 1) * LANES]
        return jnp.concatenate([tok_ref[slot * tiles + c] for c in range(tiles)], axis=1)

    l0 = l0_ref[0, 0]
    l1, l2 = token_order(0, l1_ref), token_order(1, l2_ref)
    o1, o2 = token_order(2, o1_ref), token_order(3, o2_ref)
    m = jnp.maximum(jnp.maximum(l0, l1), l2)
    e0, e1, e2 = jnp.exp(l0 - m), jnp.exp(l1 - m), jnp.exp(l2 - m)
    inv = 1.0 / (e0 + e1 + e2)
    yc = ((e0 * inv) * o0_ref[0, 0].astype(f32) + (e1 * inv) * o1 + (e2 * inv) * o2).astype(jnp.bfloat16)
    merged = None
    for i, y in enumerate((ya_ref[...], yb_ref[...], yc, yd_ref[...])):
        gate = jax.nn.sigmoid(zg_ref[:, i * D_MODEL:(i + 1) * D_MODEL].astype(f32))
        term = gate * jnp.dot(y, wb_ref[i], preferred_element_type=f32)
        merged = term if merged is None else merged + term
    out_ref[...] = x_ref[...] + jnp.dot(merged.astype(jnp.bfloat16), wo_ref[...], preferred_element_type=f32)


def merge_project(x, ya, yb, dil_o, dil_lse, yd, z, w_branch, w_out, seq):
    n = x.shape[0]
    tm = MERGE_TM
    n_t = seq // tm
    width = 4 * LANES
    row = lambda i: (i, 0)
    bspec = pl.BlockSpec((tm, width), row)

    def dspec(dil):
        return pl.BlockSpec((1, dil, tm // dil, width), lambda i: (i // n_t, 0, i % n_t, 0))

    dspecs = [dspec(d) for _, d in DIL_PAIRS]
    return pl.pallas_call(
        _merge_kernel,
        grid=(n // tm,),
        in_specs=[pl.BlockSpec((tm, D_MODEL), row), bspec, bspec, *dspecs, *dspecs, bspec,
                  pl.BlockSpec((tm, N_BRANCHES * D_MODEL), lambda i: (i, CB_ZG * LANES // (N_BRANCHES * D_MODEL))),
                  pl.BlockSpec((N_BRANCHES, width, D_MODEL), lambda i: (0, 0, 0)),
                  pl.BlockSpec((D_MODEL, D_MODEL), lambda i: (0, 0))],
        out_specs=pl.BlockSpec((tm, D_MODEL), row),
        out_shape=jax.ShapeDtypeStruct((n, D_MODEL), jnp.float32),
        scratch_shapes=[pltpu.VMEM((4 * width // LANES, tm, LANES), jnp.float32)],
        compiler_params=_cparams("arbitrary"),
        name="merge_project",
    )(x, ya, yb, *dil_o, *dil_lse, yd, z, w_branch, w_out)


MOE_TM = 512
MOE_TT = 256
MOE_W_SMALL = 64
MOE_ROWS = 1024
MOE_ALIGN = 8
MOE_XW = D_MODEL + LANES


def _router_kernel(x_ref, g_ref, w_ref, h_ref, aff_ref):
    x = x_ref[...]
    y = x * lax.rsqrt(jnp.mean(x * x, axis=-1, keepdims=True) + EPS)
    h = (y * g_ref[...]).astype(jnp.bfloat16)
    h_ref[...] = h
    logits = jnp.dot(h, w_ref[...], preferred_element_type=jnp.float32)
    lane = lax.broadcasted_iota(jnp.int32, logits.shape, 1)
    logits = jnp.where(lane < N_EXPERTS, logits, NEG_INF)
    e = jnp.exp(logits - jnp.max(logits, axis=-1, keepdims=True))
    aff_t = (e / jnp.sum(e, axis=-1, keepdims=True)).T
    for c in range(x.shape[0] // MOE_TT):
        aff_ref[c] = aff_t[:N_EXPERTS, c * MOE_TT:(c + 1) * MOE_TT]


def moe_router(x, norm_g, w_router):
    n, d = x.shape
    tm = 2 * MOE_TT
    w = jnp.pad(w_router.astype(jnp.bfloat16), ((0, 0), (0, LANES - N_EXPERTS)))
    return pl.pallas_call(
        _router_kernel,
        grid=(n // tm,),
        in_specs=[pl.BlockSpec((tm, d), lambda i: (i, 0)), pl.BlockSpec((1, d), lambda i: (0, 0)),
                  pl.BlockSpec((d, LANES), lambda i: (0, 0))],
        out_specs=[pl.BlockSpec((tm, d), lambda i: (i, 0)),
                   pl.BlockSpec((tm // MOE_TT, N_EXPERTS, MOE_TT), lambda i: (i, 0, 0))],
        out_shape=[jax.ShapeDtypeStruct((n, d), jnp.bfloat16),
                   jax.ShapeDtypeStruct((n // MOE_TT, N_EXPERTS, MOE_TT), jnp.float32)],
        compiler_params=_cparams("arbitrary"),
        name="moe_router",
    )(x, norm_g.reshape(1, d), w)


def _affinity_bits(a):
    return lax.bitcast_convert_type(a, jnp.int32)


def _threshold_kernel(aff_ref, thr_ref, need_ref, *, cap, n_tiles):
    def count(pred, thr):
        def tile(c, acc):
            return acc + jnp.where(pred(_affinity_bits(aff_ref[c]), thr), 1.0, 0.0)
        acc = lax.fori_loop(0, n_tiles, tile, jnp.zeros((N_EXPERTS, MOE_TT), jnp.float32), unroll=8)
        return jnp.sum(acc, axis=1, keepdims=True)

    def bit(i, thr):
        cand = thr | jnp.left_shift(jnp.int32(1), 30 - i)
        return jnp.where(count(lambda b, t: b >= t, cand) >= cap, cand, thr)

    thr = lax.fori_loop(0, 31, bit, jnp.zeros((N_EXPERTS, 1), jnp.int32))
    thr_ref[...] = jnp.broadcast_to(thr, thr_ref.shape)
    need_ref[...] = jnp.broadcast_to(cap - count(lambda b, t: b > t, thr), need_ref.shape)


def moe_threshold(aff, cap):
    n_tiles = aff.shape[0]
    full = pl.BlockSpec((N_EXPERTS, LANES), lambda i: (0, 0))
    return pl.pallas_call(
        functools.partial(_threshold_kernel, cap=cap, n_tiles=n_tiles),
        grid=(1,),
        in_specs=[pl.BlockSpec(aff.shape, lambda i: (0, 0, 0))],
        out_specs=[full, full],
        out_shape=[jax.ShapeDtypeStruct((N_EXPERTS, LANES), jnp.int32),
                   jax.ShapeDtypeStruct((N_EXPERTS, LANES), jnp.float32)],
        compiler_params=_cparams("arbitrary"),
        name="moe_threshold",
    )(aff)


def _assign_kernel(aff_ref, thr_ref, need_ref, codet_ref, coden_ref, start_ref, total_ref, ties_ref, run_ref):
    @pl.when(pl.program_id(0) == 0)
    def _():
        ties_ref[...] = jnp.zeros_like(ties_ref)
        run_ref[...] = jnp.zeros_like(run_ref)

    f32 = jnp.float32
    bits = _affinity_bits(aff_ref[0])
    thr = thr_ref[:, :1]
    ii = lax.broadcasted_iota(jnp.int32, (MOE_TT, MOE_TT), 0)
    jj = lax.broadcasted_iota(jnp.int32, (MOE_TT, MOE_TT), 1)
    tri = jnp.where(ii <= jj, 1.0, 0.0).astype(jnp.bfloat16)
    eq = bits == thr
    eq_f = jnp.where(eq, 1.0, 0.0)
    tie_rank = jnp.dot(eq_f.astype(jnp.bfloat16), tri, preferred_element_type=f32) + ties_ref[:, :1]
    sel = jnp.logical_or(bits > thr, jnp.logical_and(eq, tie_rank <= need_ref[:, :1]))
    sel_f = jnp.where(sel, 1.0, 0.0)
    incl = jnp.dot(sel_f.astype(jnp.bfloat16), tri, preferred_element_type=f32)
    code = jnp.where(sel, incl - 1.0, -1.0)
    codet_ref[0] = code.astype(jnp.int32)
    padded = jnp.concatenate([code, jnp.zeros((LANES - N_EXPERTS, MOE_TT), f32)], axis=0)
    coden_ref[...] = padded.T.astype(jnp.int32)
    start_ref[0] = run_ref[...].astype(jnp.int32)
    taken = jnp.sum(sel_f, axis=1, keepdims=True)
    run_ref[...] = run_ref[...] + jnp.floor((taken + (MOE_ALIGN - 1)) * (1.0 / MOE_ALIGN)) * MOE_ALIGN
    total_ref[...] = run_ref[...].astype(jnp.int32)
    ties_ref[...] = ties_ref[...] + jnp.sum(eq_f, axis=1, keepdims=True)


def moe_assign(aff, thr, need):
    n_tiles = aff.shape[0]
    tile = pl.BlockSpec((1, N_EXPERTS, MOE_TT), lambda c: (c, 0, 0))
    full = pl.BlockSpec((N_EXPERTS, LANES), lambda c: (0, 0))
    return pl.pallas_call(
        _assign_kernel,
        grid=(n_tiles,),
        in_specs=[tile, full, full],
        out_specs=[tile, pl.BlockSpec((MOE_TT, LANES), lambda c: (c, 0)),
                   pl.BlockSpec((1, N_EXPERTS, LANES), lambda c: (c, 0, 0)), full],
        out_shape=[jax.ShapeDtypeStruct((n_tiles, N_EXPERTS, MOE_TT), jnp.int32),
                   jax.ShapeDtypeStruct((n_tiles * MOE_TT, LANES), jnp.int32),
                   jax.ShapeDtypeStruct((n_tiles, N_EXPERTS, LANES), jnp.int32),
                   jax.ShapeDtypeStruct((N_EXPERTS, LANES), jnp.int32)],
        scratch_shapes=[pltpu.VMEM((N_EXPERTS, LANES), jnp.float32)] * 2,
        compiler_params=_cparams("arbitrary"),
        name="moe_assign",
    )(aff, thr, need)


def _dispatch_kernel(start_ref, big_ref, h_ref, code_ref, aff_ref, init_hbm, x_hbm, buf_ref, sem, *, n_tiles):
    del init_hbm
    c = pl.program_id(0)
    cur = c % 2
    f32 = jnp.float32
    h = h_ref[...]
    code = code_ref[0]
    gate = aff_ref[0]

    def copies(tile, slot, first, width, group):
        return [pltpu.make_async_copy(
            buf_ref.at[slot, pl.ds(k * width, width)],
            x_hbm.at[first + k, pl.ds(_slot_start(start_ref, tile, first + k), width)],
            sem.at[slot, k]) for k in range(group)]

    def stage(first, width, group):
        slot = lax.broadcasted_iota(jnp.int32, (width, MOE_TT), 0)
        onehots = []
        for k in range(group):
            e = first + k
            hit = slot == code[e:e + 1, :]
            onehots.append(jnp.where(hit, 1.0, 0.0).astype(jnp.bfloat16))
            gwin = jnp.sum(jnp.where(hit, gate[e:e + 1, :], 0.0), axis=1, keepdims=True)
            buf_ref[cur, k * width:(k + 1) * width, D_MODEL:] = jnp.broadcast_to(gwin, (width, LANES))
        buf_ref[cur, :, :D_MODEL] = jnp.dot(jnp.concatenate(onehots, axis=0), h, preferred_element_type=f32)

    def wait_previous_small():
        prev = jnp.maximum(c - 1, 0)

        @pl.when(jnp.logical_and(c > 0, big_ref[prev] == 0))
        def _():
            for cp in copies(prev, 1 - cur, 0, MOE_W_SMALL, N_EXPERTS):
                cp.wait()

    @pl.when(big_ref[c] == 0)
    def _():
        stage(0, MOE_W_SMALL, N_EXPERTS)
        wait_previous_small()
        mine = copies(c, cur, 0, MOE_W_SMALL, N_EXPERTS)
        for cp in mine:
            cp.start()

        @pl.when(c == n_tiles - 1)
        def _():
            for cp in mine:
                cp.wait()

    @pl.when(big_ref[c] != 0)
    def _():
        wait_previous_small()
        group = MOE_ROWS // MOE_TT
        for first in range(0, N_EXPERTS, group):
            stage(first, MOE_TT, group)
            mine = copies(c, cur, first, MOE_TT, group)
            for cp in mine:
                cp.start()
            for cp in mine:
                cp.wait()


def _slot_start(start_ref, tile, expert):
    return pl.multiple_of(start_ref[tile * N_EXPERTS + expert], MOE_ALIGN)


def moe_dispatch(starts, big, h, code_t, aff, slots):
    n, d = h.shape
    n_tiles = n // MOE_TT
    tile = pl.BlockSpec((1, N_EXPERTS, MOE_TT), lambda c, s, b: (c, 0, 0))
    shape = (N_EXPERTS, slots, MOE_XW)
    return pl.pallas_call(
        functools.partial(_dispatch_kernel, n_tiles=n_tiles),
        grid_spec=pltpu.PrefetchScalarGridSpec(
            num_scalar_prefetch=2,
            grid=(n_tiles,),
            in_specs=[pl.BlockSpec((MOE_TT, d), lambda c, s, b: (c, 0)), tile, tile,
                      pl.BlockSpec(memory_space=pl.ANY)],
            out_specs=pl.BlockSpec(memory_space=pl.ANY),
            scratch_shapes=[pltpu.VMEM((2, MOE_ROWS, MOE_XW), jnp.float32),
                            pltpu.SemaphoreType.DMA((2, MOE_ROWS // MOE_W_SMALL))],
        ),
        out_shape=jax.ShapeDtypeStruct(shape, jnp.float32),
        input_output_aliases={5: 0},
        compiler_params=_cparams("arbitrary"),
        name="moe_dispatch",
    )(starts, big, h, code_t, aff, jnp.zeros(shape, jnp.float32))


def _expert_kernel(total_ref, x_ref, wg_ref, wu_ref, wd_ref, o_ref):
    f32 = jnp.float32
    used = pl.program_id(1) * MOE_TM < total_ref[pl.program_id(0)]

    @pl.when(used)
    def _():
        x = x_ref[0, :, :D_MODEL].astype(jnp.bfloat16)
        a = jnp.dot(x, wg_ref[0], preferred_element_type=f32)
        u = jnp.dot(x, wu_ref[0], preferred_element_type=f32)
        he = (a * jax.nn.sigmoid(a) * u).astype(jnp.bfloat16)
        o_ref[0] = jnp.dot(he, wd_ref[0], preferred_element_type=f32) * x_ref[0, :, D_MODEL:D_MODEL + 1]

    @pl.when(jnp.logical_not(used))
    def _():
        o_ref[0] = jnp.zeros_like(o_ref[0])


def expert_ffn(total, xd, w_gate, w_up, w_down):
    e, slots, _ = xd.shape
    d = D_MODEL
    tm = MOE_TM
    wspec = pl.BlockSpec((1, d, D_EXPERT), lambda i, j, t: (i, 0, 0))
    return pl.pallas_call(
        _expert_kernel,
        grid_spec=pltpu.PrefetchScalarGridSpec(
            num_scalar_prefetch=1,
            grid=(e, slots // tm),
            in_specs=[pl.BlockSpec((1, tm, MOE_XW), lambda i, j, t: (i, j, 0)),
                      wspec, wspec,
                      pl.BlockSpec((1, D_EXPERT, d), lambda i, j, t: (i, 0, 0))],
            out_specs=pl.BlockSpec((1, tm, d), lambda i, j, t: (i, j, 0)),
        ),
        out_shape=jax.ShapeDtypeStruct((e, slots, d), jnp.float32),
        compiler_params=_cparams("arbitrary", "arbitrary"),
        name="expert_ffn",
    )(total, xd, w_gate, w_up, w_down)


def _combine_kernel(start_ref, big_ref, x_ref, code_ref, y_hbm, o_ref, buf_ref, sem, *, n_tiles):
    c = pl.program_id(0)
    cur = c % 2
    f32 = jnp.float32
    bf = jnp.bfloat16
    code = code_ref[...]
    o_ref[...] = x_ref[...]

    def copies(tile, slot, first, width, group):
        return [pltpu.make_async_copy(
            y_hbm.at[first + k, pl.ds(_slot_start(start_ref, tile, first + k), width)],
            buf_ref.at[slot, pl.ds(k * width, width)],
            sem.at[slot, k]) for k in range(group)]

    def onehot(first, width, group):
        pieces = []
        if width < LANES:
            lane = lax.broadcasted_iota(jnp.int32, (MOE_TT, LANES), 1)
            for k in range(0, group, 2):
                e = first + k
                target = jnp.where(lane < width, code[:, e:e + 1], code[:, e + 1:e + 2] + width)
                pieces.append(jnp.where(target == lane, 1.0, 0.0).astype(bf))
        else:
            lane = lax.broadcasted_iota(jnp.int32, (MOE_TT, width), 1)
            for k in range(group):
                e = first + k
                pieces.append(jnp.where(code[:, e:e + 1] == lane, 1.0, 0.0).astype(bf))
        return jnp.concatenate(pieces, axis=1)

    def accumulate(oh):
        y = buf_ref[cur]
        y_hi = y.astype(bf)
        y_lo = (y - y_hi.astype(f32)).astype(bf)
        o_ref[...] += (jnp.dot(oh, y_hi, preferred_element_type=f32)
                       + jnp.dot(oh, y_lo, preferred_element_type=f32))

    def prefetch_next():
        nxt = jnp.minimum(c + 1, n_tiles - 1)

        @pl.when(jnp.logical_and(c + 1 < n_tiles, big_ref[nxt] == 0))
        def _():
            for cp in copies(nxt, 1 - cur, 0, MOE_W_SMALL, N_EXPERTS):
                cp.start()

    @pl.when(big_ref[c] == 0)
    def _():
        mine = copies(c, cur, 0, MOE_W_SMALL, N_EXPERTS)

        @pl.when(c == 0)
        def _():
            for cp in mine:
                cp.start()

        prefetch_next()
        oh = onehot(0, MOE_W_SMALL, N_EXPERTS)
        for cp in mine:
            cp.wait()
        accumulate(oh)

    @pl.when(big_ref[c] != 0)
    def _():
        prefetch_next()
        group = MOE_ROWS // MOE_TT
        for first in range(0, N_EXPERTS, group):
            mine = copies(c, cur, first, MOE_TT, group)
            for cp in mine:
                cp.start()
            oh = onehot(first, MOE_TT, group)
            for cp in mine:
                cp.wait()
            accumulate(oh)


def moe_combine(starts, big, x, code_n, y):
    n, d = x.shape
    row = pl.BlockSpec((MOE_TT, d), lambda c, s, b: (c, 0))
    return pl.pallas_call(
        functools.partial(_combine_kernel, n_tiles=n // MOE_TT),
        grid_spec=pltpu.PrefetchScalarGridSpec(
            num_scalar_prefetch=2,
            grid=(n // MOE_TT,),
            in_specs=[row, pl.BlockSpec((MOE_TT, LANES), lambda c, s, b: (c, 0)),
                      pl.BlockSpec(memory_space=pl.ANY)],
            out_specs=row,
            scratch_shapes=[pltpu.VMEM((2, MOE_ROWS, d), jnp.float32),
                            pltpu.SemaphoreType.DMA((2, MOE_ROWS // MOE_W_SMALL))],
        ),
        out_shape=jax.ShapeDtypeStruct((n, d), jnp.float32),
        compiler_params=_cparams("arbitrary"),
        name="moe_combine",
    )(starts, big, x, code_n, y)


def ec_moe(x, norm_g, w_router, w_gate, w_up, w_down):
    n_tok, _ = x.shape
    cap = EC_FACTOR * n_tok // N_EXPERTS
    h, aff = moe_router(x, norm_g, w_router)
    thr, need = moe_threshold(aff, cap)
    code_t, code_n, starts, total = moe_assign(aff, thr, need)
    starts = starts[:, :, 0]
    total = total[:, 0]
    ends = jnp.concatenate([starts[1:], total[None]], axis=0)
    big = (jnp.max(ends - starts, axis=1) > MOE_W_SMALL).astype(jnp.int32)
    starts = starts.reshape(-1)
    n_tiles = n_tok // MOE_TT
    slots = -(-(cap + MOE_ALIGN * n_tiles + MOE_TT) // MOE_TM) * MOE_TM
    xd = moe_dispatch(starts, big, h, code_t, aff, slots)
    y = expert_ffn(total, xd, w_gate, w_up, w_down)
    return moe_combine(starts, big, x, code_n, y)


def split_in_proj(w_in):
    w = w_in.astype(jnp.bfloat16)
    sizes = (LRU_WIDTH, LRU_WIDTH, MLA_Q_RANK, MLA_KV_RANK, MLA_ROPE,
             DIL_HEADS * DIL_HEAD_DIM, DIL_HEADS * DIL_HEAD_DIM, DIL_HEADS * DIL_HEAD_DIM,
             RET_HEADS * RET_QK, RET_HEADS * RET_QK, RET_HEADS * RET_V, RET_HEADS * RET_V,
             N_BRANCHES * D_MODEL)
    parts, off = [], 0
    for s in sizes:
        parts.append(w[:, off:off + s])
        off += s
    xa, ga, cq, ckv, kr, dq, dk, dv, rq, rk, rv, rg, zg = parts
    gw = DIL_GROUP_WIDTH
    grp = lambda a, g: a[:, g * gw:(g + 1) * gw]
    zeros = jnp.zeros((w.shape[0], Z_PAD), w.dtype)
    main = jnp.concatenate([zg, xa, ga, cq, ckv, kr, zeros, grp(dq, 0), grp(dk, 0), grp(dv, 0),
                            rq, rk, rv, rg], axis=1)
    dil = [jnp.concatenate([grp(dq, g), grp(dk, g), grp(dv, g)], axis=1) for g in range(1, len(DIL_PAIRS))]
    return main, dil


def mixer(x, batch, seq, norm_g, w_in, conv_w, conv_b, lru_gate_w, lru_gate_b, lru_lambda, mla_q_norm,
          mla_kv_norm, w_uq, w_ukv, ret_decay, w_branch, w_out, mla_tables, ret_tables):
    n = batch * seq
    w_main, w_dil = split_in_proj(w_in)
    z = norm_matmul(x, norm_g, w_main, jnp.bfloat16, min(1024, n), 1024)
    wg, gb = lru_gate_dense(lru_gate_w, lru_gate_b)
    h_fwd = lru_direction(z, conv_w, conv_b, wg, gb, lru_lambda, batch, seq, False)
    ya = lru_direction(z, conv_w, conv_b, wg, gb, lru_lambda, batch, seq, True, h_fwd)
    q, k, v = mla_project(z, mla_q_norm, mla_kv_norm, mla_weights(w_uq, w_ukv), mla_tables, batch, seq)
    yb = mla_attention(q, k, v, batch, seq)
    dil = [band_group(z.reshape(batch, 1, seq, Z_COLS), 0, 1, batch, seq, (CB_DQ, CB_DK, CB_DV))]
    pairs = DIL_HEADS_PER_GROUP // 2
    for g in range(1, len(DIL_PAIRS)):
        d = DIL_PAIRS[g][1]
        zd = norm_matmul_dil(x, norm_g, w_dil[g - 1], d, batch, seq, min(1024, seq))
        dil.append(band_group(zd, g, d, batch, seq, (0, pairs, 2 * pairs)))
    yd = retention(z, ret_decay, ret_tables, batch, seq)
    return merge_project(x, ya, yb, [o for o, _ in dil], [l for _, l in dil], yd, z,
                         w_branch.astype(jnp.bfloat16), w_out.astype(jnp.bfloat16), seq)


def trunk(x, norm_mix, w_in, conv_w, conv_b, lru_gate_w, lru_gate_b, lru_lambda, mla_q_norm, mla_kv_norm,
          w_uq, w_ukv, ret_decay, w_branch, w_out, norm_ffn, w_router, w_gate, w_up, w_down, norm_final):
    batch, seq, d = x.shape
    x = x.reshape(batch * seq, d)
    mla_tables = mla_rope_tables(seq)
    ret_tables = ret_rope_tables(seq)
    bf = jnp.bfloat16
    for l in range(norm_mix.shape[0]):
        x = mixer(x, batch, seq, norm_mix[l], w_in[l], conv_w[l], conv_b[l], lru_gate_w[l], lru_gate_b[l],
                  lru_lambda[l], mla_q_norm[l], mla_kv_norm[l], w_uq[l], w_ukv[l], ret_decay[l],
                  w_branch[l], w_out[l], mla_tables, ret_tables)
        x = ec_moe(x, norm_ffn[l], w_router[l], w_gate[l].astype(bf), w_up[l].astype(bf), w_down[l].astype(bf))
    return rmsnorm_pallas(x, norm_final, jnp.float32, min(1024, batch * seq)).reshape(batch, seq, d)


def kernel(x_prompt, x_sample, norm_mix, w_in, conv_w, conv_b, lru_gate_w, lru_gate_b, lru_lambda,
           mla_q_norm, mla_kv_norm, w_uq, w_ukv, ret_decay, w_branch, w_out, norm_ffn, w_router,
           w_gate, w_up, w_down, norm_final):
    args = (norm_mix, w_in, conv_w, conv_b, lru_gate_w, lru_gate_b, lru_lambda, mla_q_norm, mla_kv_norm,
            w_uq, w_ukv, ret_decay, w_branch, w_out, norm_ffn, w_router, w_gate, w_up, w_down, norm_final)
    return trunk(x_prompt, *args), trunk(x_sample, *args)
```

```python
import functools
import math

import numpy as np
import jax
import jax.numpy as jnp
from jax import lax
from jax.experimental import pallas as pl
from jax.experimental.pallas import tpu as pltpu

D_MODEL = 1024
DEPTH = 4
EPS = 1e-6
NEG_INF = -1e30
ROPE_BASE = 10000.0
N_BRANCHES = 4
LRU_WIDTH = 512
LRU_BLOCKS = 8
LRU_BLOCK = LRU_WIDTH // LRU_BLOCKS
CONV_WIDTH = 4
CONV_LEFT = 2
LRU_C = 8.0
MLA_HEADS = 8
MLA_NOPE = 64
MLA_ROPE = 32
MLA_V = 64
MLA_Q_RANK = 256
MLA_KV_RANK = 128
DIL_PAIRS = ((128, 1), (512, 4), (2048, 16))
DIL_HEADS_PER_GROUP = 8
DIL_HEADS = DIL_HEADS_PER_GROUP * len(DIL_PAIRS)
DIL_HEAD_DIM = 64
RET_HEADS = 4
RET_QK = 128
RET_V = 128
N_EXPERTS = 16
EC_FACTOR = 2
D_EXPERT = 1024

LANES = 128
VMEM_LIMIT_BYTES = 56 * 1024 * 1024

Z_COLS = 9216
Z_PAD = 96
CB_ZG = 0
CB_XA, CB_GA = 32, 36
CB_MLA = 40
CB_DQ, CB_DK, CB_DV = 44, 48, 52
CB_RQ, CB_RK, CB_RV, CB_RG = 56, 60, 64, 68
DIL_GROUP_WIDTH = DIL_HEADS_PER_GROUP * DIL_HEAD_DIM


def _cparams(*sem):
    return pltpu.CompilerParams(dimension_semantics=sem, vmem_limit_bytes=VMEM_LIMIT_BYTES)


def _norm_matmul_kernel(x_ref, g_ref, w_ref, o_ref, h_ref):
    @pl.when(pl.program_id(1) == 0)
    def _():
        x = x_ref[...]
        y = x * lax.rsqrt(jnp.mean(x * x, axis=-1, keepdims=True) + EPS)
        h_ref[...] = (y * g_ref[...]).astype(h_ref.dtype)

    o_ref[...] = jnp.dot(h_ref[...], w_ref[...], preferred_element_type=jnp.float32).astype(o_ref.dtype)


def norm_matmul(x, g, w, out_dtype, tm, tn):
    n, d = x.shape
    c = w.shape[1]
    return pl.pallas_call(
        _norm_matmul_kernel,
        grid=(n // tm, c // tn),
        in_specs=[
            pl.BlockSpec((tm, d), lambda i, j: (i, 0)),
            pl.BlockSpec((1, d), lambda i, j: (0, 0)),
            pl.BlockSpec((d, tn), lambda i, j: (0, j)),
        ],
        out_specs=pl.BlockSpec((tm, tn), lambda i, j: (i, j)),
        out_shape=jax.ShapeDtypeStruct((n, c), out_dtype),
        scratch_shapes=[pltpu.VMEM((tm, d), jnp.bfloat16)],
        compiler_params=_cparams("arbitrary", "arbitrary"),
        name="norm_matmul",
    )(x, g.reshape(1, d), w)


def _norm_matmul_dil_kernel(x_ref, g_ref, w_ref, o_ref, hf_ref, hp_ref, *, dil):
    x = x_ref[...]
    y = x * lax.rsqrt(jnp.mean(x * x, axis=-1, keepdims=True) + EPS)
    y = y * g_ref[...]
    rows = x.shape[0] // dil
    for c in range(x.shape[1] // LANES):
        cols = slice(c * LANES, (c + 1) * LANES)
        hf_ref[c] = y[:, cols]
        for r in range(dil):
            hp_ref[r * rows:(r + 1) * rows, cols] = hf_ref[c, pl.ds(r, rows, stride=dil), :].astype(hp_ref.dtype)
    out = jnp.dot(hp_ref[...], w_ref[...], preferred_element_type=jnp.float32)
    for r in range(dil):
        o_ref[0, r] = out[r * rows:(r + 1) * rows, :].astype(o_ref.dtype)


def norm_matmul_dil(x, g, w, dil, batch, seq, tm):
    n, d = x.shape
    c = w.shape[1]
    n_t = seq // tm
    return pl.pallas_call(
        functools.partial(_norm_matmul_dil_kernel, dil=dil),
        grid=(n // tm,),
        in_specs=[
            pl.BlockSpec((tm, d), lambda i: (i, 0)),
            pl.BlockSpec((1, d), lambda i: (0, 0)),
            pl.BlockSpec((d, c), lambda i: (0, 0)),
        ],
        out_specs=pl.BlockSpec((1, dil, tm // dil, c), lambda i: (i // n_t, 0, i % n_t, 0)),
        out_shape=jax.ShapeDtypeStruct((batch, dil, seq // dil, c), jnp.bfloat16),
        scratch_shapes=[pltpu.VMEM((d // LANES, tm, LANES), jnp.float32), pltpu.VMEM((tm, d), jnp.bfloat16)],
        compiler_params=_cparams("arbitrary"),
        name=f"norm_matmul_dil{dil}",
    )(x, g.reshape(1, d), w)


def _rmsnorm_kernel(x_ref, g_ref, o_ref):
    x = x_ref[...]
    y = x * lax.rsqrt(jnp.mean(x * x, axis=-1, keepdims=True) + EPS)
    o_ref[...] = (y * g_ref[...]).astype(o_ref.dtype)


def rmsnorm_pallas(x, g, out_dtype, tm):
    n, d = x.shape
    return pl.pallas_call(
        _rmsnorm_kernel,
        grid=(n // tm,),
        in_specs=[pl.BlockSpec((tm, d), lambda i: (i, 0)), pl.BlockSpec((1, d), lambda i: (0, 0))],
        out_specs=pl.BlockSpec((tm, d), lambda i: (i, 0)),
        out_shape=jax.ShapeDtypeStruct((n, d), out_dtype),
        compiler_params=_cparams("arbitrary"),
        name="rmsnorm",
    )(x, g.reshape(1, d))


LRU_TC = 256
LRU_HALO = 16


def _softplus(x):
    return jnp.maximum(x, 0.0) + jnp.log(1.0 + jnp.exp(-jnp.abs(x)))


def _gelu_tanh(x):
    return 0.5 * x * (1.0 + jnp.tanh(math.sqrt(2.0 / math.pi) * (x + 0.044715 * (x * x * x))))


def _lru_scan_chunk(a, b, reverse):
    n = a.shape[0]
    row = lax.broadcasted_iota(jnp.int32, a.shape, 0)
    s = 1
    while s < n:
        if reverse:
            keep = row < (n - s)
            a_s = pltpu.roll(a, n - s, 0)
            b_s = pltpu.roll(b, n - s, 0)
        else:
            keep = row >= s
            a_s = pltpu.roll(a, s, 0)
            b_s = pltpu.roll(b, s, 0)
        b = jnp.where(keep, a * b_s + b, b)
        a = jnp.where(keep, a * a_s, a)
        s *= 2
    return a, b


def _lru_kernel(*refs, reverse, n_t):
    if reverse:
        (xp_ref, xc_ref, xn_ref, cw_ref, cb_ref, wg_ref, gb_ref, lam_ref, hf_ref, ga_ref,
         o_ref, carry_ref) = refs
    else:
        (xp_ref, xc_ref, xn_ref, cw_ref, cb_ref, wg_ref, gb_ref, lam_ref, o_ref, carry_ref) = refs
    step = pl.program_id(1)
    t = (n_t - 1 - step) if reverse else step

    @pl.when(step == 0)
    def _():
        carry_ref[...] = jnp.zeros_like(carry_ref)

    prev = jnp.where(t > 0, xp_ref[...].astype(jnp.float32), 0.0)
    nxt = jnp.where(t < n_t - 1, xn_ref[...].astype(jnp.float32), 0.0)
    win = jnp.concatenate([prev, xc_ref[...].astype(jnp.float32), nxt], axis=0)
    xc = cb_ref[...]
    for k in range(CONV_WIDTH):
        lo = LRU_HALO - CONV_LEFT + k
        xc = xc + cw_ref[k:k + 1, :] * win[lo:lo + LRU_TC, :]
    gl = jnp.dot(xc.astype(jnp.bfloat16), wg_ref[0], preferred_element_type=jnp.float32) + gb_ref[0]
    r = jax.nn.sigmoid(gl[:, :LRU_WIDTH])
    i = jax.nn.sigmoid(gl[:, LRU_WIDTH:])
    log_a = (-LRU_C) * r * _softplus(-lam_ref[0])
    a = jnp.exp(log_a)
    b = jnp.sqrt(1.0 - jnp.exp(2.0 * log_a)) * i * xc
    a_cum, b_cum = _lru_scan_chunk(a, b, reverse)
    h = b_cum + a_cum * carry_ref[0:1, :]
    last = 0 if reverse else LRU_TC - 1
    carry_ref[0:1, :] = h[last:last + 1, :]
    if reverse:
        o_ref[...] = (_gelu_tanh(ga_ref[...].astype(jnp.float32)) * (hf_ref[...] + h)).astype(o_ref.dtype)
    else:
        o_ref[...] = h


def lru_direction(z, conv_w, conv_b, wg, gb, lam, batch, seq, reverse, h_fwd=None):
    n_t = seq // LRU_TC
    per_halo = LRU_TC // LRU_HALO
    n_halo = batch * seq // LRU_HALO
    d = 1 if reverse else 0

    def tt(s):
        return (n_t - 1 - s) if reverse else s

    xa_col = CB_XA * LANES // LRU_WIDTH
    ga_col = CB_GA * LANES // LRU_WIDTH

    def cur(b, s):
        return (b * n_t + tt(s), 0)

    def prev(b, s):
        return (jnp.maximum((b * n_t + tt(s)) * per_halo - 1, 0), xa_col)

    def nxt(b, s):
        return (jnp.minimum((b * n_t + tt(s) + 1) * per_halo, n_halo - 1), xa_col)

    const2 = lambda b, s: (0, 0)
    in_specs = [
        pl.BlockSpec((LRU_HALO, LRU_WIDTH), prev),
        pl.BlockSpec((LRU_TC, LRU_WIDTH), lambda b, s: (b * n_t + tt(s), xa_col)),
        pl.BlockSpec((LRU_HALO, LRU_WIDTH), nxt),
        pl.BlockSpec((CONV_WIDTH, LRU_WIDTH), const2),
        pl.BlockSpec((1, LRU_WIDTH), const2),
        pl.BlockSpec((1, LRU_WIDTH, 2 * LRU_WIDTH), lambda b, s: (d, 0, 0)),
        pl.BlockSpec((1, 1, 2 * LRU_WIDTH), lambda b, s: (d, 0, 0)),
        pl.BlockSpec((1, 1, LRU_WIDTH), lambda b, s: (d, 0, 0)),
    ]
    args = [z, z, z, conv_w, conv_b.reshape(1, LRU_WIDTH), wg, gb, lam.reshape(2, 1, LRU_WIDTH)]
    if reverse:
        in_specs += [pl.BlockSpec((LRU_TC, LRU_WIDTH), cur),
                     pl.BlockSpec((LRU_TC, LRU_WIDTH), lambda b, s: (b * n_t + tt(s), ga_col))]
        args += [h_fwd, z]
        out_dtype = jnp.bfloat16
    else:
        out_dtype = jnp.float32
    return pl.pallas_call(
        functools.partial(_lru_kernel, reverse=reverse, n_t=n_t),
        grid=(batch, n_t),
        in_specs=in_specs,
        out_specs=pl.BlockSpec((LRU_TC, LRU_WIDTH), cur),
        out_shape=jax.ShapeDtypeStruct((batch * seq, LRU_WIDTH), out_dtype),
        scratch_shapes=[pltpu.VMEM((8, LRU_WIDTH), jnp.float32)],
        compiler_params=_cparams("arbitrary", "arbitrary"),
        name="lru_bwd" if reverse else "lru_fwd",
    )(*args)


def lru_gate_dense(gate_w, gate_b):
    eye = jnp.eye(LRU_BLOCKS, dtype=gate_w.dtype)
    dense = jnp.einsum('dgnij,nm->dgnimj', gate_w, eye).reshape(2, 2, LRU_WIDTH, LRU_WIDTH)
    wg = jnp.concatenate([dense[:, 0], dense[:, 1]], axis=-1).astype(jnp.bfloat16)
    gb = jnp.concatenate([gate_b[:, 0], gate_b[:, 1]], axis=-1).reshape(2, 1, 2 * LRU_WIDTH)
    return wg, gb


MLA_TM = 512
MLA_TQ = 2048
MLA_TK = 256
MLA_DP = 128


def _mla_proj_kernel(z_ref, qn_ref, kn_ref, wqa_ref, wqb_ref, wk_ref, wv_ref, ea_ref, eb_ref,
                     cos_ref, sin_ref, q_ref, k_ref, v_ref):
    z = z_ref[...]
    cq = z[:, :MLA_Q_RANK].astype(jnp.float32)
    ckv = z[:, MLA_Q_RANK:MLA_Q_RANK + MLA_KV_RANK].astype(jnp.float32)
    kr = z[:, MLA_Q_RANK + MLA_KV_RANK:]
    cqn = (cq * lax.rsqrt(jnp.mean(cq * cq, axis=-1, keepdims=True) + EPS) * qn_ref[...]).astype(jnp.bfloat16)
    ckn = (ckv * lax.rsqrt(jnp.mean(ckv * ckv, axis=-1, keepdims=True) + EPS) * kn_ref[...]).astype(jnp.bfloat16)
    cos = cos_ref[...]
    sin = sin_ref[...]
    f32 = jnp.float32
    k_rope = (jnp.dot(kr, ea_ref[...], preferred_element_type=f32) * cos
              + jnp.dot(kr, eb_ref[...], preferred_element_type=f32) * sin)
    scale = (MLA_NOPE + MLA_ROPE) ** -0.5 * math.log2(math.e)
    row = lax.broadcasted_iota(jnp.int32, (MLA_DP, MLA_TM), 0)
    for h in range(MLA_HEADS):
        qa = jnp.dot(cqn, wqa_ref[h], preferred_element_type=f32)
        qb = jnp.dot(cqn, wqb_ref[h], preferred_element_type=f32)
        q_ref[0, h] = ((qa * cos + qb * sin) * scale).astype(q_ref.dtype)
        k_ref[0, h] = (jnp.dot(ckn, wk_ref[h], preferred_element_type=f32) + k_rope).astype(k_ref.dtype)
        vt = lax.dot_general(wv_ref[h], ckn, (((1,), (1,)), ((), ())), preferred_element_type=f32)
        vt = jnp.where(row == _mla_ones_row(h), 1.0, vt)
        for c in range(MLA_TM // MLA_TK):
            v_ref[0, h, c] = vt[:, c * MLA_TK:(c + 1) * MLA_TK].astype(v_ref.dtype)


def _mla_ones_row(head):
    return MLA_V if head % 2 == 0 else 0


def _rot_half_matrix(n):
    half = n // 2
    r = np.zeros((n, n), np.float32)
    for j in range(half):
        r[half + j, j] = -1.0
        r[j, half + j] = 1.0
    return r


def mla_weights(w_uq, w_ukv):
    rot = jnp.asarray(_rot_half_matrix(MLA_ROPE))
    wq = jnp.transpose(w_uq, (1, 0, 2))
    pad = lambda a, lo, hi: jnp.pad(a, ((0, 0), (0, 0), (lo, hi)))
    wqa = pad(wq, 0, MLA_DP - MLA_NOPE - MLA_ROPE)
    wqb = pad(jnp.einsum('hrd,de->hre', wq[..., MLA_NOPE:], rot), MLA_NOPE, MLA_DP - MLA_NOPE - MLA_ROPE)
    wkv = jnp.transpose(w_ukv, (1, 0, 2))
    wk = pad(wkv[..., :MLA_NOPE], 0, MLA_DP - MLA_NOPE)
    wv_even = pad(wkv[..., MLA_NOPE:], 0, MLA_V)
    wv_odd = pad(wkv[..., MLA_NOPE:], MLA_V, 0)
    wv = jnp.where((jnp.arange(MLA_HEADS) % 2 == 0)[:, None, None], wv_even, wv_odd)
    wv = jnp.transpose(wv, (0, 2, 1))
    ea = np.zeros((LANES, MLA_DP), np.float32)
    for j in range(MLA_ROPE):
        ea[j, MLA_NOPE + j] = 1.0
    eb = np.zeros((LANES, MLA_DP), np.float32)
    eb[:MLA_ROPE, MLA_NOPE:MLA_NOPE + MLA_ROPE] = _rot_half_matrix(MLA_ROPE)
    bf = jnp.bfloat16
    return (wqa.astype(bf), wqb.astype(bf), wk.astype(bf), wv.astype(bf),
            jnp.asarray(ea, bf), jnp.asarray(eb, bf))


def mla_rope_tables(seq):
    half = MLA_ROPE // 2
    inv = ROPE_BASE ** (-jnp.arange(half, dtype=jnp.float32) / half)
    ang = jnp.arange(seq, dtype=jnp.float32)[:, None] * inv[None, :]
    ones = jnp.ones((seq, MLA_NOPE), jnp.float32)
    zeros = jnp.zeros((seq, MLA_DP - MLA_NOPE - MLA_ROPE), jnp.float32)
    cos = jnp.concatenate([ones, jnp.cos(ang), jnp.cos(ang), zeros], axis=1)
    sin = jnp.concatenate([0.0 * ones, jnp.sin(ang), jnp.sin(ang), zeros], axis=1)
    return cos, sin


def mla_project(z, q_norm, kv_norm, weights, tables, batch, seq):
    wqa, wqb, wk, wv, ea, eb = weights
    cos, sin = tables
    n_t = seq // MLA_TM
    hshape = (batch, MLA_HEADS, seq, MLA_DP)
    c3 = lambda b, t: (0, 0, 0)
    c2 = lambda b, t: (0, 0)
    hspec = pl.BlockSpec((1, MLA_HEADS, MLA_TM, MLA_DP), lambda b, t: (b, 0, t, 0))
    return pl.pallas_call(
        _mla_proj_kernel,
        grid=(batch, n_t),
        in_specs=[
            pl.BlockSpec((MLA_TM, 4 * LANES), lambda b, t: (b * n_t + t, CB_MLA // 4)),
            pl.BlockSpec((1, MLA_Q_RANK), c2),
            pl.BlockSpec((1, MLA_KV_RANK), c2),
            pl.BlockSpec((MLA_HEADS, MLA_Q_RANK, MLA_DP), c3),
            pl.BlockSpec((MLA_HEADS, MLA_Q_RANK, MLA_DP), c3),
            pl.BlockSpec((MLA_HEADS, MLA_KV_RANK, MLA_DP), c3),
            pl.BlockSpec((MLA_HEADS, MLA_KV_RANK, MLA_DP), c3),
            pl.BlockSpec((LANES, MLA_DP), c2),
            pl.BlockSpec((LANES, MLA_DP), c2),
            pl.BlockSpec((MLA_TM, MLA_DP), lambda b, t: (t, 0)),
            pl.BlockSpec((MLA_TM, MLA_DP), lambda b, t: (t, 0)),
        ],
        out_specs=[hspec, hspec,
                   pl.BlockSpec((1, MLA_HEADS, MLA_TM // MLA_TK, MLA_DP, MLA_TK), lambda b, t: (b, 0, t, 0, 0))],
        out_shape=[jax.ShapeDtypeStruct(hshape, jnp.bfloat16)] * 2
        + [jax.ShapeDtypeStruct((batch, MLA_HEADS, seq // MLA_TK, MLA_DP, MLA_TK), jnp.bfloat16)],
        compiler_params=_cparams("arbitrary", "arbitrary"),
        name="mla_project",
    )(z, q_norm.reshape(1, -1), kv_norm.reshape(1, -1), wqa, wqb, wk, wv, ea, eb, cos, sin)


def _mla_attn_kernel(q_ref, k_ref, vt_ref, o_ref, acc_ref, *, n_kv):
    acc_ref[...] = jnp.zeros_like(acc_ref)
    f32 = jnp.float32
    nt = (((1,), (1,)), ((), ()))

    def body(j, carry):
        rows = pl.ds(pl.multiple_of(j * MLA_TK, MLA_TK), MLA_TK)
        new = []
        for hh in range(2):
            m_prev = carry[hh]
            s = lax.dot_general(k_ref[0, hh, rows, :], q_ref[0, hh], nt, preferred_element_type=f32)
            m_next = jnp.maximum(m_prev, jnp.max(s, axis=0, keepdims=True))
            p = jnp.exp2(s - m_next)
            alpha = jnp.exp2(m_prev - m_next)
            new.append(m_next)
            acc_ref[hh] = alpha * acc_ref[hh] + jnp.dot(vt_ref[0, hh, j], p.astype(jnp.bfloat16),
                                                         preferred_element_type=f32)
        return tuple(new)

    m0 = jnp.full((1, MLA_TQ), NEG_INF, f32)
    lax.fori_loop(0, n_kv, body, (m0, m0), unroll=min(4, n_kv))
    acc_a, acc_b = acc_ref[0], acc_ref[1]
    l_a = acc_a[_mla_ones_row(0):_mla_ones_row(0) + 1, :]
    l_b = acc_b[_mla_ones_row(1):_mla_ones_row(1) + 1, :]
    row = lax.broadcasted_iota(jnp.int32, acc_a.shape, 0)
    out_t = jnp.where(row < MLA_V, acc_a / l_a, acc_b / l_b)
    o_ref[...] = out_t.T.astype(o_ref.dtype)


def mla_attention(q, k, vt, batch, seq):
    n_q = seq // MLA_TQ
    n_kv = seq // MLA_TK
    qspec = pl.BlockSpec((1, 2, MLA_TQ, MLA_DP), lambda b, hp, i: (b, hp, i, 0))
    kspec = pl.BlockSpec((1, 2, seq, MLA_DP), lambda b, hp, i: (b, hp, 0, 0))
    vspec = pl.BlockSpec((1, 2, n_kv, MLA_DP, MLA_TK), lambda b, hp, i: (b, hp, 0, 0, 0))
    return pl.pallas_call(
        functools.partial(_mla_attn_kernel, n_kv=n_kv),
        grid=(batch, MLA_HEADS // 2, n_q),
        in_specs=[qspec, kspec, vspec],
        out_specs=pl.BlockSpec((MLA_TQ, LANES), lambda b, hp, i: (b * n_q + i, hp)),
        out_shape=jax.ShapeDtypeStruct((batch * seq, MLA_HEADS * MLA_V), jnp.bfloat16),
        scratch_shapes=[pltpu.VMEM((2, MLA_DP, MLA_TQ), jnp.float32)],
        compiler_params=_cparams("arbitrary", "arbitrary", "arbitrary"),
        name="mla_attention",
    )(q, k, vt)


BAND_Q = 128
BAND_W = 256
BAND_RADIUS = 64
BAND_ROWS_PER_STEP = 2048


def band_bias_table(group, dil):
    n = DIL_HEADS
    slopes = np.asarray([2.0 ** (-8.0 * (h + 1) / n) for h in range(n)], np.float32)
    slopes = slopes[group * DIL_HEADS_PER_GROUP:(group + 1) * DIL_HEADS_PER_GROUP]
    iq = np.arange(BAND_Q)[:, None]
    ik = np.arange(BAND_W)[None, :]
    tabs = []
    for d in range(3):
        dist = np.abs(d * BAND_RADIUS + iq - ik)
        bias = -slopes[:, None, None] * (dil * dist).astype(np.float32)[None]
        tabs.append(np.where((dist <= BAND_RADIUS)[None], bias, np.float32(NEG_INF)))
    return jnp.asarray(np.stack(tabs).astype(np.float32))


def _band_kernel(q_ref, k_ref, v_ref, bias_ref, o_ref, lse_ref, *, length):
    lane = lax.broadcasted_iota(jnp.int32, (1, LANES), 1)
    first = lane < DIL_HEAD_DIM
    scale = DIL_HEAD_DIM ** -0.5

    n_blocks = length // BAND_Q
    n_res = q_ref.shape[1]

    def body(idx, carry):
        rr = idx // n_blocks
        q0 = pl.multiple_of((idx % n_blocks) * BAND_Q, BAND_Q)
        start = pl.multiple_of(jnp.clip(q0 - BAND_RADIUS, 0, length - BAND_W), BAND_RADIUS)
        didx = (q0 - start) // BAND_RADIUS
        q = q_ref[0, rr, pl.ds(q0, BAND_Q), :]
        kw = k_ref[0, rr, pl.ds(start, BAND_W), :]
        vw = v_ref[0, rr, pl.ds(start, BAND_W), :]
        outs, lses = [], []
        for hh in range(2):
            sel = first if hh == 0 else jnp.logical_not(first)
            qh = jnp.where(sel, q, jnp.zeros_like(q))
            s = lax.dot_general(qh, kw, (((1,), (1,)), ((), ())), preferred_element_type=jnp.float32)
            s = s * scale + bias_ref[didx, hh]
            m = jnp.max(s, axis=1, keepdims=True)
            e = jnp.exp(s - m)
            den = jnp.sum(e, axis=1, keepdims=True)
            p = (e / den).astype(jnp.bfloat16)
            outs.append(jnp.dot(p, vw, preferred_element_type=jnp.float32))
            lses.append(m + jnp.log(den))
        o_ref[0, rr, pl.ds(q0, BAND_Q), :] = jnp.where(first, outs[0], outs[1]).astype(o_ref.dtype)
        lse_ref[0, rr, pl.ds(q0, BAND_Q), :] = jnp.where(first, lses[0], lses[1])
        return carry

    lax.fori_loop(0, n_res * n_blocks, body, 0, unroll=min(16, n_res * n_blocks))


def band_group(src, group, dil, batch, seq, col_blocks):
    length = seq // dil
    pairs = DIL_HEADS_PER_GROUP // 2
    n_res = max(1, min(dil, BAND_ROWS_PER_STEP // length))

    def zspec(cb):
        return pl.BlockSpec((1, n_res, length, LANES), lambda b, r, hp: (b, r, 0, cb + hp))

    ospec = pl.BlockSpec((1, n_res, length, LANES), lambda b, r, hp: (b, r, 0, hp))
    oshape = (batch, dil, length, DIL_GROUP_WIDTH)
    return pl.pallas_call(
        functools.partial(_band_kernel, length=length),
        grid=(batch, dil // n_res, pairs),
        in_specs=[zspec(col_blocks[0]), zspec(col_blocks[1]), zspec(col_blocks[2]),
                  pl.BlockSpec((3, 2, BAND_Q, BAND_W), lambda b, r, hp: (0, hp, 0, 0))],
        out_specs=[ospec, ospec],
        out_shape=[jax.ShapeDtypeStruct(oshape, jnp.bfloat16), jax.ShapeDtypeStruct(oshape, jnp.float32)],
        compiler_params=_cparams("arbitrary", "arbitrary", "arbitrary"),
        name=f"band_attention_g{group}",
    )(src, src, src, band_bias_table(group, dil))


RET_C = 256


def _ret_kernel(lg_ref, q_ref, k_ref, v_ref, g_ref, cos_ref, sin_ref, o_ref, qs_ref, ks_ref, o1_ref, *, n_c):
    h = pl.program_id(1)
    lgf = lg_ref[0, h]
    lgb = lg_ref[1, h]
    c = RET_C
    f32 = jnp.float32
    bf = jnp.bfloat16
    ii = lax.broadcasted_iota(jnp.int32, (c, c), 0)
    jj = lax.broadcasted_iota(jnp.int32, (c, c), 1)
    diff = (ii - jj).astype(f32)
    decay = jnp.where(diff >= 0.0, jnp.exp(lgf * jnp.maximum(diff, 0.0)), jnp.exp(lgb * jnp.maximum(-diff, 0.0)))
    idx = lax.broadcasted_iota(jnp.int32, (c, 1), 0).astype(f32)
    xi_f = jnp.exp(lgf * (idx + 1.0))
    zeta_f = jnp.exp(lgf * (c - 1.0 - idx))
    xi_b = jnp.exp(lgb * (c - idx))
    zeta_b = jnp.exp(lgb * idx)
    cd_f = jnp.exp(lgf * c)
    cd_b = jnp.exp(lgb * c)
    kscale = RET_QK ** -0.5

    def rope(x, rows):
        return x * cos_ref[rows, :] + pltpu.roll(x, RET_QK // 2, 1) * sin_ref[rows, :]

    def fwd(n, state):
        rows = pl.ds(pl.multiple_of(n * c, c), c)
        q = rope(q_ref[0, rows, :].astype(f32), rows)
        k = rope(k_ref[0, rows, :].astype(f32), rows) * kscale
        v = v_ref[0, rows, :]
        qs_ref[rows, :] = q
        ks_ref[rows, :] = k
        qb = q.astype(bf)
        s = lax.dot_general(qb, k.astype(bf), (((1,), (1,)), ((), ())), preferred_element_type=f32) * decay
        o = jnp.dot(s.astype(bf), v, preferred_element_type=f32)
        o = o + xi_f * jnp.dot(qb, state.astype(bf), preferred_element_type=f32)
        o1_ref[rows, :] = o
        kz = (k * zeta_f).T.astype(bf)
        return cd_f * state + jnp.dot(kz, v, preferred_element_type=f32)

    lax.fori_loop(0, n_c, fwd, jnp.zeros((RET_QK, RET_V), f32), unroll=min(8, n_c))

    def bwd(step, state):
        n = n_c - 1 - step
        rows = pl.ds(pl.multiple_of(n * c, c), c)
        q = qs_ref[rows, :]
        k = ks_ref[rows, :]
        v = v_ref[0, rows, :]
        of = o1_ref[rows, :] + xi_b * jnp.dot(q.astype(bf), state.astype(bf), preferred_element_type=f32)
        mu = jnp.mean(of, axis=-1, keepdims=True)
        var = jnp.mean(jnp.square(of - mu), axis=-1, keepdims=True)
        of = (of - mu) * lax.rsqrt(var + EPS)
        g = g_ref[0, rows, :].astype(f32)
        o_ref[0, rows, :] = (g * jax.nn.sigmoid(g) * of).astype(o_ref.dtype)
        kz = (k * zeta_b).T.astype(bf)
        return cd_b * state + jnp.dot(kz, v, preferred_element_type=f32)

    lax.fori_loop(0, n_c, bwd, jnp.zeros((RET_QK, RET_V), f32), unroll=min(8, n_c))


def ret_rope_tables(seq):
    half = RET_QK // 2
    inv = ROPE_BASE ** (-jnp.arange(half, dtype=jnp.float32) / half)
    ang = jnp.arange(seq, dtype=jnp.float32)[:, None] * inv[None, :]
    cos = jnp.concatenate([jnp.cos(ang), jnp.cos(ang)], axis=1)
    sin = jnp.concatenate([-jnp.sin(ang), jnp.sin(ang)], axis=1)
    return cos, sin


def retention(z, ret_decay, tables, batch, seq):
    zv = z.reshape(batch, seq, Z_COLS)
    log_gamma = jax.nn.log_sigmoid(ret_decay.astype(jnp.float32))
    cos, sin = tables

    def zspec(cb):
        return pl.BlockSpec((1, seq, LANES), lambda b, h: (b, 0, cb + h))

    tspec = pl.BlockSpec((seq, RET_QK), lambda b, h: (0, 0))
    out = pl.pallas_call(
        functools.partial(_ret_kernel, n_c=seq // RET_C),
        grid=(batch, RET_HEADS),
        in_specs=[pl.BlockSpec(memory_space=pltpu.SMEM),
                  zspec(CB_RQ), zspec(CB_RK), zspec(CB_RV), zspec(CB_RG), tspec, tspec],
        out_specs=pl.BlockSpec((1, seq, LANES), lambda b, h: (b, 0, h)),
        out_shape=jax.ShapeDtypeStruct((batch, seq, RET_HEADS * RET_V), jnp.bfloat16),
        scratch_shapes=[pltpu.VMEM((seq, RET_QK), jnp.float32)] * 3,
        compiler_params=_cparams("arbitrary", "arbitrary"),
        name="retention",
    )(log_gamma, zv, zv, zv, zv, cos, sin)
    return out.reshape(batch * seq, RET_HEADS * RET_V)


MERGE_TM = 512


def _merge_kernel(x_ref, ya_ref, yb_ref, o0_ref, o1_ref, o2_ref, l0_ref, l1_ref, l2_ref, yd_ref,
                  zg_ref, wb_ref, wo_ref, out_ref, tok_ref):
    f32 = jnp.float32

    def token_order(slot, src_ref):
        dil, rows, width = src_ref.shape[1:]
        tiles = width // LANES
        for r in range(dil):
            v = src_ref[0, r].astype(f32)
            for c in range(tiles):
                tok_ref[slot * tiles + c, pl.ds(r, rows, stride=dil), :] = v[:, c * LANES:(c + 1) * LANES]
        return jnp.concatenate([tok_ref[slot * tiles + c] for c in range(tiles)], axis=1)

    l0 = l0_ref[0, 0]
    l1, l2 = token_order(0, l1_ref), token_order(1, l2_ref)
    o1, o2 = token_order(2, o1_ref), token_order(3, o2_ref)
    m = jnp.maximum(jnp.maximum(l0, l1), l2)
    e0, e1, e2 = jnp.exp(l0 - m), jnp.exp(l1 - m), jnp.exp(l2 - m)
    inv = 1.0 / (e0 + e1 + e2)
    yc = ((e0 * inv) * o0_ref[0, 0].astype(f32) + (e1 * inv) * o1 + (e2 * inv) * o2).astype(jnp.bfloat16)
    merged = None
    for i, y in enumerate((ya_ref[...], yb_ref[...], yc, yd_ref[...])):
        gate = jax.nn.sigmoid(zg_ref[:, i * D_MODEL:(i + 1) * D_MODEL].astype(f32))
        term = gate * jnp.dot(y, wb_ref[i], preferred_element_type=f32)
        merged = term if merged is None else merged + term
    out_ref[...] = x_ref[...] + jnp.dot(merged.astype(jnp.bfloat16), wo_ref[...], preferred_element_type=f32)


def merge_project(x, ya, yb, dil_o, dil_lse, yd, z, w_branch, w_out, seq):
    n = x.shape[0]
    tm = MERGE_TM
    n_t = seq // tm
    width = 4 * LANES
    row = lambda i: (i, 0)
    bspec = pl.BlockSpec((tm, width), row)

    def dspec(dil):
        return pl.BlockSpec((1, dil, tm // dil, width), lambda i: (i // n_t, 0, i % n_t, 0))

    dspecs = [dspec(d) for _, d in DIL_PAIRS]
    return pl.pallas_call(
        _merge_kernel,
        grid=(n // tm,),
        in_specs=[pl.BlockSpec((tm, D_MODEL), row), bspec, bspec, *dspecs, *dspecs, bspec,
                  pl.BlockSpec((tm, N_BRANCHES * D_MODEL), lambda i: (i, CB_ZG * LANES // (N_BRANCHES * D_MODEL))),
                  pl.BlockSpec((N_BRANCHES, width, D_MODEL), lambda i: (0, 0, 0)),
                  pl.BlockSpec((D_MODEL, D_MODEL), lambda i: (0, 0))],
        out_specs=pl.BlockSpec((tm, D_MODEL), row),
        out_shape=jax.ShapeDtypeStruct((n, D_MODEL), jnp.float32),
        scratch_shapes=[pltpu.VMEM((4 * width // LANES, tm, LANES), jnp.float32)],
        compiler_params=_cparams("arbitrary"),
        name="merge_project",
    )(x, ya, yb, *dil_o, *dil_lse, yd, z, w_branch, w_out)


MOE_TM = 512
MOE_TT = 256
MOE_W_SMALL = 64
MOE_ROWS = 1024
MOE_ALIGN = 8
MOE_XW = D_MODEL + LANES


def _router_kernel(x_ref, g_ref, w_ref, h_ref, aff_ref):
    x = x_ref[...]
    y = x * lax.rsqrt(jnp.mean(x * x, axis=-1, keepdims=True) + EPS)
    h = (y * g_ref[...]).astype(jnp.bfloat16)
    h_ref[...] = h
    logits = jnp.dot(h, w_ref[...], preferred_element_type=jnp.float32)
    lane = lax.broadcasted_iota(jnp.int32, logits.shape, 1)
    logits = jnp.where(lane < N_EXPERTS, logits, NEG_INF)
    e = jnp.exp(logits - jnp.max(logits, axis=-1, keepdims=True))
    aff_t = (e / jnp.sum(e, axis=-1, keepdims=True)).T
    for c in range(x.shape[0] // MOE_TT):
        aff_ref[c] = aff_t[:N_EXPERTS, c * MOE_TT:(c + 1) * MOE_TT]


def moe_router(x, norm_g, w_router):
    n, d = x.shape
    tm = 2 * MOE_TT
    w = jnp.pad(w_router.astype(jnp.bfloat16), ((0, 0), (0, LANES - N_EXPERTS)))
    return pl.pallas_call(
        _router_kernel,
        grid=(n // tm,),
        in_specs=[pl.BlockSpec((tm, d), lambda i: (i, 0)), pl.BlockSpec((1, d), lambda i: (0, 0)),
                  pl.BlockSpec((d, LANES), lambda i: (0, 0))],
        out_specs=[pl.BlockSpec((tm, d), lambda i: (i, 0)),
                   pl.BlockSpec((tm // MOE_TT, N_EXPERTS, MOE_TT), lambda i: (i, 0, 0))],
        out_shape=[jax.ShapeDtypeStruct((n, d), jnp.bfloat16),
                   jax.ShapeDtypeStruct((n // MOE_TT, N_EXPERTS, MOE_TT), jnp.float32)],
        compiler_params=_cparams("arbitrary"),
        name="moe_router",
    )(x, norm_g.reshape(1, d), w)


def _affinity_bits(a):
    return lax.bitcast_convert_type(a, jnp.int32)


def _threshold_kernel(aff_ref, thr_ref, need_ref, *, cap, n_tiles):
    def count(pred, thr):
        def tile(c, acc):
            return acc + jnp.where(pred(_affinity_bits(aff_ref[c]), thr), 1.0, 0.0)
        acc = lax.fori_loop(0, n_tiles, tile, jnp.zeros((N_EXPERTS, MOE_TT), jnp.float32), unroll=8)
        return jnp.sum(acc, axis=1, keepdims=True)

    def bit(i, thr):
        cand = thr | jnp.left_shift(jnp.int32(1), 30 - i)
        return jnp.where(count(lambda b, t: b >= t, cand) >= cap, cand, thr)

    thr = lax.fori_loop(0, 31, bit, jnp.zeros((N_EXPERTS, 1), jnp.int32))
    thr_ref[...] = jnp.broadcast_to(thr, thr_ref.shape)
    need_ref[...] = jnp.broadcast_to(cap - count(lambda b, t: b > t, thr), need_ref.shape)


def moe_threshold(aff, cap):
    n_tiles = aff.shape[0]
    full = pl.BlockSpec((N_EXPERTS, LANES), lambda i: (0, 0))
    return pl.pallas_call(
        functools.partial(_threshold_kernel, cap=cap, n_tiles=n_tiles),
        grid=(1,),
        in_specs=[pl.BlockSpec(aff.shape, lambda i: (0, 0, 0))],
        out_specs=[full, full],
        out_shape=[jax.ShapeDtypeStruct((N_EXPERTS, LANES), jnp.int32),
                   jax.ShapeDtypeStruct((N_EXPERTS, LANES), jnp.float32)],
        compiler_params=_cparams("arbitrary"),
        name="moe_threshold",
    )(aff)


def _assign_kernel(aff_ref, thr_ref, need_ref, codet_ref, coden_ref, start_ref, total_ref, ties_ref, run_ref):
    @pl.when(pl.program_id(0) == 0)
    def _():
        ties_ref[...] = jnp.zeros_like(ties_ref)
        run_ref[...] = jnp.zeros_like(run_ref)

    f32 = jnp.float32
    bits = _affinity_bits(aff_ref[0])
    thr = thr_ref[:, :1]
    ii = lax.broadcasted_iota(jnp.int32, (MOE_TT, MOE_TT), 0)
    jj = lax.broadcasted_iota(jnp.int32, (MOE_TT, MOE_TT), 1)
    tri = jnp.where(ii <= jj, 1.0, 0.0).astype(jnp.bfloat16)
    eq = bits == thr
    eq_f = jnp.where(eq, 1.0, 0.0)
    tie_rank = jnp.dot(eq_f.astype(jnp.bfloat16), tri, preferred_element_type=f32) + ties_ref[:, :1]
    sel = jnp.logical_or(bits > thr, jnp.logical_and(eq, tie_rank <= need_ref[:, :1]))
    sel_f = jnp.where(sel, 1.0, 0.0)
    incl = jnp.dot(sel_f.astype(jnp.bfloat16), tri, preferred_element_type=f32)
    code = jnp.where(sel, incl - 1.0, -1.0)
    codet_ref[0] = code.astype(jnp.int32)
    padded = jnp.concatenate([code, jnp.zeros((LANES - N_EXPERTS, MOE_TT), f32)], axis=0)
    coden_ref[...] = padded.T.astype(jnp.int32)
    start_ref[0] = run_ref[...].astype(jnp.int32)
    taken = jnp.sum(sel_f, axis=1, keepdims=True)
    run_ref[...] = run_ref[...] + jnp.floor((taken + (MOE_ALIGN - 1)) * (1.0 / MOE_ALIGN)) * MOE_ALIGN
    total_ref[...] = run_ref[...].astype(jnp.int32)
    ties_ref[...] = ties_ref[...] + jnp.sum(eq_f, axis=1, keepdims=True)


def moe_assign(aff, thr, need):
    n_tiles = aff.shape[0]
    tile = pl.BlockSpec((1, N_EXPERTS, MOE_TT), lambda c: (c, 0, 0))
    full = pl.BlockSpec((N_EXPERTS, LANES), lambda c: (0, 0))
    return pl.pallas_call(
        _assign_kernel,
        grid=(n_tiles,),
        in_specs=[tile, full, full],
        out_specs=[tile, pl.BlockSpec((MOE_TT, LANES), lambda c: (c, 0)),
                   pl.BlockSpec((1, N_EXPERTS, LANES), lambda c: (c, 0, 0)), full],
        out_shape=[jax.ShapeDtypeStruct((n_tiles, N_EXPERTS, MOE_TT), jnp.int32),
                   jax.ShapeDtypeStruct((n_tiles * MOE_TT, LANES), jnp.int32),
                   jax.ShapeDtypeStruct((n_tiles, N_EXPERTS, LANES), jnp.int32),
                   jax.ShapeDtypeStruct((N_EXPERTS, LANES), jnp.int32)],
        scratch_shapes=[pltpu.VMEM((N_EXPERTS, LANES), jnp.float32)] * 2,
        compiler_params=_cparams("arbitrary"),
        name="moe_assign",
    )(aff, thr, need)


def _dispatch_kernel(start_ref, big_ref, h_ref, code_ref, aff_ref, init_hbm, x_hbm, buf_ref, sem, *, n_tiles):
    del init_hbm
    c = pl.program_id(0)
    cur = c % 2
    f32 = jnp.float32
    h = h_ref[...]
    code = code_ref[0]
    gate = aff_ref[0]

    def copies(tile, slot, first, width, group):
        return [pltpu.make_async_copy(
            buf_ref.at[slot, pl.ds(k * width, width)],
            x_hbm.at[first + k, pl.ds(_slot_start(start_ref, tile, first + k), width)],
            sem.at[slot, k]) for k in range(group)]

    def stage(first, width, group):
        slot = lax.broadcasted_iota(jnp.int32, (width, MOE_TT), 0)
        onehots = []
        for k in range(group):
            e = first + k
            hit = slot == code[e:e + 1, :]
            onehots.append(jnp.where(hit, 1.0, 0.0).astype(jnp.bfloat16))
            gwin = jnp.sum(jnp.where(hit, gate[e:e + 1, :], 0.0), axis=1, keepdims=True)
            buf_ref[cur, k * width:(k + 1) * width, D_MODEL:] = jnp.broadcast_to(gwin, (width, LANES))
        buf_ref[cur, :, :D_MODEL] = jnp.dot(jnp.concatenate(onehots, axis=0), h, preferred_element_type=f32)

    def wait_previous_small():
        prev = jnp.maximum(c - 1, 0)

        @pl.when(jnp.logical_and(c > 0, big_ref[prev] == 0))
        def _():
            for cp in copies(prev, 1 - cur, 0, MOE_W_SMALL, N_EXPERTS):
                cp.wait()

    @pl.when(big_ref[c] == 0)
    def _():
        stage(0, MOE_W_SMALL, N_EXPERTS)
        wait_previous_small()
        mine = copies(c, cur, 0, MOE_W_SMALL, N_EXPERTS)
        for cp in mine:
            cp.start()

        @pl.when(c == n_tiles - 1)
        def _():
            for cp in mine:
                cp.wait()

    @pl.when(big_ref[c] != 0)
    def _():
        wait_previous_small()
        group = MOE_ROWS // MOE_TT
        for first in range(0, N_EXPERTS, group):
            stage(first, MOE_TT, group)
            mine = copies(c, cur, first, MOE_TT, group)
            for cp in mine:
                cp.start()
            for cp in mine:
                cp.wait()


def _slot_start(start_ref, tile, expert):
    return pl.multiple_of(start_ref[tile * N_EXPERTS + expert], MOE_ALIGN)


def moe_dispatch(starts, big, h, code_t, aff, slots):
    n, d = h.shape
    n_tiles = n // MOE_TT
    tile = pl.BlockSpec((1, N_EXPERTS, MOE_TT), lambda c, s, b: (c, 0, 0))
    shape = (N_EXPERTS, slots, MOE_XW)
    return pl.pallas_call(
        functools.partial(_dispatch_kernel, n_tiles=n_tiles),
        grid_spec=pltpu.PrefetchScalarGridSpec(
            num_scalar_prefetch=2,
            grid=(n_tiles,),
            in_specs=[pl.BlockSpec((MOE_TT, d), lambda c, s, b: (c, 0)), tile, tile,
                      pl.BlockSpec(memory_space=pl.ANY)],
            out_specs=pl.BlockSpec(memory_space=pl.ANY),
            scratch_shapes=[pltpu.VMEM((2, MOE_ROWS, MOE_XW), jnp.float32),
                            pltpu.SemaphoreType.DMA((2, MOE_ROWS // MOE_W_SMALL))],
        ),
        out_shape=jax.ShapeDtypeStruct(shape, jnp.float32),
        input_output_aliases={5: 0},
        compiler_params=_cparams("arbitrary"),
        name="moe_dispatch",
    )(starts, big, h, code_t, aff, jnp.zeros(shape, jnp.float32))


def _expert_kernel(total_ref, x_ref, wg_ref, wu_ref, wd_ref, o_ref):
    f32 = jnp.float32
    used = pl.program_id(1) * MOE_TM < total_ref[pl.program_id(0)]

    @pl.when(used)
    def _():
        x = x_ref[0, :, :D_MODEL].astype(jnp.bfloat16)
        half = D_EXPERT // 2
        acc = None
        for lo in (0, half):
            a = jnp.dot(x, wg_ref[0, :, lo:lo + half], preferred_element_type=f32)
            u = jnp.dot(x, wu_ref[0, :, lo:lo + half], preferred_element_type=f32)
            he = (a * jax.nn.sigmoid(a) * u).astype(jnp.bfloat16)
            part = jnp.dot(he, wd_ref[0, lo:lo + half, :], preferred_element_type=f32)
            acc = part if acc is None else acc + part
        o_ref[0] = acc * x_ref[0, :, D_MODEL:D_MODEL + 1]

    @pl.when(jnp.logical_not(used))
    def _():
        o_ref[0] = jnp.zeros_like(o_ref[0])


def expert_ffn(total, xd, w_gate, w_up, w_down):
    e, slots, _ = xd.shape
    d = D_MODEL
    tm = MOE_TM
    wspec = pl.BlockSpec((1, d, D_EXPERT), lambda i, j, t: (i, 0, 0))
    return pl.pallas_call(
        _expert_kernel,
        grid_spec=pltpu.PrefetchScalarGridSpec(
            num_scalar_prefetch=1,
            grid=(e, slots // tm),
            in_specs=[pl.BlockSpec((1, tm, MOE_XW), lambda i, j, t: (i, j, 0)),
                      wspec, wspec,
                      pl.BlockSpec((1, D_EXPERT, d), lambda i, j, t: (i, 0, 0))],
            out_specs=pl.BlockSpec((1, tm, d), lambda i, j, t: (i, j, 0)),
        ),
        out_shape=jax.ShapeDtypeStruct((e, slots, d), jnp.float32),
        compiler_params=_cparams("arbitrary", "arbitrary"),
        name="expert_ffn",
    )(total, xd, w_gate, w_up, w_down)


def _combine_kernel(start_ref, big_ref, x_ref, code_ref, y_hbm, o_ref, buf_ref, sem, *, n_tiles):
    c = pl.program_id(0)
    cur = c % 2
    f32 = jnp.float32
    bf = jnp.bfloat16
    code = code_ref[...]
    o_ref[...] = x_ref[...]

    def copies(tile, slot, first, width, group):
        return [pltpu.make_async_copy(
            y_hbm.at[first + k, pl.ds(_slot_start(start_ref, tile, first + k), width)],
            buf_ref.at[slot, pl.ds(k * width, width)],
            sem.at[slot, k]) for k in range(group)]

    def onehot(first, width, group):
        pieces = []
        if width < LANES:
            lane = lax.broadcasted_iota(jnp.int32, (MOE_TT, LANES), 1)
            for k in range(0, group, 2):
                e = first + k
                target = jnp.where(lane < width, code[:, e:e + 1], code[:, e + 1:e + 2] + width)
                pieces.append(jnp.where(target == lane, 1.0, 0.0).astype(bf))
        else:
            lane = lax.broadcasted_iota(jnp.int32, (MOE_TT, width), 1)
            for k in range(group):
                e = first + k
                pieces.append(jnp.where(code[:, e:e + 1] == lane, 1.0, 0.0).astype(bf))
        return jnp.concatenate(pieces, axis=1)

    def accumulate(oh):
        y = buf_ref[cur]
        y_hi = y.astype(bf)
        y_lo = (y - y_hi.astype(f32)).astype(bf)
        o_ref[...] += (jnp.dot(oh, y_hi, preferred_element_type=f32)
                       + jnp.dot(oh, y_lo, preferred_element_type=f32))

    def prefetch_next():
        nxt = jnp.minimum(c + 1, n_tiles - 1)

        @pl.when(jnp.logical_and(c + 1 < n_tiles, big_ref[nxt] == 0))
        def _():
            for cp in copies(nxt, 1 - cur, 0, MOE_W_SMALL, N_EXPERTS):
                cp.start()

    @pl.when(big_ref[c] == 0)
    def _():
        mine = copies(c, cur, 0, MOE_W_SMALL, N_EXPERTS)

        @pl.when(c == 0)
        def _():
            for cp in mine:
                cp.start()

        prefetch_next()
        oh = onehot(0, MOE_W_SMALL, N_EXPERTS)
        for cp in mine:
            cp.wait()
        accumulate(oh)

    @pl.when(big_ref[c] != 0)
    def _():
        prefetch_next()
        group = MOE_ROWS // MOE_TT
        for first in range(0, N_EXPERTS, group):
            mine = copies(c, cur, first, MOE_TT, group)
            for cp in mine:
                cp.start()
            oh = onehot(first, MOE_TT, group)
            for cp in mine:
                cp.wait()
            accumulate(oh)


def moe_combine(starts, big, x, code_n, y):
    n, d = x.shape
    row = pl.BlockSpec((MOE_TT, d), lambda c, s, b: (c, 0))
    return pl.pallas_call(
        functools.partial(_combine_kernel, n_tiles=n // MOE_TT),
        grid_spec=pltpu.PrefetchScalarGridSpec(
            num_scalar_prefetch=2,
            grid=(n // MOE_TT,),
            in_specs=[row, pl.BlockSpec((MOE_TT, LANES), lambda c, s, b: (c, 0)),
                      pl.BlockSpec(memory_space=pl.ANY)],
            out_specs=row,
            scratch_shapes=[pltpu.VMEM((2, MOE_ROWS, d), jnp.float32),
                            pltpu.SemaphoreType.DMA((2, MOE_ROWS // MOE_W_SMALL))],
        ),
        out_shape=jax.ShapeDtypeStruct((n, d), jnp.float32),
        compiler_params=_cparams("arbitrary"),
        name="moe_combine",
    )(starts, big, x, code_n, y)


def ec_moe(x, norm_g, w_router, w_gate, w_up, w_down):
    n_tok, _ = x.shape
    cap = EC_FACTOR * n_tok // N_EXPERTS
    h, aff = moe_router(x, norm_g, w_router)
    thr, need = moe_threshold(aff, cap)
    code_t, code_n, starts, total = moe_assign(aff, thr, need)
    starts = starts[:, :, 0]
    total = total[:, 0]
    ends = jnp.concatenate([starts[1:], total[None]], axis=0)
    big = (jnp.max(ends - starts, axis=1) > MOE_W_SMALL).astype(jnp.int32)
    starts = starts.reshape(-1)
    n_tiles = n_tok // MOE_TT
    slots = -(-(cap + MOE_ALIGN * n_tiles + MOE_TT) // MOE_TM) * MOE_TM
    xd = moe_dispatch(starts, big, h, code_t, aff, slots)
    y = expert_ffn(total, xd, w_gate, w_up, w_down)
    return moe_combine(starts, big, x, code_n, y)


def split_in_proj(w_in):
    w = w_in.astype(jnp.bfloat16)
    sizes = (LRU_WIDTH, LRU_WIDTH, MLA_Q_RANK, MLA_KV_RANK, MLA_ROPE,
             DIL_HEADS * DIL_HEAD_DIM, DIL_HEADS * DIL_HEAD_DIM, DIL_HEADS * DIL_HEAD_DIM,
             RET_HEADS * RET_QK, RET_HEADS * RET_QK, RET_HEADS * RET_V, RET_HEADS * RET_V,
             N_BRANCHES * D_MODEL)
    parts, off = [], 0
    for s in sizes:
        parts.append(w[:, off:off + s])
        off += s
    xa, ga, cq, ckv, kr, dq, dk, dv, rq, rk, rv, rg, zg = parts
    gw = DIL_GROUP_WIDTH
    grp = lambda a, g: a[:, g * gw:(g + 1) * gw]
    zeros = jnp.zeros((w.shape[0], Z_PAD), w.dtype)
    main = jnp.concatenate([zg, xa, ga, cq, ckv, kr, zeros, grp(dq, 0), grp(dk, 0), grp(dv, 0),
                            rq, rk, rv, rg], axis=1)
    dil = [jnp.concatenate([grp(dq, g), grp(dk, g), grp(dv, g)], axis=1) for g in range(1, len(DIL_PAIRS))]
    return main, dil


def mixer(x, batch, seq, norm_g, w_in, conv_w, conv_b, lru_gate_w, lru_gate_b, lru_lambda, mla_q_norm,
          mla_kv_norm, w_uq, w_ukv, ret_decay, w_branch, w_out, mla_tables, ret_tables):
    n = batch * seq
    w_main, w_dil = split_in_proj(w_in)
    z = norm_matmul(x, norm_g, w_main, jnp.bfloat16, min(1024, n), 1024)
    wg, gb = lru_gate_dense(lru_gate_w, lru_gate_b)
    h_fwd = lru_direction(z, conv_w, conv_b, wg, gb, lru_lambda, batch, seq, False)
    ya = lru_direction(z, conv_w, conv_b, wg, gb, lru_lambda, batch, seq, True, h_fwd)
    q, k, v = mla_project(z, mla_q_norm, mla_kv_norm, mla_weights(w_uq, w_ukv), mla_tables, batch, seq)
    yb = mla_attention(q, k, v, batch, seq)
    dil = [band_group(z.reshape(batch, 1, seq, Z_COLS), 0, 1, batch, seq, (CB_DQ, CB_DK, CB_DV))]
    pairs = DIL_HEADS_PER_GROUP // 2
    for g in range(1, len(DIL_PAIRS)):
        d = DIL_PAIRS[g][1]
        zd = norm_matmul_dil(x, norm_g, w_dil[g - 1], d, batch, seq, min(1024, seq))
        dil.append(band_group(zd, g, d, batch, seq, (0, pairs, 2 * pairs)))
    yd = retention(z, ret_decay, ret_tables, batch, seq)
    return merge_project(x, ya, yb, [o for o, _ in dil], [l for _, l in dil], yd, z,
                         w_branch.astype(jnp.bfloat16), w_out.astype(jnp.bfloat16), seq)


def trunk(x, norm_mix, w_in, conv_w, conv_b, lru_gate_w, lru_gate_b, lru_lambda, mla_q_norm, mla_kv_norm,
          w_uq, w_ukv, ret_decay, w_branch, w_out, norm_ffn, w_router, w_gate, w_up, w_down, norm_final):
    batch, seq, d = x.shape
    x = x.reshape(batch * seq, d)
    mla_tables = mla_rope_tables(seq)
    ret_tables = ret_rope_tables(seq)
    bf = jnp.bfloat16
    for l in range(norm_mix.shape[0]):
        x = mixer(x, batch, seq, norm_mix[l], w_in[l], conv_w[l], conv_b[l], lru_gate_w[l], lru_gate_b[l],
                  lru_lambda[l], mla_q_norm[l], mla_kv_norm[l], w_uq[l], w_ukv[l], ret_decay[l],
                  w_branch[l], w_out[l], mla_tables, ret_tables)
        x = ec_moe(x, norm_ffn[l], w_router[l], w_gate[l].astype(bf), w_up[l].astype(bf), w_down[l].astype(bf))
    return rmsnorm_pallas(x, norm_final, jnp.float32, min(1024, batch * seq)).reshape(batch, seq, d)


def kernel(x_prompt, x_sample, norm_mix, w_in, conv_w, conv_b, lru_gate_w, lru_gate_b, lru_lambda,
           mla_q_norm, mla_kv_norm, w_uq, w_ukv, ret_decay, w_branch, w_out, norm_ffn, w_router,
           w_gate, w_up, w_down, norm_final):
    args = (norm_mix, w_in, conv_w, conv_b, lru_gate_w, lru_gate_b, lru_lambda, mla_q_norm, mla_kv_norm,
            w_uq, w_ukv, ret_decay, w_branch, w_out, norm_ffn, w_router, w_gate, w_up, w_down, norm_final)
    return trunk(x_prompt, *args), trunk(x_sample, *args)
```
